```python
import math
import jax, jax.numpy as jnp
from jax import lax
import numpy as np

D_MODEL = 1024
BATCH = 8
SEQ = 2048
DEPTH = 4
DEC_BATCH = 32
DEC_SEQ = 1
PAST_LEN = 16384
PAGE_SIZE = 128

D_MIX = D_MODEL
SSM_WIDTH = D_MIX // 4
SSM_GROUP = 16
SSM_GROUPS = SSM_WIDTH // SSM_GROUP
SSM_STATE = 64
HG_WIDTH = D_MIX // 4
HG_HEADS = 4
HG_DK = HG_WIDTH // HG_HEADS
HG_DV = HG_WIDTH // HG_HEADS
HG_CHUNK = 64
LB_FLOOR = 1e-30
MLA_WIDTH = D_MIX - SSM_WIDTH - HG_WIDTH
MLA_HEADS = 8
MLA_V_DIM = MLA_WIDTH // MLA_HEADS
MLA_NOPE_DIM = 64
MLA_ROPE_DIM = 32
MLA_Q_LORA = 384
MLA_KV_LORA = 256
MLA_SCALE = (MLA_NOPE_DIM + MLA_ROPE_DIM) ** -0.5
ROPE_THETA = 10000.0
ATTN_BLOCK = 128
NEG_MASK = -1e30
N_IN = SSM_WIDTH + 4 * HG_WIDTH + MLA_Q_LORA + MLA_KV_LORA + MLA_ROPE_DIM
D_FF = -(-8 * D_MODEL // (3 * 128)) * 128
ALPHA = (2 * DEPTH) ** 0.25
BETA = (8 * DEPTH) ** -0.25

kernel_name = "hymba_s5_hgrn2_mla_macaron_deepnorm_adaln_step"


def _layer_norm(x, g, b, eps=1e-5):
    xf = x.astype(jnp.float32)
    mu = jnp.mean(xf, -1, keepdims=True)
    var = jnp.mean(jnp.square(xf - mu), -1, keepdims=True)
    return ((xf - mu) * lax.rsqrt(var + eps) * g + b).astype(x.dtype)


def _rms_norm(x, g, eps=1e-6):
    xf = x.astype(jnp.float32)
    return (xf * lax.rsqrt(jnp.mean(xf * xf, -1, keepdims=True) + eps) * g).astype(x.dtype)


def _swiglu(h, w_gu, w_down):
    g, u = jnp.split(h @ w_gu, 2, axis=-1)
    return (jax.nn.silu(g) * u) @ w_down


def _rope_angles(pos):
    inv = 1.0 / (ROPE_THETA ** (jnp.arange(0, MLA_ROPE_DIM, 2, dtype=jnp.float32) / MLA_ROPE_DIM))
    ang = pos.astype(jnp.float32)[:, None] * inv[None, :]
    return jnp.cos(ang), jnp.sin(ang)


def _rope(x, cos, sin):
    x1, x2 = jnp.split(x, 2, axis=-1)
    return jnp.concatenate([x1 * cos - x2 * sin, x2 * cos + x1 * sin], -1).astype(x.dtype)


def _s5(u, lam_re, lam_im, b_re, b_im, c_re, c_im, d, log_step, x0_re, x0_im):
    f32 = jnp.float32
    lam_re, lam_im = lam_re.astype(f32), lam_im.astype(f32)
    step = jnp.exp(log_step.astype(f32))[:, None]
    mag = jnp.exp(lam_re * step)
    ab_re, ab_im = mag * jnp.cos(lam_im * step), mag * jnp.sin(lam_im * step)
    den = lam_re * lam_re + lam_im * lam_im
    nr = ab_re - 1.0
    coef_re = (nr * lam_re + ab_im * lam_im) / den
    coef_im = (ab_im * lam_re - nr * lam_im) / den
    b_re, b_im = b_re.astype(f32), b_im.astype(f32)
    bb_re = coef_re[..., None] * b_re - coef_im[..., None] * b_im
    bb_im = coef_re[..., None] * b_im + coef_im[..., None] * b_re
    uf = u.astype(f32)
    bu_re = jnp.einsum('blgh,gph->blgp', uf, bb_re)
    bu_im = jnp.einsum('blgh,gph->blgp', uf, bb_im)
    a_re = jnp.broadcast_to(ab_re, bu_re.shape)
    a_im = jnp.broadcast_to(ab_im, bu_im.shape)

    def combine(e1, e2):
        a1r, a1i, b1r, b1i = e1
        a2r, a2i, b2r, b2i = e2
        return (a2r * a1r - a2i * a1i, a2r * a1i + a2i * a1r,
                a2r * b1r - a2i * b1i + b2r, a2r * b1i + a2i * b1r + b2i)

    pw_re, pw_im, xr, xi = lax.associative_scan(combine, (a_re, a_im, bu_re, bu_im), axis=1)
    x0r, x0i = x0_re.astype(f32)[:, None], x0_im.astype(f32)[:, None]
    xr = xr + pw_re * x0r - pw_im * x0i
    xi = xi + pw_re * x0i + pw_im * x0r
    y = (jnp.einsum('blgp,ghp->blgh', xr, c_re.astype(f32))
         - jnp.einsum('blgp,ghp->blgh', xi, c_im.astype(f32)) + d.astype(f32) * uf)
    return y.astype(u.dtype), xr[:, -1], xi[:, -1]


def _hgrn2(q, log_f, k, v, s0):
    f32 = jnp.float32
    B, L, Hh, _ = q.shape
    n = -(-L // HG_CHUNK)
    pad = n * HG_CHUNK - L

    def blocks(t):
        t = jnp.pad(t.astype(f32), ((0, 0), (0, pad), (0, 0), (0, 0)))
        return jnp.moveaxis(t.reshape(B, n, HG_CHUNK, Hh, t.shape[-1]), 1, 0)

    mask = jnp.tril(jnp.ones((HG_CHUNK, HG_CHUNK), bool))[None, :, :, None, None]

    def step(S, inp):
        qc, lfc, kc, vc = inp
        b = jnp.cumsum(lfc, axis=1)
        o = jnp.einsum('bthk,bhkv->bthv', qc * jnp.exp(b), S)
        diff = b[:, :, None] - b[:, None]
        dec = jnp.where(mask, jnp.exp(jnp.where(mask, diff, 0.0)), 0.0)
        a = jnp.einsum('bthk,btshk,bshk->bhts', qc, dec, kc)
        o = o + jnp.einsum('bhts,bshv->bthv', a, vc)
        bl = b[:, -1]
        S = jnp.exp(bl)[..., None] * S + jnp.einsum('bshk,bshv->bhkv', kc * jnp.exp(bl[:, None] - b), vc)
        return S, o

    S, o = lax.scan(step, s0.astype(f32), (blocks(q), blocks(log_f), blocks(k), blocks(v)))
    o = jnp.moveaxis(o, 0, 1).reshape(B, n * HG_CHUNK, Hh, -1)[:, :L]
    return o.astype(v.dtype), S


def _mla_prompt(q_nope, q_pe, ckv, kpe, w_uk, w_uv):
    B, L, Hm, _ = q_nope.shape
    k_nope = jnp.einsum('blc,chd->blhd', ckv, w_uk)
    v = jnp.einsum('blc,chd->blhd', ckv, w_uv)
    nb = L // ATTN_BLOCK

    def to_blocks(t):
        return jnp.moveaxis(t.reshape(B, nb, ATTN_BLOCK, *t.shape[2:]), 1, 0)

    kpos = jnp.arange(L)
    qpos = kpos.reshape(nb, ATTN_BLOCK)

    def block(args):
        qn, qp, qi = args
        s = (jnp.einsum('bqhd,bkhd->bhqk', qn, k_nope)
             + jnp.einsum('bqhr,bkr->bhqk', qp, kpe)).astype(jnp.float32) * MLA_SCALE
        s = jnp.where(kpos[None, :] <= qi[:, None], s, NEG_MASK)
        p = jax.nn.softmax(s, axis=-1).astype(v.dtype)
        return jnp.einsum('bhqk,bkhd->bqhd', p, v)

    o = lax.map(block, (to_blocks(q_nope), to_blocks(q_pe), qpos))
    return jnp.moveaxis(o, 0, 1).reshape(B, L, Hm * MLA_V_DIM)


def _mla_sample(q_nope, q_pe, ckv, kpe, w_uk, w_uv, past_ckv, past_kpe):
    B, L, Hm, _ = q_nope.shape
    q_abs = jnp.einsum('bqhd,chd->bqhc', q_nope, w_uk)
    s_past = (jnp.einsum('bqhc,bkc->bhqk', q_abs, past_ckv)
              + jnp.einsum('bqhr,bkr->bhqk', q_pe, past_kpe)).astype(jnp.float32) * MLA_SCALE
    s_new = (jnp.einsum('bqhc,bkc->bhqk', q_abs, ckv)
             + jnp.einsum('bqhr,bkr->bhqk', q_pe, kpe)).astype(jnp.float32) * MLA_SCALE
    s_new = jnp.where(jnp.tril(jnp.ones((L, L), bool)), s_new, NEG_MASK)
    p = jax.nn.softmax(jnp.concatenate([s_past, s_new], -1), axis=-1).astype(ckv.dtype)
    n_past = past_ckv.shape[1]
    o_lat = (jnp.einsum('bhqk,bkc->bqhc', p[..., :n_past], past_ckv)
             + jnp.einsum('bhqk,bkc->bqhc', p[..., n_past:], ckv))
    return jnp.einsum('bqhc,chd->bqhd', o_lat, w_uv).reshape(B, L, Hm * MLA_V_DIM)


def _run_group(x, c, pos, past, ssm0_re, ssm0_im, hg0, lb, W):
    B, L, _ = x.shape
    cos, sin = _rope_angles(pos)
    offs = np.cumsum([SSM_WIDTH, HG_WIDTH, HG_WIDTH, HG_WIDTH, HG_WIDTH, MLA_Q_LORA, MLA_KV_LORA]).tolist()
    ckv_l, kpe_l, sre_l, sim_l, hg_l = [], [], [], [], []
    for l in range(DEPTH):
        mod = (jax.nn.silu(c) @ W['w_ada'][l] + W['b_ada'][l]).reshape(B, 1, 9, D_MODEL)

        def modulate(t, i):
            return t * (1.0 + mod[:, :, 3 * i + 1]) + mod[:, :, 3 * i]

        def gate(i):
            return 1.0 + mod[:, :, 3 * i + 2]

        f = _swiglu(modulate(x, 0), W['ffn_w_gu'][l, 0], W['ffn_w_down'][l, 0])
        x = _layer_norm(ALPHA * x + 0.5 * gate(0) * f, W['ln_g'][l, 0], W['ln_b'][l, 0])

        z = modulate(x, 1) @ W['w_in'][l]
        u, zf, zq, zi, zg, zcq, zckv, zkpe = jnp.split(z, offs, axis=-1)

        y, sre, sim = _s5(u.reshape(B, L, SSM_GROUPS, SSM_GROUP),
                          W['ssm_lambda_re'][l], W['ssm_lambda_im'][l], W['ssm_b_re'][l], W['ssm_b_im'][l],
                          W['ssm_c_re'][l], W['ssm_c_im'][l], W['ssm_d'][l].reshape(SSM_GROUPS, SSM_GROUP),
                          W['ssm_log_step'][l], ssm0_re[l], ssm0_im[l])
        ga, gb = jnp.split(jax.nn.gelu(y.reshape(B, L, SSM_WIDTH)) @ W['ssm_w_glu'][l], 2, axis=-1)
        o_ssm = _rms_norm(ga * jax.nn.sigmoid(gb), W['norm_ssm'][l])

        lbl = lb[l].reshape(HG_HEADS, HG_DK)
        zf32 = zf.astype(jnp.float32).reshape(B, L, HG_HEADS, HG_DK)
        log_f = jnp.logaddexp(jnp.log(jnp.maximum(lbl, LB_FLOOR)),
                              jnp.log1p(-lbl) + jax.nn.log_sigmoid(zf32))
        k_in = (1.0 - lbl) * jax.nn.sigmoid(-zf32)
        o_h, S = _hgrn2(zq.reshape(B, L, HG_HEADS, HG_DK), log_f, k_in,
                        zi.reshape(B, L, HG_HEADS, HG_DV), hg0[l])
        o_hg = (_rms_norm(o_h, W['norm_hgrn'][l].reshape(HG_HEADS, HG_DV))
                * jax.nn.silu(zg.reshape(B, L, HG_HEADS, HG_DV))).reshape(B, L, HG_WIDTH)

        q = (_rms_norm(zcq, W['mla_q_norm'][l]) @ W['mla_w_uq'][l]).reshape(
            B, L, MLA_HEADS, MLA_NOPE_DIM + MLA_ROPE_DIM)
        q_nope = q[..., :MLA_NOPE_DIM]
        q_pe = _rope(q[..., MLA_NOPE_DIM:], cos[:, None], sin[:, None])
        ckv = _rms_norm(zckv, W['mla_kv_norm'][l])
        kpe = _rope(zkpe, cos, sin)
        w_uk = W['mla_w_uk'][l].reshape(MLA_KV_LORA, MLA_HEADS, MLA_NOPE_DIM)
        w_uv = W['mla_w_uv'][l].reshape(MLA_KV_LORA, MLA_HEADS, MLA_V_DIM)
        if past is None:
            o_m = _mla_prompt(q_nope, q_pe, ckv, kpe, w_uk, w_uv)
        else:
            cache_ckv, cache_kpe, page_table = past
            past_ckv = cache_ckv[l][page_table].reshape(B, -1, MLA_KV_LORA)
            past_kpe = cache_kpe[l][page_table].reshape(B, -1, MLA_ROPE_DIM)
            o_m = _mla_sample(q_nope, q_pe, ckv, kpe, w_uk, w_uv, past_ckv, past_kpe)
        o_mla = _rms_norm(o_m, W['norm_mla'][l])

        mix = jnp.concatenate([o_ssm, o_hg, o_mla], axis=-1) @ W['w_out'][l]
        x = _layer_norm(ALPHA * x + gate(1) * mix, W['ln_g'][l, 1], W['ln_b'][l, 1])

        f = _swiglu(modulate(x, 2), W['ffn_w_gu'][l, 1], W['ffn_w_down'][l, 1])
        x = _layer_norm(ALPHA * x + 0.5 * gate(2) * f, W['ln_g'][l, 2], W['ln_b'][l, 2])

        ckv_l.append(ckv)
        kpe_l.append(kpe)
        sre_l.append(sre)
        sim_l.append(sim)
        hg_l.append(S)
    return (x, jnp.stack(ckv_l), jnp.stack(kpe_l), jnp.stack(sre_l), jnp.stack(sim_l), jnp.stack(hg_l))


def setup_inputs(seed: int = 0) -> dict:
    key = jax.random.key(seed)
    keys = iter(jax.random.split(key, 48))
    f32 = jnp.float32

    def nrm(shape, scale=1.0):
        return jax.random.normal(next(keys), shape, f32) * scale

    n_pages = PAST_LEN // PAGE_SIZE
    n_pool = (DEC_BATCH * n_pages * 5) // 4
    G, P, H = SSM_GROUPS, SSM_STATE, SSM_GROUP
    d_in = D_MODEL ** -0.5
    page_table = jax.random.permutation(next(keys), n_pool)[: DEC_BATCH * n_pages].reshape(
        DEC_BATCH, n_pages).astype(jnp.int32)
    return {
        'x_prompt': nrm((BATCH, SEQ, D_MODEL)),
        'x_sample': nrm((DEC_BATCH, DEC_SEQ, D_MODEL)),
        'cache_kv_latent': nrm((DEPTH, n_pool, PAGE_SIZE, MLA_KV_LORA)),
        'cache_k_rope': nrm((DEPTH, n_pool, PAGE_SIZE, MLA_ROPE_DIM)),
        'state_ssm_re': nrm((DEPTH, DEC_BATCH, G, P)),
        'state_ssm_im': nrm((DEPTH, DEC_BATCH, G, P)),
        'state_hgrn': nrm((DEPTH, DEC_BATCH, HG_HEADS, HG_DK, HG_DV), 0.5),
        'page_table': page_table,
        'c_prompt': nrm((BATCH, D_MODEL)),
        'c_sample': nrm((DEC_BATCH, D_MODEL)),
        'w_ada': nrm((DEPTH, D_MODEL, 9 * D_MODEL), 0.2 * d_in),
        'b_ada': nrm((DEPTH, 9 * D_MODEL), 0.02),
        'ln_g': 1.0 + nrm((DEPTH, 3, D_MODEL), 0.02),
        'ln_b': nrm((DEPTH, 3, D_MODEL), 0.02),
        'ffn_w_gu': nrm((DEPTH, 2, D_MODEL, 2 * D_FF), d_in),
        'ffn_w_down': nrm((DEPTH, 2, D_FF, D_MODEL), BETA * D_FF ** -0.5),
        'w_in': nrm((DEPTH, D_MODEL, N_IN), d_in),
        'w_out': nrm((DEPTH, D_MIX, D_MODEL), BETA * D_MIX ** -0.5),
        'ssm_lambda_re': -0.5 + nrm((DEPTH, G, P), 0.01),
        'ssm_lambda_im': math.pi * jnp.arange(P, dtype=f32) + nrm((DEPTH, G, P), 0.01),
        'ssm_b_re': nrm((DEPTH, G, P, H), (2 * H) ** -0.5),
        'ssm_b_im': nrm((DEPTH, G, P, H), (2 * H) ** -0.5),
        'ssm_c_re': nrm((DEPTH, G, H, P), (2 * P) ** -0.5),
        'ssm_c_im': nrm((DEPTH, G, H, P), (2 * P) ** -0.5),
        'ssm_d': nrm((DEPTH, SSM_WIDTH), 0.5),
        'ssm_log_step': jax.random.uniform(next(keys), (DEPTH, G), f32, math.log(1e-3), math.log(1e-1)),
        'ssm_w_glu': nrm((DEPTH, SSM_WIDTH, 2 * SSM_WIDTH), SSM_WIDTH ** -0.5),
        'norm_ssm': 1.0 + nrm((DEPTH, SSM_WIDTH), 0.02),
        'hgrn_lb_logits': nrm((DEPTH, HG_WIDTH), 0.1),
        'norm_hgrn': 1.0 + nrm((DEPTH, HG_WIDTH), 0.02),
        'mla_q_norm': 1.0 + nrm((DEPTH, MLA_Q_LORA), 0.02),
        'mla_w_uq': nrm((DEPTH, MLA_Q_LORA, MLA_HEADS * (MLA_NOPE_DIM + MLA_ROPE_DIM)), MLA_Q_LORA ** -0.5),
        'mla_kv_norm': 1.0 + nrm((DEPTH, MLA_KV_LORA), 0.02),
        'mla_w_uk': nrm((DEPTH, MLA_KV_LORA, MLA_HEADS * MLA_NOPE_DIM), MLA_KV_LORA ** -0.5),
        'mla_w_uv': nrm((DEPTH, MLA_KV_LORA, MLA_HEADS * MLA_V_DIM), MLA_KV_LORA ** -0.5),
        'norm_mla': 1.0 + nrm((DEPTH, MLA_WIDTH), 0.02),
    }


def reference(x_prompt, x_sample, cache_kv_latent, cache_k_rope, state_ssm_re, state_ssm_im, state_hgrn,
              page_table, c_prompt, c_sample, w_ada, b_ada, ln_g, ln_b, ffn_w_gu, ffn_w_down, w_in, w_out,
              ssm_lambda_re, ssm_lambda_im, ssm_b_re, ssm_b_im, ssm_c_re, ssm_c_im, ssm_d, ssm_log_step,
              ssm_w_glu, norm_ssm, hgrn_lb_logits, norm_hgrn, mla_q_norm, mla_w_uq, mla_kv_norm, mla_w_uk,
              mla_w_uv, norm_mla):
    W = dict(w_ada=w_ada, b_ada=b_ada, ln_g=ln_g, ln_b=ln_b, ffn_w_gu=ffn_w_gu, ffn_w_down=ffn_w_down,
             w_in=w_in, w_out=w_out, ssm_lambda_re=ssm_lambda_re, ssm_lambda_im=ssm_lambda_im,
             ssm_b_re=ssm_b_re, ssm_b_im=ssm_b_im, ssm_c_re=ssm_c_re, ssm_c_im=ssm_c_im, ssm_d=ssm_d,
             ssm_log_step=ssm_log_step, ssm_w_glu=ssm_w_glu, norm_ssm=norm_ssm, norm_hgrn=norm_hgrn,
             mla_q_norm=mla_q_norm, mla_w_uq=mla_w_uq, mla_kv_norm=mla_kv_norm, mla_w_uk=mla_w_uk,
             mla_w_uv=mla_w_uv, norm_mla=norm_mla)
    sm = jax.nn.softmax(hgrn_lb_logits.astype(jnp.float32), axis=0)
    lb = jnp.clip(jnp.cumsum(sm, axis=0) - sm[0:1], 0.0, 1.0 - 1e-6)

    bp = x_prompt.shape[0]
    ssm_zero = jnp.zeros((DEPTH, bp, SSM_GROUPS, SSM_STATE), jnp.float32)
    hg_zero = jnp.zeros((DEPTH, bp, HG_HEADS, HG_DK, HG_DV), jnp.float32)
    pos_p = jnp.arange(x_prompt.shape[1])
    y_prompt, ckv_p, kpe_p, sre_p, sim_p, hg_p = _run_group(
        x_prompt, c_prompt, pos_p, None, ssm_zero, ssm_zero, hg_zero, lb, W)
    pos_s = PAST_LEN + jnp.arange(x_sample.shape[1])
    y_sample, ckv_s, kpe_s, sre_s, sim_s, hg_s = _run_group(
        x_sample, c_sample, pos_s, (cache_kv_latent, cache_k_rope, page_table),
        state_ssm_re, state_ssm_im, state_hgrn, lb, W)
    return (y_prompt, y_sample, ckv_p, ckv_s, kpe_p, kpe_s, sre_p, sre_s, sim_p, sim_s, hg_p, hg_s)
```

```python
import functools
import math

import numpy as np
import jax
import jax.numpy as jnp
from jax import lax
from jax.experimental import pallas as pl
from jax.experimental.pallas import tpu as pltpu

F32 = jnp.float32
BF16 = jnp.bfloat16

D_MODEL = 1024
D_FF = 2816
SSM_WIDTH = 256
SSM_GROUP = 16
SSM_GROUPS = 16
SSM_STATE = 64
HG_WIDTH = 256
HG_HEADS = 4
HG_DK = 64
LB_FLOOR = 1e-30
MLA_HEADS = 8
MLA_V_DIM = 64
MLA_NOPE_DIM = 64
MLA_ROPE_DIM = 32
MLA_Q_LORA = 384
MLA_KV_LORA = 256
MLA_SCALE = (MLA_NOPE_DIM + MLA_ROPE_DIM) ** -0.5
ROPE_THETA = 10000.0
NEG_MASK = -1e30
PAGE_SIZE = 128

LANES = 128
HEAD_PAD = LANES
MLA_PAD = MLA_HEADS * HEAD_PAD
N_IN_PAD = 2048
ROPE_LANE0 = MLA_NOPE_DIM
S5_CHUNK = 16
HG_GROUP = 16
VMEM_LIMIT = 56 * 1024 * 1024
DEPTH_ = 4
ALPHA = (2 * DEPTH_) ** 0.25


def _cparams(sem):
    return pltpu.CompilerParams(dimension_semantics=sem, vmem_limit_bytes=VMEM_LIMIT)


def _dot(a, b):
    return jnp.dot(a, b, preferred_element_type=F32)


def _dot_nt(a, b):
    return lax.dot_general(a, b, (((1,), (1,)), ((), ())), preferred_element_type=F32)


def _dot_tn(a, b):
    return lax.dot_general(a, b, (((0,), (0,)), ((), ())), preferred_element_type=F32)


def _split_dot(w, x, terms):
    acc = None
    r = x
    for _ in range(terms):
        p = r.astype(BF16)
        d = _dot(w, p)
        acc = d if acc is None else acc + d
        r = r - p.astype(F32)
    return acc


def _split_dot_r(x, w, terms):
    acc = None
    r = x
    for _ in range(terms):
        p = r.astype(BF16)
        d = _dot(p, w)
        acc = d if acc is None else acc + d
        r = r - p.astype(F32)
    return acc


def _layer_norm(y, g, b):
    mu = jnp.mean(y, -1, keepdims=True)
    d = y - mu
    var = jnp.mean(d * d, -1, keepdims=True)
    return d * lax.rsqrt(var + 1e-5) * g + b


def _sigmoid(x):
    return 1.0 / (1.0 + jnp.exp(-x))


def _silu(x):
    return x * _sigmoid(x)


def _gelu_tanh(x):
    c = math.sqrt(2.0 / math.pi)
    return 0.5 * x * (1.0 + jnp.tanh(c * (x + 0.044715 * (x * x * x))))


def _ada_kernel(c_ref, w_ref, b_ref, o_ref):
    c = c_ref[...]
    s = _silu(c).astype(BF16)
    o_ref[...] = _dot(s, w_ref[...].astype(BF16)) + b_ref[...]


def _ada_call(c_all, w_ada, b_ada):
    rows = c_all.shape[0]
    depth, d, n = w_ada.shape
    tn = 1152
    return pl.pallas_call(
        _ada_kernel,
        grid=(depth, n // tn),
        in_specs=[pl.BlockSpec((rows, d), lambda l, j: (0, 0)),
                  pl.BlockSpec((None, d, tn), lambda l, j: (l, 0, j)),
                  pl.BlockSpec((None, 1, tn), lambda l, j: (l, 0, j))],
        out_specs=pl.BlockSpec((None, rows, tn), lambda l, j: (l, 0, j)),
        out_shape=jax.ShapeDtypeStruct((depth, rows, n), F32),
        compiler_params=_cparams(("parallel", "parallel")),
        name="ada",
    )(c_all, w_ada, b_ada.reshape(depth, 1, n))


class _Group:
    def __init__(self, mod, batch, seq, tm):
        self.batch, self.seq, self.tm = batch, seq, tm
        self.rows = batch * seq
        self.per_row = seq == 1
        depth = mod.shape[0]
        if self.per_row:
            self.mod = jnp.transpose(mod, (0, 2, 1, 3))
        else:
            self.mod = mod.reshape(depth, batch, 9, 1, D_MODEL)
            self.tiles_per_batch = seq // tm

    def mod_spec(self, l, j):
        if self.per_row:
            return pl.BlockSpec((None, None, self.tm, D_MODEL), lambda i: (l, j, i, 0))
        tpb = self.tiles_per_batch
        return pl.BlockSpec((None, None, None, 1, D_MODEL), lambda i: (l, i // tpb, j, 0, 0))

    def pos_spec(self, width):
        if self.per_row:
            return pl.BlockSpec((1, width), lambda i: (0, 0))
        tpb = self.tiles_per_batch
        return pl.BlockSpec((self.tm, width), lambda i: (i % tpb, 0))

    def row_spec(self, width, col=0):
        return pl.BlockSpec((self.tm, width), lambda i: (i, col))

    @property
    def grid(self):
        return (self.rows // self.tm,)


def _const_spec(shape, index):
    return pl.BlockSpec(shape, lambda i: index)


FF_CHUNK = 256


def _ffn_kernel(x_ref, sh_ref, sc_ref, gt_ref, wgu_ref, wd_ref, lg_ref, lb_ref, o_ref, a_ref):
    x = x_ref[...]
    h = (x * (1.0 + sc_ref[...]) + sh_ref[...]).astype(BF16)
    for c in range(D_FF // FF_CHUNK):
        lo = c * FF_CHUNK
        g = _dot(h, wgu_ref[:, lo:lo + FF_CHUNK])
        u = _dot(h, wgu_ref[:, D_FF + lo:D_FF + lo + FF_CHUNK])
        a_ref[:, lo:lo + FF_CHUNK] = (_silu(g) * u).astype(BF16)
    f = _dot(a_ref[...], wd_ref[...])
    y = ALPHA * x + 0.5 * (1.0 + gt_ref[...]) * f
    o_ref[...] = _layer_norm(y, lg_ref[...], lb_ref[...])


def _ffn_call(grp, x, l, which, mod_idx, wgu, wd, ln_g, ln_b):
    tm = grp.tm
    return pl.pallas_call(
        _ffn_kernel,
        grid=grp.grid,
        in_specs=[grp.row_spec(D_MODEL),
                  grp.mod_spec(l, 3 * mod_idx), grp.mod_spec(l, 3 * mod_idx + 1), grp.mod_spec(l, 3 * mod_idx + 2),
                  _const_spec((None, None, D_MODEL, 2 * D_FF), (l, which, 0, 0)),
                  _const_spec((None, None, D_FF, D_MODEL), (l, which, 0, 0)),
                  _const_spec((None, None, 1, D_MODEL), (l, mod_idx, 0, 0)),
                  _const_spec((None, None, 1, D_MODEL), (l, mod_idx, 0, 0))],
        out_specs=grp.row_spec(D_MODEL),
        out_shape=jax.ShapeDtypeStruct((grp.rows, D_MODEL), F32),
        scratch_shapes=[pltpu.VMEM((tm, D_FF), BF16)],
        compiler_params=_cparams(("parallel",)),
        name="ffn",
    )(x, grp.mod, grp.mod, grp.mod, wgu, wd, ln_g, ln_b)


Z_U = (0, 256)
Z_H = (256, 1280)
Z_CQ = (1280, 1664)
Z_CKV = (1664, 1920)
Z_KPE = (1920, 2048)


def _rope_group(t, c, s1, s2):
    return t * c + pltpu.roll(t, LANES - MLA_ROPE_DIM // 2, 1) * s1 + pltpu.roll(t, MLA_ROPE_DIM // 2, 1) * s2


def _mix_in_kernel(x_ref, sh_ref, sc_ref, win_ref, qn_ref, wuq_ref, kvn_ref, wuk_ref, wuv_ref,
                   rc_ref, rs1_ref, rs2_ref,
                   u_ref, zh_ref, qh_ref, kh_ref, vh_ref, ckv_ref, kpe_ref):
    x = x_ref[...]
    h = (x * (1.0 + sc_ref[...]) + sh_ref[...]).astype(BF16)
    u_ref[...] = _dot(h, win_ref[:, Z_U[0]:Z_U[1]])
    zh_ref[...] = _dot(h, win_ref[:, Z_H[0]:Z_H[1]])
    rc, rs1, rs2 = rc_ref[...], rs1_ref[...], rs2_ref[...]

    zcq = _dot(h, win_ref[:, Z_CQ[0]:Z_CQ[1]])
    cq = zcq * lax.rsqrt(jnp.mean(zcq * zcq, -1, keepdims=True) + 1e-6) * qn_ref[...]
    q = _dot(cq.astype(BF16), wuq_ref[...]) * MLA_SCALE
    for hd in range(MLA_HEADS):
        lo = hd * HEAD_PAD
        qh_ref[:, lo:lo + HEAD_PAD] = _rope_group(q[:, lo:lo + HEAD_PAD], rc, rs1, rs2).astype(BF16)

    zckv = _dot(h, win_ref[:, Z_CKV[0]:Z_CKV[1]])
    ckv = zckv * lax.rsqrt(jnp.mean(zckv * zckv, -1, keepdims=True) + 1e-6) * kvn_ref[...]
    ckv_ref[...] = ckv
    ckv_b = ckv.astype(BF16)
    kpe = _rope_group(_dot(h, win_ref[:, Z_KPE[0]:Z_KPE[1]]), rc, rs1, rs2)
    kpe_ref[...] = pltpu.roll(kpe, LANES - ROPE_LANE0, 1)[:, :MLA_ROPE_DIM]
    kn = _dot(ckv_b, wuk_ref[...])
    for hd in range(MLA_HEADS):
        lo = hd * HEAD_PAD
        kh_ref[:, lo:lo + HEAD_PAD] = (kn[:, lo:lo + HEAD_PAD] + kpe).astype(BF16)
    vh_ref[...] = _dot(ckv_b, wuv_ref[...]).astype(BF16)


def _mix_in_call(grp, x, l, w, rope):
    rows = grp.rows
    outs = [((rows, SSM_WIDTH), F32), ((rows, 4 * HG_WIDTH), F32), ((rows, MLA_PAD), BF16),
            ((rows, MLA_PAD), BF16), ((rows, MLA_PAD), BF16), ((rows, MLA_KV_LORA), F32),
            ((rows, MLA_ROPE_DIM), F32)]
    return pl.pallas_call(
        _mix_in_kernel,
        grid=grp.grid,
        in_specs=[grp.row_spec(D_MODEL), grp.mod_spec(l, 3), grp.mod_spec(l, 4),
                  _const_spec((None, D_MODEL, N_IN_PAD), (l, 0, 0)),
                  _const_spec((None, 1, MLA_Q_LORA), (l, 0, 0)),
                  _const_spec((None, MLA_Q_LORA, MLA_PAD), (l, 0, 0)),
                  _const_spec((None, 1, MLA_KV_LORA), (l, 0, 0)),
                  _const_spec((None, MLA_KV_LORA, MLA_PAD), (l, 0, 0)),
                  _const_spec((None, MLA_KV_LORA, MLA_PAD), (l, 0, 0)),
                  grp.pos_spec(LANES), grp.pos_spec(LANES), grp.pos_spec(LANES)],
        out_specs=[grp.row_spec(s[1]) for s, _ in outs],
        out_shape=[jax.ShapeDtypeStruct(s, d) for s, d in outs],
        compiler_params=_cparams(("parallel",)),
        name="mix_in",
    )(x, grp.mod, grp.mod, w["w_in"], w["q_norm"], w["w_uq"], w["kv_norm"], w["w_uk"], w["w_uv"], *rope)


def _s5_kernel(u_ref, m_ref, wis_ref, wso_ref, av_ref, d_ref, x0_ref, y_ref, xf_ref, s_scr, x_scr, *, nb, nc):
    u = u_ref[...]
    ub = u.astype(BF16)
    s_scr[...] = _dot(ub, wis_ref[...])
    a1, a2, a3 = av_ref[0:1, :], av_ref[1:2, :], av_ref[2:3, :]

    def body(c, carry):
        v, w = carry
        r = pl.multiple_of(c * nb, nb)
        x_scr[pl.ds(r, nb), :] = v
        sv = s_scr[pl.ds(r, nb), 0:LANES]
        sw = s_scr[pl.ds(r, nb), LANES:2 * LANES]
        return a1 * v + a2 * w + sv, a1 * w + a3 * v + sw

    v, _ = lax.fori_loop(0, nc, body, (x0_ref[:, 0:LANES], x0_ref[:, LANES:2 * LANES]), unroll=8)
    xf_ref[...] = v
    y_ref[...] = _dot(ub, m_ref[...]) + _dot(x_scr[...].astype(BF16), wso_ref[...]) + u * d_ref[...]


def _s5_call(u_g, x0, w, l, nb, nc):
    g = SSM_GROUPS
    rows = nb * nc
    wide = S5_CHUNK * SSM_GROUP
    return pl.pallas_call(
        functools.partial(_s5_kernel, nb=nb, nc=nc),
        grid=(g,),
        in_specs=[pl.BlockSpec((None, rows, wide), lambda i: (i, 0, 0)),
                  pl.BlockSpec((None, None, wide, wide), lambda i: (l, i, 0, 0)),
                  pl.BlockSpec((None, None, wide, wide), lambda i: (l, i, 0, 0)),
                  pl.BlockSpec((None, None, LANES, wide), lambda i: (l, i, 0, 0)),
                  pl.BlockSpec((None, None, 8, LANES), lambda i: (l, i, 0, 0)),
                  pl.BlockSpec((None, None, 1, wide), lambda i: (l, i, 0, 0)),
                  pl.BlockSpec((None, nb, wide), lambda i: (i, 0, 0))],
        out_specs=[pl.BlockSpec((None, rows, wide), lambda i: (i, 0, 0)),
                   pl.BlockSpec((None, nb, LANES), lambda i: (i, 0, 0))],
        out_shape=[jax.ShapeDtypeStruct((g, rows, wide), F32), jax.ShapeDtypeStruct((g, nb, LANES), F32)],
        scratch_shapes=[pltpu.VMEM((rows, wide), F32), pltpu.VMEM((rows, LANES), F32)],
        compiler_params=_cparams(("parallel",)),
        name="s5_chunks",
    )(u_g, w["s5_m"], w["s5_wis"], w["s5_wso"], w["s5_av"], w["s5_d"], x0)


def _s5_step_kernel(u_ref, xr_ref, xi_ref, bb_ref, cc_ref, ar_ref, ai_ref, d_ref, y_ref, or_ref, oi_ref):
    u = u_ref[...]
    n = SSM_GROUPS * SSM_STATE
    bu = _dot(u.astype(BF16), bb_ref[...])
    xr, xi, ar, ai = xr_ref[...], xi_ref[...], ar_ref[...], ai_ref[...]
    nr = ar * xr - ai * xi + bu[:, :n]
    ni = ar * xi + ai * xr + bu[:, n:]
    or_ref[...] = nr
    oi_ref[...] = ni
    y_ref[...] = (_dot(nr.astype(BF16), cc_ref[0:n, :]) + _dot(ni.astype(BF16), cc_ref[n:2 * n, :])
                  + u * d_ref[...])


def _s5_step_call(u, xr, xi, w, l):
    b = u.shape[0]
    n = SSM_GROUPS * SSM_STATE
    full = lambda shape: pl.BlockSpec(shape, lambda i: (0,) * len(shape))
    lsel = lambda shape: pl.BlockSpec((None,) + shape, lambda i: (l,) + (0,) * len(shape))
    return pl.pallas_call(
        _s5_step_kernel,
        grid=(1,),
        in_specs=[full((b, SSM_WIDTH)), full((b, n)), full((b, n)),
                  lsel((SSM_WIDTH, 2 * n)), lsel((2 * n, SSM_WIDTH)), lsel((1, n)), lsel((1, n)),
                  lsel((1, SSM_WIDTH))],
        out_specs=[full((b, SSM_WIDTH)), full((b, n)), full((b, n))],
        out_shape=[jax.ShapeDtypeStruct((b, SSM_WIDTH), F32), jax.ShapeDtypeStruct((b, n), F32),
                   jax.ShapeDtypeStruct((b, n), F32)],
        compiler_params=_cparams(("arbitrary",)),
        name="s5_step",
    )(u, xr, xi, w["s5_bb"], w["s5_cc"], w["s5_ar"], w["s5_ai"], w["s5_dflat"])


def _hgrn_kernel(zh_ref, s0_ref, lbp_ref, e_ref, bd_ref, lcum_ref, lall_ref, o_ref, sf_ref,
                 fpad, kpad, vpad, st_ref, oi_ref, *, tt, n_valid):
    j = pl.program_id(1)

    @pl.when(j == 0)
    def _():
        st_ref[...] = s0_ref[...]

    w = HG_WIDTH
    zf = zh_ref[:, 0:w]
    q = zh_ref[:, w:2 * w]
    v = zh_ref[:, 2 * w:3 * w]
    la, l1, oml = lbp_ref[0:1, :], lbp_ref[1:2, :], lbp_ref[2:3, :]
    lsig = jnp.minimum(zf, 0.0) - jnp.log1p(jnp.exp(-jnp.abs(zf)))
    t2 = l1 + lsig
    lf = jnp.maximum(la, t2) + jnp.log1p(jnp.exp(-jnp.abs(la - t2)))
    k = oml * _sigmoid(-zf)
    row = lax.broadcasted_iota(jnp.int32, (tt, w), 0)
    if n_valid < tt:
        live = row < n_valid
        lf = jnp.where(live, lf, 0.0)
        k = jnp.where(live, k, 0.0)
    f = jnp.exp(lf)

    pad = HG_GROUP
    zero_pad = jnp.zeros((pad, w), F32)
    fpad[0:pad, :] = zero_pad
    kpad[0:pad, :] = zero_pad
    vpad[0:pad, :] = zero_pad
    fpad[pad:pad + tt, :] = f
    kpad[pad:pad + tt, :] = k
    vpad[pad:pad + tt, :] = v

    in_group = row & (HG_GROUP - 1)
    e = e_ref[...]
    acc = _dot((q * k).astype(BF16), e) * v
    decay = jnp.ones((tt, w), F32)
    for d in range(1, HG_GROUP):
        decay = decay * fpad[pad - d + 1:pad - d + 1 + tt, :]
        p = jnp.where(in_group >= d, q * kpad[pad - d:pad - d + tt, :] * decay, 0.0)
        acc = acc + _dot(p.astype(BF16), e) * vpad[pad - d:pad - d + tt, :]

    b = _split_dot(lcum_ref[...], lf, 3)
    bl = _split_dot(lall_ref[...], lf, 3)
    qe = (q * jnp.exp(b)).astype(BF16)
    kx = (k * jnp.exp(bl - b)).astype(BF16)
    ebl = jnp.exp(bl)
    vb = v.astype(BF16)
    bd = bd_ref[...]
    for g in range(tt // HG_GROUP):
        lo = g * HG_GROUP
        st = st_ref[...]
        oi_ref[lo:lo + HG_GROUP, :] = _dot_nt(qe[lo:lo + HG_GROUP], st.astype(BF16))
        upd = _dot_tn(vb[lo:lo + HG_GROUP], kx[lo:lo + HG_GROUP])
        st_ref[...] = st * ebl[lo:lo + 1, :] + upd * bd
    o_ref[...] = acc + oi_ref[...]

    @pl.when(j == pl.num_programs(1) - 1)
    def _():
        sf_ref[...] = st_ref[...]


def _hgrn_call(zh, s0t, w, l, batch, seq_rows, tt, n_valid):
    wd = HG_WIDTH
    nj = seq_rows // tt
    return pl.pallas_call(
        functools.partial(_hgrn_kernel, tt=tt, n_valid=n_valid),
        grid=(batch, nj),
        in_specs=[pl.BlockSpec((tt, 4 * wd), lambda b, j: (b * nj + j, 0)),
                  pl.BlockSpec((None, wd, wd), lambda b, j: (b, 0, 0)),
                  pl.BlockSpec((None, 8, wd), lambda b, j: (l, 0, 0)),
                  pl.BlockSpec((wd, wd), lambda b, j: (0, 0)),
                  pl.BlockSpec((wd, wd), lambda b, j: (0, 0)),
                  pl.BlockSpec((tt, tt), lambda b, j: (0, 0)),
                  pl.BlockSpec((tt, tt), lambda b, j: (0, 0))],
        out_specs=[pl.BlockSpec((tt, wd), lambda b, j: (b * nj + j, 0)),
                   pl.BlockSpec((None, wd, wd), lambda b, j: (b, 0, 0))],
        out_shape=[jax.ShapeDtypeStruct((batch * seq_rows, wd), F32),
                   jax.ShapeDtypeStruct((batch, wd, wd), F32)],
        scratch_shapes=[pltpu.VMEM((HG_GROUP + tt, wd), F32)] * 3
                       + [pltpu.VMEM((wd, wd), F32), pltpu.VMEM((tt, wd), F32)],
        compiler_params=_cparams(("parallel", "arbitrary")),
        name="hgrn",
    )(zh, s0t, w["hg_lbp"], w["hg_e"], w["hg_bd"], w["hg_lcum"][tt], w["hg_lall"][tt])


def _flash_kernel(q_ref, k_ref, v_ref, o_ref, m_ref, l_ref, acc_ref, *, tq):
    qi = pl.program_id(2)
    ki = pl.program_id(3)

    @pl.when(ki == 0)
    def _():
        m_ref[...] = jnp.full(m_ref.shape, NEG_MASK, F32)
        l_ref[...] = jnp.zeros(l_ref.shape, F32)
        acc_ref[...] = jnp.zeros(acc_ref.shape, F32)

    @pl.when(ki <= qi)
    def _():
        s = _dot_nt(q_ref[...], k_ref[...])
        row = lax.broadcasted_iota(jnp.int32, s.shape, 0) + qi * tq
        col = lax.broadcasted_iota(jnp.int32, s.shape, 1) + ki * tq
        s = jnp.where(col <= row, s, NEG_MASK)
        m_prev = m_ref[...]
        m_new = jnp.maximum(m_prev, jnp.max(s, -1, keepdims=True))
        alpha = jnp.exp(m_prev - m_new)
        p = jnp.exp(s - m_new)
        l_ref[...] = alpha * l_ref[...] + jnp.sum(p, -1, keepdims=True)
        acc_ref[...] = alpha * acc_ref[...] + _dot(p.astype(BF16), v_ref[...])
        m_ref[...] = m_new

    @pl.when(ki == qi)
    def _():
        o_ref[...] = acc_ref[...] / l_ref[...]


def _flash_call(qh, kh, vh, batch, seq, tq):
    nq = seq // tq
    q_spec = pl.BlockSpec((None, tq, HEAD_PAD), lambda b, h, qi, ki: (b, qi, h))
    kv_spec = pl.BlockSpec((None, tq, HEAD_PAD), lambda b, h, qi, ki: (b, jnp.minimum(ki, qi), h))
    shp = (batch, seq, MLA_PAD)
    return pl.pallas_call(
        functools.partial(_flash_kernel, tq=tq),
        grid=(batch, MLA_HEADS, nq, nq),
        in_specs=[q_spec, kv_spec, kv_spec],
        out_specs=q_spec,
        out_shape=jax.ShapeDtypeStruct(shp, F32),
        scratch_shapes=[pltpu.VMEM((tq, 1), F32), pltpu.VMEM((tq, 1), F32), pltpu.VMEM((tq, HEAD_PAD), F32)],
        compiler_params=_cparams(("parallel", "parallel", "parallel", "arbitrary")),
        name="flash",
    )(qh.reshape(shp), kh.reshape(shp), vh.reshape(shp))


PAGES_PER_STEP = 16


def _decode_kernel(pt_ref, qh_ref, ckvn_ref, kpen_ref, wuk_ref, wuv_ref, sel_ref, hm_ref, *rest, npg):
    ckv_refs = rest[:npg]
    kpe_refs = rest[npg:2 * npg]
    o_ref = rest[2 * npg]
    qa_ref, qp_ref, m_ref, l_ref, acc_ref = rest[2 * npg + 1:]
    j = pl.program_id(1)
    hm = hm_ref[...]

    @pl.when(j == 0)
    def _():
        qbd = jnp.where(hm > 0, jnp.broadcast_to(qh_ref[...], hm.shape), jnp.zeros_like(hm)).astype(BF16)
        qa_ref[...] = _dot_nt(qbd, wuk_ref[...]).astype(BF16)
        qp_ref[...] = _dot(qbd, sel_ref[...]).astype(BF16)
        m_ref[...] = jnp.full(m_ref.shape, NEG_MASK, F32)
        l_ref[...] = jnp.zeros(l_ref.shape, F32)
        acc_ref[...] = jnp.zeros(acc_ref.shape, F32)

    qa = qa_ref[...]
    qp = qp_ref[...]
    pages = [r[...].astype(BF16) for r in ckv_refs]
    s = jnp.concatenate(
        [_dot_nt(qa, pg) + _dot_nt(qp, kr[...].astype(BF16)) for pg, kr in zip(pages, kpe_refs)], axis=1)
    m_prev = m_ref[...]
    m_new = jnp.maximum(m_prev, jnp.max(s, -1, keepdims=True))
    alpha = jnp.exp(m_prev - m_new)
    p = jnp.exp(s - m_new)
    l_ref[...] = alpha * l_ref[...] + jnp.sum(p, -1, keepdims=True)
    pb = p.astype(BF16)
    acc = alpha * acc_ref[...]
    for i, pg in enumerate(pages):
        acc = acc + _dot(pb[:, i * PAGE_SIZE:(i + 1) * PAGE_SIZE], pg)
    acc_ref[...] = acc
    m_ref[...] = m_new

    @pl.when(j == pl.num_programs(1) - 1)
    def _():
        ckvn = ckvn_ref[...]
        s_new = (jnp.sum(qa.astype(F32) * ckvn.astype(BF16).astype(F32), -1, keepdims=True)
                 + jnp.sum(qp.astype(F32) * kpen_ref[...].astype(BF16).astype(F32), -1, keepdims=True))
        m_old = m_ref[...]
        m_fin = jnp.maximum(m_old, s_new)
        a = jnp.exp(m_old - m_fin)
        p_new = jnp.exp(s_new - m_fin)
        l_fin = a * l_ref[...] + p_new
        lat = a * acc_ref[...] + p_new.astype(BF16).astype(F32) * ckvn.astype(BF16).astype(F32)
        lat = lat / l_fin
        o = _dot(lat.astype(BF16), wuv_ref[...])
        o_ref[...] = jnp.sum(jnp.where(hm > 0, o, 0.0), 0, keepdims=True)


def _decode_call(page_table, qh, ckv_new, kpe_new, cache_ckv, cache_kpe, w, l, sel, hm):
    batch, n_pages = page_table.shape
    npg = min(PAGES_PER_STEP, n_pages)
    nj = n_pages // npg

    def page_spec(i, width):
        return pl.BlockSpec((None, None, PAGE_SIZE, width), lambda b, j, pt: (l, pt[b, j * npg + i], 0, 0))

    row3 = lambda width: pl.BlockSpec((None, 1, width), lambda b, j, pt: (b, 0, 0))
    wsel = lambda shape: pl.BlockSpec((None,) + shape, lambda b, j, pt: (l,) + (0,) * len(shape))
    full = lambda shape: pl.BlockSpec(shape, lambda b, j, pt: (0,) * len(shape))
    grid_spec = pltpu.PrefetchScalarGridSpec(
        num_scalar_prefetch=1,
        grid=(batch, nj),
        in_specs=[row3(MLA_PAD), row3(MLA_KV_LORA), row3(MLA_ROPE_DIM),
                  wsel((MLA_KV_LORA, MLA_PAD)), wsel((MLA_KV_LORA, MLA_PAD)),
                  full((MLA_PAD, MLA_ROPE_DIM)), full((MLA_HEADS, MLA_PAD))]
                 + [page_spec(i, MLA_KV_LORA) for i in range(npg)]
                 + [page_spec(i, MLA_ROPE_DIM) for i in range(npg)],
        out_specs=row3(MLA_PAD),
        scratch_shapes=[pltpu.VMEM((MLA_HEADS, MLA_KV_LORA), BF16), pltpu.VMEM((MLA_HEADS, MLA_ROPE_DIM), BF16),
                        pltpu.VMEM((MLA_HEADS, 1), F32), pltpu.VMEM((MLA_HEADS, 1), F32),
                        pltpu.VMEM((MLA_HEADS, MLA_KV_LORA), F32)],
    )
    out = pl.pallas_call(
        functools.partial(_decode_kernel, npg=npg),
        grid_spec=grid_spec,
        out_shape=jax.ShapeDtypeStruct((batch, 1, MLA_PAD), F32),
        compiler_params=_cparams(("parallel", "arbitrary")),
        name="decode",
    )(page_table, qh.reshape(batch, 1, MLA_PAD), ckv_new.reshape(batch, 1, MLA_KV_LORA),
      kpe_new.reshape(batch, 1, MLA_ROPE_DIM), w["w_uk"], w["w_uv"], sel, hm,
      *([cache_ckv] * npg), *([cache_kpe] * npg))
    return out.reshape(batch, MLA_PAD)


def _mix_out_kernel(x_ref, gt_ref, y_ref, zg_ref, oh_ref, om_ref, wglu_ref, nssm_ref, nhg_ref, nmla_ref,
                    e64_ref, woa_ref, wob_ref, lg_ref, lb_ref, o_ref, cat_ref):
    w = SSM_WIDTH
    gab = _dot(_gelu_tanh(y_ref[...]).astype(BF16), wglu_ref[...])
    t = gab[:, :w] * _sigmoid(gab[:, w:])
    cat_ref[:, 0:w] = (t * lax.rsqrt(jnp.mean(t * t, -1, keepdims=True) + 1e-6) * nssm_ref[...]).astype(BF16)

    oh = oh_ref[...]
    msq = _split_dot_r(oh * oh, e64_ref[...], 2)
    cat_ref[:, w:2 * w] = (oh * lax.rsqrt(msq + 1e-6) * nhg_ref[...] * _silu(zg_ref[...])).astype(BF16)

    om = om_ref[...]
    ms = jnp.sum(om * om, -1, keepdims=True) * (1.0 / (MLA_HEADS * MLA_V_DIM))
    o_mla = (om * lax.rsqrt(ms + 1e-6) * nmla_ref[...]).astype(BF16)

    mix = _dot(cat_ref[...], woa_ref[...]) + _dot(o_mla, wob_ref[...])
    y = ALPHA * x_ref[...] + (1.0 + gt_ref[...]) * mix
    o_ref[...] = _layer_norm(y, lg_ref[...], lb_ref[...])


def _mix_out_call(grp, x, l, y_ssm, zh, o_h, o_m, w, ln_g, ln_b):
    tm = grp.tm
    return pl.pallas_call(
        _mix_out_kernel,
        grid=grp.grid,
        in_specs=[grp.row_spec(D_MODEL), grp.mod_spec(l, 5),
                  grp.row_spec(SSM_WIDTH), grp.row_spec(HG_WIDTH, col=3), grp.row_spec(HG_WIDTH),
                  grp.row_spec(MLA_PAD),
                  _const_spec((None, SSM_WIDTH, 2 * SSM_WIDTH), (l, 0, 0)),
                  _const_spec((None, 1, SSM_WIDTH), (l, 0, 0)),
                  _const_spec((None, 1, HG_WIDTH), (l, 0, 0)),
                  _const_spec((None, 1, MLA_PAD), (l, 0, 0)),
                  _const_spec((HG_WIDTH, HG_WIDTH), (0, 0)),
                  _const_spec((None, SSM_WIDTH + HG_WIDTH, D_MODEL), (l, 0, 0)),
                  _const_spec((None, MLA_PAD, D_MODEL), (l, 0, 0)),
                  _const_spec((None, None, 1, D_MODEL), (l, 1, 0, 0)),
                  _const_spec((None, None, 1, D_MODEL), (l, 1, 0, 0))],
        out_specs=grp.row_spec(D_MODEL),
        out_shape=jax.ShapeDtypeStruct((grp.rows, D_MODEL), F32),
        scratch_shapes=[pltpu.VMEM((tm, SSM_WIDTH + HG_WIDTH), BF16)],
        compiler_params=_cparams(("parallel",)),
        name="mix_out",
    )(x, grp.mod, y_ssm, zh, o_h, o_m, w["w_glu"], w["n_ssm"], w["n_hg"], w["n_mla"], w["hg_e64"],
      w["w_out_a"], w["w_out_b"], ln_g, ln_b)


def _pad_heads(wt, per_head):
    lead = wt.shape[:-1]
    wt = wt.reshape(lead + (MLA_HEADS, per_head))
    wt = jnp.pad(wt, [(0, 0)] * len(lead) + [(0, 0), (0, HEAD_PAD - per_head)])
    return wt.reshape(lead + (MLA_PAD,))


def _s5_tables(lam_re, lam_im, b_re, b_im, c_re, c_im, d, log_step):
    hp = lax.Precision.HIGHEST
    depth, g, p = lam_re.shape
    h, t = SSM_GROUP, S5_CHUNK
    step = jnp.exp(log_step)[..., None]
    mag = jnp.exp(lam_re * step)
    ab_re, ab_im = mag * jnp.cos(lam_im * step), mag * jnp.sin(lam_im * step)
    den = lam_re * lam_re + lam_im * lam_im
    nr = ab_re - 1.0
    coef_re = (nr * lam_re + ab_im * lam_im) / den
    coef_im = (ab_im * lam_re - nr * lam_im) / den
    bb_re = coef_re[..., None] * b_re - coef_im[..., None] * b_im
    bb_im = coef_re[..., None] * b_im + coef_im[..., None] * b_re
    n = jnp.arange(t + 1, dtype=F32)[:, None, None, None]
    pmag = jnp.exp(n * lam_re * step)
    pw_re, pw_im = pmag * jnp.cos(n * lam_im * step), pmag * jnp.sin(n * lam_im * step)
    pb_re = pw_re[:t, ..., None] * bb_re - pw_im[:t, ..., None] * bb_im
    pb_im = pw_re[:t, ..., None] * bb_im + pw_im[:t, ..., None] * bb_re
    kern = (jnp.einsum('lgop,nlgpi->nlgoi', c_re, pb_re, precision=hp)
            - jnp.einsum('lgop,nlgpi->nlgoi', c_im, pb_im, precision=hp))
    kern = jnp.concatenate([kern, jnp.zeros_like(kern[:1])], 0)
    s_idx, t_idx = np.arange(t)[:, None], np.arange(t)[None, :]
    lag = np.where(t_idx >= s_idx, t_idx - s_idx, t)
    m = kern[lag]
    m = jnp.transpose(m, (2, 3, 0, 5, 1, 4)).reshape(depth, g, t * h, t * h)
    wr = jnp.transpose(pb_re[::-1], (1, 2, 0, 4, 3)).reshape(depth, g, t * h, p)
    wi = jnp.transpose(pb_im[::-1], (1, 2, 0, 4, 3)).reshape(depth, g, t * h, p)
    wis = jnp.concatenate([wr, wi, wi, wr], -1)
    cp_re = c_re[None] * pw_re[1:, :, :, None, :] - c_im[None] * pw_im[1:, :, :, None, :]
    cp_im = c_re[None] * pw_im[1:, :, :, None, :] + c_im[None] * pw_re[1:, :, :, None, :]
    so_re = jnp.transpose(cp_re, (1, 2, 4, 0, 3)).reshape(depth, g, p, t * h)
    so_im = jnp.transpose(-cp_im, (1, 2, 4, 0, 3)).reshape(depth, g, p, t * h)
    wso = jnp.concatenate([so_re, so_im], 2)
    a_re, a_im = pw_re[t], pw_im[t]
    av = jnp.stack([jnp.concatenate([a_re, a_re], -1), jnp.concatenate([-a_im, a_im], -1),
                    jnp.concatenate([a_im, -a_im], -1)], 2)
    av = jnp.pad(av, ((0, 0), (0, 0), (0, 5), (0, 0)))
    dg = d.reshape(depth, g, 1, h)
    d_t = jnp.tile(dg, (1, 1, 1, t))
    eye = jnp.eye(g, dtype=F32)
    bbr = jnp.einsum('lgph,gk->lghkp', bb_re, eye).reshape(depth, g * h, g * p)
    bbi = jnp.einsum('lgph,gk->lghkp', bb_im, eye).reshape(depth, g * h, g * p)
    ccr = jnp.einsum('lghp,gk->lgpkh', c_re, eye).reshape(depth, g * p, g * h)
    cci = jnp.einsum('lghp,gk->lgpkh', -c_im, eye).reshape(depth, g * p, g * h)
    return dict(s5_m=m.astype(BF16), s5_wis=wis.astype(BF16), s5_wso=wso.astype(BF16), s5_av=av, s5_d=d_t,
                s5_bb=jnp.concatenate([bbr, bbi], -1).astype(BF16),
                s5_cc=jnp.concatenate([ccr, cci], 1).astype(BF16),
                s5_ar=ab_re.reshape(depth, 1, g * p), s5_ai=ab_im.reshape(depth, 1, g * p),
                s5_dflat=d.reshape(depth, 1, g * h))


def _rope_tables(pos):
    half = MLA_ROPE_DIM // 2
    inv = 1.0 / (ROPE_THETA ** (jnp.arange(0, MLA_ROPE_DIM, 2, dtype=F32) / MLA_ROPE_DIM))
    ang = pos.astype(F32)[:, None] * inv[None, :]
    cos, sin = jnp.cos(ang), jnp.sin(ang)
    n = pos.shape[0]
    one, zero = jnp.ones((n, ROPE_LANE0), F32), jnp.zeros((n, ROPE_LANE0), F32)
    z16, z32 = jnp.zeros((n, half), F32), jnp.zeros((n, LANES - ROPE_LANE0 - MLA_ROPE_DIM), F32)
    rc = jnp.concatenate([one, cos, cos, z32], 1)
    rs1 = jnp.concatenate([zero, -sin, z16, z32], 1)
    rs2 = jnp.concatenate([zero, z16, sin, z32], 1)
    return rc, rs1, rs2


def _group_patterns(tt):
    r = np.arange(tt)
    same = (r[:, None] // HG_GROUP) == (r[None, :] // HG_GROUP)
    lcum = same & (r[None, :] <= r[:, None])
    return jnp.asarray(lcum, BF16), jnp.asarray(same, BF16)


def _prepare(p):
    depth = p["w_in"].shape[0]
    w = {}
    w["ffn_w_gu"] = p["ffn_w_gu"].astype(BF16)
    w["ffn_w_down"] = p["ffn_w_down"].astype(BF16)
    w_in = p["w_in"]
    n_main = Z_CKV[1]
    zpad = lambda n: jnp.zeros((depth, D_MODEL, n), w_in.dtype)
    w["w_in"] = jnp.concatenate(
        [w_in[:, :, :n_main], zpad(ROPE_LANE0), w_in[:, :, n_main:], zpad(LANES - ROPE_LANE0 - MLA_ROPE_DIM)],
        -1).astype(BF16)
    w["w_uq"] = _pad_heads(p["mla_w_uq"], MLA_NOPE_DIM + MLA_ROPE_DIM).astype(BF16)
    w["w_uk"] = _pad_heads(p["mla_w_uk"], MLA_NOPE_DIM).astype(BF16)
    w["w_uv"] = _pad_heads(p["mla_w_uv"], MLA_V_DIM).astype(BF16)
    w["q_norm"] = p["mla_q_norm"].reshape(depth, 1, MLA_Q_LORA)
    w["kv_norm"] = p["mla_kv_norm"].reshape(depth, 1, MLA_KV_LORA)
    w["w_glu"] = p["ssm_w_glu"].astype(BF16)
    w["n_ssm"] = p["norm_ssm"].reshape(depth, 1, SSM_WIDTH)
    w["n_hg"] = p["norm_hgrn"].reshape(depth, 1, HG_WIDTH)
    w["n_mla"] = _pad_heads(p["norm_mla"], MLA_V_DIM).reshape(depth, 1, MLA_PAD)
    w_out = p["w_out"]
    split = SSM_WIDTH + HG_WIDTH
    w["w_out_a"] = w_out[:, :split].astype(BF16)
    wb = w_out[:, split:].reshape(depth, MLA_HEADS, MLA_V_DIM, D_MODEL)
    w["w_out_b"] = jnp.pad(wb, ((0, 0), (0, 0), (0, HEAD_PAD - MLA_V_DIM), (0, 0))).reshape(
        depth, MLA_PAD, D_MODEL).astype(BF16)
    w.update(_s5_tables(p["ssm_lambda_re"].astype(F32), p["ssm_lambda_im"].astype(F32),
                        p["ssm_b_re"].astype(F32), p["ssm_b_im"].astype(F32),
                        p["ssm_c_re"].astype(F32), p["ssm_c_im"].astype(F32),
                        p["ssm_d"].astype(F32), p["ssm_log_step"].astype(F32)))
    sm = jax.nn.softmax(p["hgrn_lb_logits"].astype(F32), axis=0)
    lb = jnp.clip(jnp.cumsum(sm, axis=0) - sm[0:1], 0.0, 1.0 - 1e-6)
    rows = jnp.stack([jnp.log(jnp.maximum(lb, LB_FLOOR)), jnp.log1p(-lb), 1.0 - lb], 1)
    w["hg_lbp"] = jnp.pad(rows, ((0, 0), (0, 5), (0, 0)))
    head = np.arange(HG_WIDTH) // HG_DK
    same_head = head[:, None] == head[None, :]
    w["hg_e"] = jnp.asarray(same_head, BF16)
    w["hg_bd"] = jnp.asarray(same_head, F32)
    w["hg_e64"] = jnp.asarray(same_head / HG_DK, BF16)
    return w


def _hg_state_in(s):
    b = s.shape[0]
    eye = jnp.eye(HG_HEADS, dtype=s.dtype)
    return jnp.einsum('bhkv,hg->bhvgk', s, eye).reshape(b, HG_WIDTH, HG_WIDTH)


def _hg_state_out(st):
    b = st.shape[0]
    st = st.reshape(b, HG_HEADS, HG_DK, HG_HEADS, HG_DK)
    diag = jnp.stack([st[:, h, :, h, :] for h in range(HG_HEADS)], 1)
    return jnp.transpose(diag, (0, 1, 3, 2))


def _run_prompt(x, grp, w, p, rope):
    batch, seq = grp.batch, grp.seq
    depth = p["w_in"].shape[0]
    nc = seq // S5_CHUNK
    tt = min(256, seq)
    tq = min(512, seq)
    ln_g = p["ln_g"].reshape(depth, 3, 1, D_MODEL)
    ln_b = p["ln_b"].reshape(depth, 3, 1, D_MODEL)
    x0 = jnp.zeros((SSM_GROUPS, batch, 2 * LANES), F32)
    s0 = jnp.zeros((batch, HG_WIDTH, HG_WIDTH), F32)
    ckv_l, kpe_l, sre_l, sim_l, hg_l = [], [], [], [], []
    for l in range(depth):
        x = _ffn_call(grp, x, l, 0, 0, w["ffn_w_gu"], w["ffn_w_down"], ln_g, ln_b)
        u, zh, qh, kh, vh, ckv, kpe = _mix_in_call(grp, x, l, w, rope)
        u_g = jnp.transpose(u.reshape(batch, nc, S5_CHUNK, SSM_GROUPS, SSM_GROUP), (3, 1, 0, 2, 4))
        y_g, xf = _s5_call(u_g.reshape(SSM_GROUPS, nc * batch, S5_CHUNK * SSM_GROUP), x0, w, l, batch, nc)
        y = jnp.transpose(y_g.reshape(SSM_GROUPS, nc, batch, S5_CHUNK, SSM_GROUP), (2, 1, 3, 0, 4))
        y = y.reshape(batch * seq, SSM_WIDTH)
        o_h, st = _hgrn_call(zh, s0, w, l, batch, seq, tt, tt)
        o_m = _flash_call(qh, kh, vh, batch, seq, tq).reshape(batch * seq, MLA_PAD)
        x = _mix_out_call(grp, x, l, y, zh, o_h, o_m, w, ln_g, ln_b)
        x = _ffn_call(grp, x, l, 1, 2, w["ffn_w_gu"], w["ffn_w_down"], ln_g, ln_b)
        ckv_l.append(ckv.reshape(batch, seq, MLA_KV_LORA))
        kpe_l.append(kpe.reshape(batch, seq, MLA_ROPE_DIM))
        xf = jnp.transpose(xf, (1, 0, 2))
        sre_l.append(xf[..., :SSM_STATE])
        sim_l.append(xf[..., SSM_STATE:])
        hg_l.append(_hg_state_out(st))
    return (x.reshape(batch, seq, D_MODEL), jnp.stack(ckv_l), jnp.stack(kpe_l), jnp.stack(sre_l),
            jnp.stack(sim_l), jnp.stack(hg_l))


def _run_sample(x, grp, w, p, rope, cache_ckv, cache_kpe, page_table, ssm_re, ssm_im, hg_state):
    batch = grp.batch
    depth = p["w_in"].shape[0]
    ln_g = p["ln_g"].reshape(depth, 3, 1, D_MODEL)
    ln_b = p["ln_b"].reshape(depth, 3, 1, D_MODEL)
    n = SSM_GROUPS * SSM_STATE
    lane = np.arange(MLA_PAD)
    sel = np.zeros((MLA_PAD, MLA_ROPE_DIM), np.float32)
    for i in range(MLA_ROPE_DIM):
        sel[(lane % HEAD_PAD) == ROPE_LANE0 + i, i] = 1.0
    sel = jnp.asarray(sel, BF16)
    hm = jnp.asarray((lane[None, :] // HEAD_PAD) == np.arange(MLA_HEADS)[:, None], F32)
    ckv_l, kpe_l, sre_l, sim_l, hg_l = [], [], [], [], []
    for l in range(depth):
        x = _ffn_call(grp, x, l, 0, 0, w["ffn_w_gu"], w["ffn_w_down"], ln_g, ln_b)
        u, zh, qh, kh, vh, ckv, kpe = _mix_in_call(grp, x, l, w, rope)
        y, nr, ni = _s5_step_call(u, ssm_re[l].reshape(batch, n), ssm_im[l].reshape(batch, n), w, l)
        zh_pad = jnp.pad(zh[:, None, :], ((0, 0), (0, HG_GROUP - 1), (0, 0))).reshape(batch * HG_GROUP, -1)
        o_hp, st = _hgrn_call(zh_pad, _hg_state_in(hg_state[l]), w, l, batch, HG_GROUP, HG_GROUP, 1)
        o_h = o_hp.reshape(batch, HG_GROUP, HG_WIDTH)[:, 0]
        o_m = _decode_call(page_table, qh, ckv, kpe, cache_ckv, cache_kpe, w, l, sel, hm)
        x = _mix_out_call(grp, x, l, y, zh, o_h, o_m, w, ln_g, ln_b)
        x = _ffn_call(grp, x, l, 1, 2, w["ffn_w_gu"], w["ffn_w_down"], ln_g, ln_b)
        ckv_l.append(ckv.reshape(batch, 1, MLA_KV_LORA))
        kpe_l.append(kpe.reshape(batch, 1, MLA_ROPE_DIM))
        sre_l.append(nr.reshape(batch, SSM_GROUPS, SSM_STATE))
        sim_l.append(ni.reshape(batch, SSM_GROUPS, SSM_STATE))
        hg_l.append(_hg_state_out(st))
    return (x.reshape(batch, 1, D_MODEL), jnp.stack(ckv_l), jnp.stack(kpe_l), jnp.stack(sre_l),
            jnp.stack(sim_l), jnp.stack(hg_l))


def kernel(x_prompt, x_sample, cache_kv_latent, cache_k_rope, state_ssm_re, state_ssm_im, state_hgrn, page_table, c_prompt, c_sample, w_ada, b_ada, ln_g, ln_b, ffn_w_gu, ffn_w_down, w_in, w_out, ssm_lambda_re, ssm_lambda_im, ssm_b_re, ssm_b_im, ssm_c_re, ssm_c_im, ssm_d, ssm_log_step, ssm_w_glu, norm_ssm, hgrn_lb_logits, norm_hgrn, mla_q_norm, mla_w_uq, mla_kv_norm, mla_w_uk, mla_w_uv, norm_mla):
    p = dict(ln_g=ln_g, ln_b=ln_b, ffn_w_gu=ffn_w_gu, ffn_w_down=ffn_w_down, w_in=w_in, w_out=w_out,
             ssm_lambda_re=ssm_lambda_re, ssm_lambda_im=ssm_lambda_im, ssm_b_re=ssm_b_re, ssm_b_im=ssm_b_im,
             ssm_c_re=ssm_c_re, ssm_c_im=ssm_c_im, ssm_d=ssm_d, ssm_log_step=ssm_log_step, ssm_w_glu=ssm_w_glu,
             norm_ssm=norm_ssm, hgrn_lb_logits=hgrn_lb_logits, norm_hgrn=norm_hgrn, mla_q_norm=mla_q_norm,
             mla_w_uq=mla_w_uq, mla_kv_norm=mla_kv_norm, mla_w_uk=mla_w_uk, mla_w_uv=mla_w_uv, norm_mla=norm_mla)
    depth = w_in.shape[0]
    assert depth == DEPTH_ and x_sample.shape[1] == 1
    bp, seq, _ = x_prompt.shape
    bs = x_sample.shape[0]
    past_len = page_table.shape[1] * PAGE_SIZE
    w = _prepare(p)
    w["hg_lcum"], w["hg_lall"] = {}, {}
    for tt in {min(256, seq), HG_GROUP}:
        w["hg_lcum"][tt], w["hg_lall"][tt] = _group_patterns(tt)

    mod = _ada_call(jnp.concatenate([c_prompt, c_sample], 0), w_ada, b_ada)
    mod = mod.reshape(depth, bp + bs, 9, D_MODEL)
    grp_p = _Group(mod[:, :bp], bp, seq, min(512, seq))
    grp_s = _Group(mod[:, bp:], bs, 1, bs)

    rope_p = _rope_tables(jnp.arange(seq))
    rope_s = _rope_tables(past_len + jnp.arange(1))
    y_p, ckv_p, kpe_p, sre_p, sim_p, hg_p = _run_prompt(x_prompt.reshape(bp * seq, D_MODEL), grp_p, w, p, rope_p)
    y_s, ckv_s, kpe_s, sre_s, sim_s, hg_s = _run_sample(
        x_sample.reshape(bs, D_MODEL), grp_s, w, p, rope_s, cache_kv_latent, cache_k_rope, page_table,
        state_ssm_re, state_ssm_im, state_hgrn)
    return (y_p, y_s, ckv_p, ckv_s, kpe_p, kpe_s, sre_p, sre_s, sim_p, sim_s, hg_p, hg_s)
```

```python
import functools
import math

import numpy as np
import jax
import jax.numpy as jnp
from jax import lax
from jax.experimental import pallas as pl
from jax.experimental.pallas import tpu as pltpu

F32 = jnp.float32
BF16 = jnp.bfloat16

D_MODEL = 1024
D_FF = 2816
SSM_WIDTH = 256
SSM_GROUP = 16
SSM_GROUPS = 16
SSM_STATE = 64
HG_WIDTH = 256
HG_HEADS = 4
HG_DK = 64
LB_FLOOR = 1e-30
MLA_HEADS = 8
MLA_V_DIM = 64
MLA_NOPE_DIM = 64
MLA_ROPE_DIM = 32
MLA_Q_LORA = 384
MLA_KV_LORA = 256
MLA_SCALE = (MLA_NOPE_DIM + MLA_ROPE_DIM) ** -0.5
ROPE_THETA = 10000.0
NEG_MASK = -1e30
PAGE_SIZE = 128

LANES = 128
HEAD_PAD = LANES
MLA_PAD = MLA_HEADS * HEAD_PAD
N_IN_PAD = 2048
ROPE_LANE0 = MLA_NOPE_DIM
S5_CHUNK = 16
HG_GROUP = 16
VMEM_LIMIT = 56 * 1024 * 1024
DEPTH_ = 4
ALPHA = (2 * DEPTH_) ** 0.25


def _cparams(sem):
    return pltpu.CompilerParams(dimension_semantics=sem, vmem_limit_bytes=VMEM_LIMIT)


def _dot(a, b):
    return jnp.dot(a, b, preferred_element_type=F32)


def _dot_nt(a, b):
    return lax.dot_general(a, b, (((1,), (1,)), ((), ())), preferred_element_type=F32)


def _dot_tn(a, b):
    return lax.dot_general(a, b, (((0,), (0,)), ((), ())), preferred_element_type=F32)


def _split_dot(w, x, terms):
    acc = None
    r = x
    for _ in range(terms):
        p = r.astype(BF16)
        d = _dot(w, p)
        acc = d if acc is None else acc + d
        r = r - p.astype(F32)
    return acc


def _split_dot_r(x, w, terms):
    acc = None
    r = x
    for _ in range(terms):
        p = r.astype(BF16)
        d = _dot(p, w)
        acc = d if acc is None else acc + d
        r = r - p.astype(F32)
    return acc


def _layer_norm(y, g, b):
    mu = jnp.mean(y, -1, keepdims=True)
    d = y - mu
    var = jnp.mean(d * d, -1, keepdims=True)
    return d * lax.rsqrt(var + 1e-5) * g + b


def _sigmoid(x):
    return 1.0 / (1.0 + jnp.exp(-x))


def _silu(x):
    return x * _sigmoid(x)


def _gelu_tanh(x):
    c = math.sqrt(2.0 / math.pi)
    return 0.5 * x * (1.0 + jnp.tanh(c * (x + 0.044715 * (x * x * x))))


def _ada_kernel(c_ref, w_ref, b_ref, o_ref):
    c = c_ref[...]
    s = _silu(c).astype(BF16)
    o_ref[...] = _dot(s, w_ref[...].astype(BF16)) + b_ref[...]


def _ada_call(c_all, w_ada, b_ada):
    rows = c_all.shape[0]
    depth, d, n = w_ada.shape
    tn = 1152
    return pl.pallas_call(
        _ada_kernel,
        grid=(depth, n // tn),
        in_specs=[pl.BlockSpec((rows, d), lambda l, j: (0, 0)),
                  pl.BlockSpec((None, d, tn), lambda l, j: (l, 0, j)),
                  pl.BlockSpec((None, 1, tn), lambda l, j: (l, 0, j))],
        out_specs=pl.BlockSpec((None, rows, tn), lambda l, j: (l, 0, j)),
        out_shape=jax.ShapeDtypeStruct((depth, rows, n), F32),
        compiler_params=_cparams(("parallel", "parallel")),
        name="ada",
    )(c_all, w_ada, b_ada.reshape(depth, 1, n))


class _Group:
    def __init__(self, mod, batch, seq, tm):
        self.batch, self.seq, self.tm = batch, seq, tm
        self.rows = batch * seq
        self.per_row = seq == 1
        depth = mod.shape[0]
        if self.per_row:
            self.mod = jnp.transpose(mod, (0, 2, 1, 3))
        else:
            self.mod = mod.reshape(depth, batch, 9, 1, D_MODEL)
            self.tiles_per_batch = seq // tm

    def mod_spec(self, l, j):
        if self.per_row:
            return pl.BlockSpec((None, None, self.tm, D_MODEL), lambda i: (l, j, i, 0))
        tpb = self.tiles_per_batch
        return pl.BlockSpec((None, None, None, 1, D_MODEL), lambda i: (l, i // tpb, j, 0, 0))

    def pos_spec(self, width):
        if self.per_row:
            return pl.BlockSpec((1, width), lambda i: (0, 0))
        tpb = self.tiles_per_batch
        return pl.BlockSpec((self.tm, width), lambda i: (i % tpb, 0))

    def row_spec(self, width, col=0):
        return pl.BlockSpec((self.tm, width), lambda i: (i, col))

    def tcol_shape(self, height):
        return (1, height, self.batch) if self.per_row else (self.batch, height, self.seq)

    def tcol_spec(self, height):
        if self.per_row:
            return pl.BlockSpec((None, height, self.tm), lambda i: (0, 0, 0))
        tpb = self.tiles_per_batch
        return pl.BlockSpec((None, height, self.tm), lambda i: (i // tpb, 0, i % tpb))

    @property
    def grid(self):
        return (self.rows // self.tm,)


def _const_spec(shape, index):
    return pl.BlockSpec(shape, lambda i: index)


FF_CHUNK = 256


def _ffn_kernel(x_ref, sh_ref, sc_ref, gt_ref, wgu_ref, wd_ref, lg_ref, lb_ref, o_ref, a_ref):
    x = x_ref[...]
    h = (x * (1.0 + sc_ref[...]) + sh_ref[...]).astype(BF16)
    for c in range(D_FF // FF_CHUNK):
        lo = c * FF_CHUNK
        g = _dot(h, wgu_ref[:, lo:lo + FF_CHUNK])
        u = _dot(h, wgu_ref[:, D_FF + lo:D_FF + lo + FF_CHUNK])
        a_ref[:, lo:lo + FF_CHUNK] = (_silu(g) * u).astype(BF16)
    f = _dot(a_ref[...], wd_ref[...])
    y = ALPHA * x + 0.5 * (1.0 + gt_ref[...]) * f
    o_ref[...] = _layer_norm(y, lg_ref[...], lb_ref[...])


def _ffn_call(grp, x, l, which, mod_idx, wgu, wd, ln_g, ln_b):
    tm = grp.tm
    return pl.pallas_call(
        _ffn_kernel,
        grid=grp.grid,
        in_specs=[grp.row_spec(D_MODEL),
                  grp.mod_spec(l, 3 * mod_idx), grp.mod_spec(l, 3 * mod_idx + 1), grp.mod_spec(l, 3 * mod_idx + 2),
                  _const_spec((None, None, D_MODEL, 2 * D_FF), (l, which, 0, 0)),
                  _const_spec((None, None, D_FF, D_MODEL), (l, which, 0, 0)),
                  _const_spec((None, None, 1, D_MODEL), (l, mod_idx, 0, 0)),
                  _const_spec((None, None, 1, D_MODEL), (l, mod_idx, 0, 0))],
        out_specs=grp.row_spec(D_MODEL),
        out_shape=jax.ShapeDtypeStruct((grp.rows, D_MODEL), F32),
        scratch_shapes=[pltpu.VMEM((tm, D_FF), BF16)],
        compiler_params=_cparams(("parallel",)),
        name="ffn",
    )(x, grp.mod, grp.mod, grp.mod, wgu, wd, ln_g, ln_b)


Z_U = (0, 256)
Z_H = (256, 1280)
Z_CQ = (1280, 1664)
Z_CKV = (1664, 1920)
Z_KPE = (1920, 2048)


def _rope_group(t, c, s1, s2):
    return t * c + pltpu.roll(t, LANES - MLA_ROPE_DIM // 2, 1) * s1 + pltpu.roll(t, MLA_ROPE_DIM // 2, 1) * s2


def _mix_in_kernel(x_ref, sh_ref, sc_ref, win_ref, qn_ref, wuq_ref, kvn_ref, wuk_ref, wuv_ref,
                   rc_ref, rs1_ref, rs2_ref,
                   u_ref, zh_ref, qh_ref, kh_ref, vt_ref, ckv_ref, kpe_ref):
    x = x_ref[...]
    h = (x * (1.0 + sc_ref[...]) + sh_ref[...]).astype(BF16)
    u_ref[...] = _dot(h, win_ref[:, Z_U[0]:Z_U[1]])
    zh_ref[...] = _dot(h, win_ref[:, Z_H[0]:Z_H[1]])
    rc, rs1, rs2 = rc_ref[...], rs1_ref[...], rs2_ref[...]

    zcq = _dot(h, win_ref[:, Z_CQ[0]:Z_CQ[1]])
    cq = zcq * lax.rsqrt(jnp.mean(zcq * zcq, -1, keepdims=True) + 1e-6) * qn_ref[...]
    q = _dot(cq.astype(BF16), wuq_ref[...]) * MLA_SCALE
    for hd in range(MLA_HEADS):
        lo = hd * HEAD_PAD
        qh_ref[:, lo:lo + HEAD_PAD] = _rope_group(q[:, lo:lo + HEAD_PAD], rc, rs1, rs2).astype(BF16)

    zckv = _dot(h, win_ref[:, Z_CKV[0]:Z_CKV[1]])
    ckv = zckv * lax.rsqrt(jnp.mean(zckv * zckv, -1, keepdims=True) + 1e-6) * kvn_ref[...]
    ckv_ref[...] = ckv
    ckv_b = ckv.astype(BF16)
    kpe = _rope_group(_dot(h, win_ref[:, Z_KPE[0]:Z_KPE[1]]), rc, rs1, rs2)
    kpe_ref[...] = pltpu.roll(kpe, LANES - ROPE_LANE0, 1)[:, :MLA_ROPE_DIM]
    kn = _dot(ckv_b, wuk_ref[...])
    for hd in range(MLA_HEADS):
        lo = hd * HEAD_PAD
        kh_ref[:, lo:lo + HEAD_PAD] = (kn[:, lo:lo + HEAD_PAD] + kpe).astype(BF16)
    vt_ref[...] = _dot_nt(wuv_ref[...], ckv_b).astype(BF16)


def _mix_in_call(grp, x, l, w, rope):
    rows = grp.rows
    v_rows = MLA_HEADS * MLA_V_DIM
    outs = [((rows, SSM_WIDTH), F32), ((rows, 4 * HG_WIDTH), F32), ((rows, MLA_PAD), BF16),
            ((rows, MLA_PAD), BF16), (grp.tcol_shape(v_rows), BF16), ((rows, MLA_KV_LORA), F32),
            ((rows, MLA_ROPE_DIM), F32)]
    out_specs = [grp.row_spec(s[1]) for s, _ in outs]
    out_specs[4] = grp.tcol_spec(v_rows)
    return pl.pallas_call(
        _mix_in_kernel,
        grid=grp.grid,
        in_specs=[grp.row_spec(D_MODEL), grp.mod_spec(l, 3), grp.mod_spec(l, 4),
                  _const_spec((None, D_MODEL, N_IN_PAD), (l, 0, 0)),
                  _const_spec((None, 1, MLA_Q_LORA), (l, 0, 0)),
                  _const_spec((None, MLA_Q_LORA, MLA_PAD), (l, 0, 0)),
                  _const_spec((None, 1, MLA_KV_LORA), (l, 0, 0)),
                  _const_spec((None, MLA_KV_LORA, MLA_PAD), (l, 0, 0)),
                  _const_spec((None, v_rows, MLA_KV_LORA), (l, 0, 0)),
                  grp.pos_spec(LANES), grp.pos_spec(LANES), grp.pos_spec(LANES)],
        out_specs=out_specs,
        out_shape=[jax.ShapeDtypeStruct(s, d) for s, d in outs],
        compiler_params=_cparams(("parallel",)),
        name="mix_in",
    )(x, grp.mod, grp.mod, w["w_in"], w["q_norm"], w["w_uq"], w["kv_norm"], w["w_uk"], w["w_uv_t"], *rope)


def _s5_kernel(u_ref, m_ref, wis_ref, wso_ref, av_ref, d_ref, x0_ref, y_ref, xf_ref, s_scr, x_scr, *, nb, nc):
    u = u_ref[...]
    ub = u.astype(BF16)
    s_scr[...] = _dot(ub, wis_ref[...])
    a1, a2, a3 = av_ref[0:1, :], av_ref[1:2, :], av_ref[2:3, :]

    def body(c, carry):
        v, w = carry
        r = pl.multiple_of(c * nb, nb)
        x_scr[pl.ds(r, nb), :] = v
        sv = s_scr[pl.ds(r, nb), 0:LANES]
        sw = s_scr[pl.ds(r, nb), LANES:2 * LANES]
        return a1 * v + a2 * w + sv, a1 * w + a3 * v + sw

    v, _ = lax.fori_loop(0, nc, body, (x0_ref[:, 0:LANES], x0_ref[:, LANES:2 * LANES]), unroll=8)
    xf_ref[...] = v
    y_ref[...] = _dot(ub, m_ref[...]) + _dot(x_scr[...].astype(BF16), wso_ref[...]) + u * d_ref[...]


def _s5_call(u_g, x0, w, l, nb, nc):
    g = SSM_GROUPS
    rows = nb * nc
    wide = S5_CHUNK * SSM_GROUP
    return pl.pallas_call(
        functools.partial(_s5_kernel, nb=nb, nc=nc),
        grid=(g,),
        in_specs=[pl.BlockSpec((None, rows, wide), lambda i: (i, 0, 0)),
                  pl.BlockSpec((None, None, wide, wide), lambda i: (l, i, 0, 0)),
                  pl.BlockSpec((None, None, wide, wide), lambda i: (l, i, 0, 0)),
                  pl.BlockSpec((None, None, LANES, wide), lambda i: (l, i, 0, 0)),
                  pl.BlockSpec((None, None, 8, LANES), lambda i: (l, i, 0, 0)),
                  pl.BlockSpec((None, None, 1, wide), lambda i: (l, i, 0, 0)),
                  pl.BlockSpec((None, nb, wide), lambda i: (i, 0, 0))],
        out_specs=[pl.BlockSpec((None, rows, wide), lambda i: (i, 0, 0)),
                   pl.BlockSpec((None, nb, LANES), lambda i: (i, 0, 0))],
        out_shape=[jax.ShapeDtypeStruct((g, rows, wide), F32), jax.ShapeDtypeStruct((g, nb, LANES), F32)],
        scratch_shapes=[pltpu.VMEM((rows, wide), F32), pltpu.VMEM((rows, LANES), F32)],
        compiler_params=_cparams(("parallel",)),
        name="s5_chunks",
    )(u_g, w["s5_m"], w["s5_wis"], w["s5_wso"], w["s5_av"], w["s5_d"], x0)


def _s5_step_kernel(u_ref, xr_ref, xi_ref, bb_ref, cc_ref, ar_ref, ai_ref, d_ref, y_ref, or_ref, oi_ref):
    u = u_ref[...]
    n = SSM_GROUPS * SSM_STATE
    bu = _dot(u.astype(BF16), bb_ref[...])
    xr, xi, ar, ai = xr_ref[...], xi_ref[...], ar_ref[...], ai_ref[...]
    nr = ar * xr - ai * xi + bu[:, :n]
    ni = ar * xi + ai * xr + bu[:, n:]
    or_ref[...] = nr
    oi_ref[...] = ni
    y_ref[...] = (_dot(nr.astype(BF16), cc_ref[0:n, :]) + _dot(ni.astype(BF16), cc_ref[n:2 * n, :])
                  + u * d_ref[...])


def _s5_step_call(u, xr, xi, w, l):
    b = u.shape[0]
    n = SSM_GROUPS * SSM_STATE
    full = lambda shape: pl.BlockSpec(shape, lambda i: (0,) * len(shape))
    lsel = lambda shape: pl.BlockSpec((None,) + shape, lambda i: (l,) + (0,) * len(shape))
    return pl.pallas_call(
        _s5_step_kernel,
        grid=(1,),
        in_specs=[full((b, SSM_WIDTH)), full((b, n)), full((b, n)),
                  lsel((SSM_WIDTH, 2 * n)), lsel((2 * n, SSM_WIDTH)), lsel((1, n)), lsel((1, n)),
                  lsel((1, SSM_WIDTH))],
        out_specs=[full((b, SSM_WIDTH)), full((b, n)), full((b, n))],
        out_shape=[jax.ShapeDtypeStruct((b, SSM_WIDTH), F32), jax.ShapeDtypeStruct((b, n), F32),
                   jax.ShapeDtypeStruct((b, n), F32)],
        compiler_params=_cparams(("arbitrary",)),
        name="s5_step",
    )(u, xr, xi, w["s5_bb"], w["s5_cc"], w["s5_ar"], w["s5_ai"], w["s5_dflat"])


def _hgrn_kernel(zh_ref, s0_ref, lbp_ref, e_ref, bd_ref, lcum_ref, lall_ref, o_ref, sf_ref,
                 fpad, kpad, vpad, st_ref, oi_ref, *, tt, n_valid):
    j = pl.program_id(1)

    @pl.when(j == 0)
    def _():
        st_ref[...] = s0_ref[...]

    w = HG_WIDTH
    zf = zh_ref[:, 0:w]
    q = zh_ref[:, w:2 * w]
    v = zh_ref[:, 2 * w:3 * w]
    la, l1, oml = lbp_ref[0:1, :], lbp_ref[1:2, :], lbp_ref[2:3, :]
    lsig = jnp.minimum(zf, 0.0) - jnp.log1p(jnp.exp(-jnp.abs(zf)))
    t2 = l1 + lsig
    lf = jnp.maximum(la, t2) + jnp.log1p(jnp.exp(-jnp.abs(la - t2)))
    k = oml * _sigmoid(-zf)
    row = lax.broadcasted_iota(jnp.int32, (tt, w), 0)
    if n_valid < tt:
        live = row < n_valid
        lf = jnp.where(live, lf, 0.0)
        k = jnp.where(live, k, 0.0)
    f = jnp.exp(lf)

    pad = HG_GROUP
    zero_pad = jnp.zeros((pad, w), F32)
    fpad[0:pad, :] = zero_pad
    kpad[0:pad, :] = zero_pad
    vpad[0:pad, :] = zero_pad
    fpad[pad:pad + tt, :] = f
    kpad[pad:pad + tt, :] = k
    vpad[pad:pad + tt, :] = v

    in_group = row & (HG_GROUP - 1)
    e = e_ref[...]
    acc = _dot((q * k).astype(BF16), e) * v
    decay = jnp.ones((tt, w), F32)
    for d in range(1, HG_GROUP):
        decay = decay * fpad[pad - d + 1:pad - d + 1 + tt, :]
        p = jnp.where(in_group >= d, q * kpad[pad - d:pad - d + tt, :] * decay, 0.0)
        acc = acc + _dot(p.astype(BF16), e) * vpad[pad - d:pad - d + tt, :]

    b = _split_dot(lcum_ref[...], lf, 3)
    bl = _split_dot(lall_ref[...], lf, 3)
    qe = (q * jnp.exp(b)).astype(BF16)
    kx = (k * jnp.exp(bl - b)).astype(BF16)
    ebl = jnp.exp(bl)
    vb = v.astype(BF16)
    bd = bd_ref[...]
    for g in range(tt // HG_GROUP):
        lo = g * HG_GROUP
        st = st_ref[...]
        oi_ref[lo:lo + HG_GROUP, :] = _dot_nt(qe[lo:lo + HG_GROUP], st.astype(BF16))
        upd = _dot_tn(vb[lo:lo + HG_GROUP], kx[lo:lo + HG_GROUP])
        st_ref[...] = st * ebl[lo:lo + 1, :] + upd * bd
    o_ref[...] = acc + oi_ref[...]

    @pl.when(j == pl.num_programs(1) - 1)
    def _():
        sf_ref[...] = st_ref[...]


def _hgrn_call(zh, s0t, w, l, batch, seq_rows, tt, n_valid):
    wd = HG_WIDTH
    nj = seq_rows // tt
    return pl.pallas_call(
        functools.partial(_hgrn_kernel, tt=tt, n_valid=n_valid),
        grid=(batch, nj),
        in_specs=[pl.BlockSpec((tt, 4 * wd), lambda b, j: (b * nj + j, 0)),
                  pl.BlockSpec((None, wd, wd), lambda b, j: (b, 0, 0)),
                  pl.BlockSpec((None, 8, wd), lambda b, j: (l, 0, 0)),
                  pl.BlockSpec((wd, wd), lambda b, j: (0, 0)),
                  pl.BlockSpec((wd, wd), lambda b, j: (0, 0)),
                  pl.BlockSpec((tt, tt), lambda b, j: (0, 0)),
                  pl.BlockSpec((tt, tt), lambda b, j: (0, 0))],
        out_specs=[pl.BlockSpec((tt, wd), lambda b, j: (b * nj + j, 0)),
                   pl.BlockSpec((None, wd, wd), lambda b, j: (b, 0, 0))],
        out_shape=[jax.ShapeDtypeStruct((batch * seq_rows, wd), F32),
                   jax.ShapeDtypeStruct((batch, wd, wd), F32)],
        scratch_shapes=[pltpu.VMEM((HG_GROUP + tt, wd), F32)] * 3
                       + [pltpu.VMEM((wd, wd), F32), pltpu.VMEM((tt, wd), F32)],
        compiler_params=_cparams(("parallel", "arbitrary")),
        name="hgrn",
    )(zh, s0t, w["hg_lbp"], w["hg_e"], w["hg_bd"], w["hg_lcum"][tt], w["hg_lall"][tt])


def _flash_kernel(q_ref, k_ref, vt_ref, o_ref, m_ref, l_ref, acc_ref):
    qi = pl.program_id(1)
    ki = pl.program_id(2)

    @pl.when(ki == 0)
    def _():
        m_ref[...] = jnp.full(m_ref.shape, NEG_MASK, F32)
        l_ref[...] = jnp.zeros(l_ref.shape, F32)
        acc_ref[...] = jnp.zeros(acc_ref.shape, F32)

    def step(diagonal):
        for hd in range(MLA_HEADS):
            sl = slice(hd * HEAD_PAD, (hd + 1) * HEAD_PAD)
            vs = slice(hd * MLA_V_DIM, (hd + 1) * MLA_V_DIM)
            st = _dot_nt(k_ref[:, sl], q_ref[:, sl])
            if diagonal:
                key = lax.broadcasted_iota(jnp.int32, st.shape, 0)
                qry = lax.broadcasted_iota(jnp.int32, st.shape, 1)
                st = jnp.where(key <= qry, st, NEG_MASK)
            m_prev = m_ref[hd:hd + 1, :]
            m_new = jnp.maximum(m_prev, jnp.max(st, 0, keepdims=True))
            alpha = jnp.exp(m_prev - m_new)
            p = jnp.exp(st - m_new)
            l_ref[hd:hd + 1, :] = alpha * l_ref[hd:hd + 1, :] + jnp.sum(p, 0, keepdims=True)
            acc_ref[vs, :] = alpha * acc_ref[vs, :] + _dot(vt_ref[vs, :], p.astype(BF16))
            m_ref[hd:hd + 1, :] = m_new

    @pl.when(ki < qi)
    def _():
        step(False)

    @pl.when(ki == qi)
    def _():
        step(True)
        for hd in range(MLA_HEADS):
            vs = slice(hd * MLA_V_DIM, (hd + 1) * MLA_V_DIM)
            o_ref[vs, :] = acc_ref[vs, :] / l_ref[hd:hd + 1, :]


def _flash_call(qh, kh, vt, batch, seq, tq):
    nq = seq // tq
    v_rows = MLA_HEADS * MLA_V_DIM
    q_spec = pl.BlockSpec((None, tq, MLA_PAD), lambda b, qi, ki: (b, qi, 0))
    k_spec = pl.BlockSpec((None, tq, MLA_PAD), lambda b, qi, ki: (b, jnp.minimum(ki, qi), 0))
    vt_spec = pl.BlockSpec((None, v_rows, tq), lambda b, qi, ki: (b, 0, jnp.minimum(ki, qi)))
    shp = (batch, seq, MLA_PAD)
    return pl.pallas_call(
        _flash_kernel,
        grid=(batch, nq, nq),
        in_specs=[q_spec, k_spec, vt_spec],
        out_specs=pl.BlockSpec((None, v_rows, tq), lambda b, qi, ki: (b, 0, qi)),
        out_shape=jax.ShapeDtypeStruct((batch, v_rows, seq), F32),
        scratch_shapes=[pltpu.VMEM((MLA_HEADS, tq), F32), pltpu.VMEM((MLA_HEADS, tq), F32),
                        pltpu.VMEM((v_rows, tq), F32)],
        compiler_params=_cparams(("parallel", "parallel", "arbitrary")),
        name="flash",
    )(qh.reshape(shp), kh.reshape(shp), vt)


PAGES_PER_STEP = 16


def _decode_kernel(pt_ref, qh_ref, ckvn_ref, kpen_ref, wuk_ref, wuv_ref, sel_ref, hm_ref, *rest, npg):
    ckv_refs = rest[:npg]
    kpe_refs = rest[npg:2 * npg]
    o_ref = rest[2 * npg]
    qa_ref, qp_ref, m_ref, l_ref, acc_ref = rest[2 * npg + 1:]
    j = pl.program_id(1)
    hm = hm_ref[...]

    @pl.when(j == 0)
    def _():
        qbd = jnp.where(hm > 0, jnp.broadcast_to(qh_ref[...], hm.shape), jnp.zeros_like(hm)).astype(BF16)
        qa_ref[...] = _dot_nt(qbd, wuk_ref[...]).astype(BF16)
        qp_ref[...] = _dot(qbd, sel_ref[...]).astype(BF16)
        m_ref[...] = jnp.full(m_ref.shape, NEG_MASK, F32)
        l_ref[...] = jnp.zeros(l_ref.shape, F32)
        acc_ref[...] = jnp.zeros(acc_ref.shape, F32)

    qa = qa_ref[...]
    qp = qp_ref[...]
    pages = [r[...].astype(BF16) for r in ckv_refs]
    s = jnp.concatenate(
        [_dot_nt(qa, pg) + _dot(qp, kr[...].astype(BF16)) for pg, kr in zip(pages, kpe_refs)], axis=1)
    m_prev = m_ref[...]
    m_new = jnp.maximum(m_prev, jnp.max(s, -1, keepdims=True))
    alpha = jnp.exp(m_prev - m_new)
    p = jnp.exp(s - m_new)
    l_ref[...] = alpha * l_ref[...] + jnp.sum(p, -1, keepdims=True)
    pb = p.astype(BF16)
    acc = alpha * acc_ref[...]
    for i, pg in enumerate(pages):
        acc = acc + _dot(pb[:, i * PAGE_SIZE:(i + 1) * PAGE_SIZE], pg)
    acc_ref[...] = acc
    m_ref[...] = m_new

    @pl.when(j == pl.num_programs(1) - 1)
    def _():
        ckvn = ckvn_ref[...]
        s_new = (jnp.sum(qa.astype(F32) * ckvn.astype(BF16).astype(F32), -1, keepdims=True)
                 + jnp.sum(qp.astype(F32) * kpen_ref[...].astype(BF16).astype(F32), -1, keepdims=True))
        m_old = m_ref[...]
        m_fin = jnp.maximum(m_old, s_new)
        a = jnp.exp(m_old - m_fin)
        p_new = jnp.exp(s_new - m_fin)
        l_fin = a * l_ref[...] + p_new
        lat = a * acc_ref[...] + p_new.astype(BF16).astype(F32) * ckvn.astype(BF16).astype(F32)
        lat = lat / l_fin
        o = _dot(lat.astype(BF16), wuv_ref[...])
        o_ref[...] = jnp.sum(jnp.where(hm > 0, o, 0.0), 0, keepdims=True)


def _decode_call(page_table, qh, ckv_new, kpe_new, cache_ckv, cache_kpe, w, l, sel, hm):
    batch, n_pages = page_table.shape
    npg = min(PAGES_PER_STEP, n_pages)
    nj = n_pages // npg

    def page_spec(i, shape):
        return pl.BlockSpec((None, None) + shape, lambda b, j, pt: (l, pt[b, j * npg + i], 0, 0))

    row3 = lambda width: pl.BlockSpec((None, 1, width), lambda b, j, pt: (b, 0, 0))
    wsel = lambda shape: pl.BlockSpec((None,) + shape, lambda b, j, pt: (l,) + (0,) * len(shape))
    full = lambda shape: pl.BlockSpec(shape, lambda b, j, pt: (0,) * len(shape))
    grid_spec = pltpu.PrefetchScalarGridSpec(
        num_scalar_prefetch=1,
        grid=(batch, nj),
        in_specs=[row3(MLA_PAD), row3(MLA_KV_LORA), row3(MLA_ROPE_DIM),
                  wsel((MLA_KV_LORA, MLA_PAD)), wsel((MLA_KV_LORA, MLA_PAD)),
                  full((MLA_PAD, MLA_ROPE_DIM)), full((MLA_HEADS, MLA_PAD))]
                 + [page_spec(i, (PAGE_SIZE, MLA_KV_LORA)) for i in range(npg)]
                 + [page_spec(i, (MLA_ROPE_DIM, PAGE_SIZE)) for i in range(npg)],
        out_specs=row3(MLA_PAD),
        scratch_shapes=[pltpu.VMEM((MLA_HEADS, MLA_KV_LORA), BF16), pltpu.VMEM((MLA_HEADS, MLA_ROPE_DIM), BF16),
                        pltpu.VMEM((MLA_HEADS, 1), F32), pltpu.VMEM((MLA_HEADS, 1), F32),
                        pltpu.VMEM((MLA_HEADS, MLA_KV_LORA), F32)],
    )
    out = pl.pallas_call(
        functools.partial(_decode_kernel, npg=npg),
        grid_spec=grid_spec,
        out_shape=jax.ShapeDtypeStruct((batch, 1, MLA_PAD), F32),
        compiler_params=_cparams(("parallel", "arbitrary")),
        name="decode",
    )(page_table, qh.reshape(batch, 1, MLA_PAD), ckv_new.reshape(batch, 1, MLA_KV_LORA),
      kpe_new.reshape(batch, 1, MLA_ROPE_DIM), w["w_uk"], w["w_uv"], sel, hm,
      *([cache_ckv] * npg), *([cache_kpe] * npg))
    return out.reshape(batch, MLA_PAD)


def _mix_out_kernel(x_ref, gt_ref, y_ref, zg_ref, oh_ref, om_ref, wglu_ref, nssm_ref, nhg_ref, nmla_ref,
                    e64_ref, woa_ref, wob_ref, lg_ref, lb_ref, o_ref, cat_ref):
    w = SSM_WIDTH
    gab = _dot(_gelu_tanh(y_ref[...]).astype(BF16), wglu_ref[...])
    t = gab[:, :w] * _sigmoid(gab[:, w:])
    cat_ref[:, 0:w] = (t * lax.rsqrt(jnp.mean(t * t, -1, keepdims=True) + 1e-6) * nssm_ref[...]).astype(BF16)

    oh = oh_ref[...]
    msq = _split_dot_r(oh * oh, e64_ref[...], 2)
    cat_ref[:, w:2 * w] = (oh * lax.rsqrt(msq + 1e-6) * nhg_ref[...] * _silu(zg_ref[...])).astype(BF16)

    om = om_ref[...]
    tm = om.shape[1]
    nm = nmla_ref[...]
    nm = nm[:, :tm] if tm <= LANES else jnp.tile(nm, (1, tm // LANES))
    ms = jnp.mean(om * om, 0, keepdims=True)
    o_mla = (om * lax.rsqrt(ms + 1e-6) * nm).astype(BF16)

    mix = _dot(cat_ref[...], woa_ref[...]) + _dot_tn(o_mla, wob_ref[...])
    y = ALPHA * x_ref[...] + (1.0 + gt_ref[...]) * mix
    o_ref[...] = _layer_norm(y, lg_ref[...], lb_ref[...])


def _mix_out_call(grp, x, l, y_ssm, zh, o_h, o_m, w, ln_g, ln_b):
    tm = grp.tm
    v_rows = MLA_HEADS * MLA_V_DIM
    return pl.pallas_call(
        _mix_out_kernel,
        grid=grp.grid,
        in_specs=[grp.row_spec(D_MODEL), grp.mod_spec(l, 5),
                  grp.row_spec(SSM_WIDTH), grp.row_spec(HG_WIDTH, col=3), grp.row_spec(HG_WIDTH),
                  grp.tcol_spec(v_rows),
                  _const_spec((None, SSM_WIDTH, 2 * SSM_WIDTH), (l, 0, 0)),
                  _const_spec((None, 1, SSM_WIDTH), (l, 0, 0)),
                  _const_spec((None, 1, HG_WIDTH), (l, 0, 0)),
                  _const_spec((None, v_rows, LANES), (l, 0, 0)),
                  _const_spec((HG_WIDTH, HG_WIDTH), (0, 0)),
                  _const_spec((None, SSM_WIDTH + HG_WIDTH, D_MODEL), (l, 0, 0)),
                  _const_spec((None, v_rows, D_MODEL), (l, 0, 0)),
                  _const_spec((None, None, 1, D_MODEL), (l, 1, 0, 0)),
                  _const_spec((None, None, 1, D_MODEL), (l, 1, 0, 0))],
        out_specs=grp.row_spec(D_MODEL),
        out_shape=jax.ShapeDtypeStruct((grp.rows, D_MODEL), F32),
        scratch_shapes=[pltpu.VMEM((tm, SSM_WIDTH + HG_WIDTH), BF16)],
        compiler_params=_cparams(("parallel",)),
        name="mix_out",
    )(x, grp.mod, y_ssm, zh, o_h, o_m, w["w_glu"], w["n_ssm"], w["n_hg"], w["n_mla"], w["hg_e64"],
      w["w_out_a"], w["w_out_b"], ln_g, ln_b)


def _pad_heads(wt, per_head):
    lead = wt.shape[:-1]
    wt = wt.reshape(lead + (MLA_HEADS, per_head))
    wt = jnp.pad(wt, [(0, 0)] * len(lead) + [(0, 0), (0, HEAD_PAD - per_head)])
    return wt.reshape(lead + (MLA_PAD,))


def _s5_tables(lam_re, lam_im, b_re, b_im, c_re, c_im, d, log_step):
    hp = lax.Precision.HIGHEST
    depth, g, p = lam_re.shape
    h, t = SSM_GROUP, S5_CHUNK
    step = jnp.exp(log_step)[..., None]
    mag = jnp.exp(lam_re * step)
    ab_re, ab_im = mag * jnp.cos(lam_im * step), mag * jnp.sin(lam_im * step)
    den = lam_re * lam_re + lam_im * lam_im
    nr = ab_re - 1.0
    coef_re = (nr * lam_re + ab_im * lam_im) / den
    coef_im = (ab_im * lam_re - nr * lam_im) / den
    bb_re = coef_re[..., None] * b_re - coef_im[..., None] * b_im
    bb_im = coef_re[..., None] * b_im + coef_im[..., None] * b_re
    n = jnp.arange(t + 1, dtype=F32)[:, None, None, None]
    pmag = jnp.exp(n * lam_re * step)
    pw_re, pw_im = pmag * jnp.cos(n * lam_im * step), pmag * jnp.sin(n * lam_im * step)
    pb_re = pw_re[:t, ..., None] * bb_re - pw_im[:t, ..., None] * bb_im
    pb_im = pw_re[:t, ..., None] * bb_im + pw_im[:t, ..., None] * bb_re
    kern = (jnp.einsum('lgop,nlgpi->nlgoi', c_re, pb_re, precision=hp)
            - jnp.einsum('lgop,nlgpi->nlgoi', c_im, pb_im, precision=hp))
    kern = jnp.concatenate([kern, jnp.zeros_like(kern[:1])], 0)
    s_idx, t_idx = np.arange(t)[:, None], np.arange(t)[None, :]
    lag = np.where(t_idx >= s_idx, t_idx - s_idx, t)
    m = kern[lag]
    m = jnp.transpose(m, (2, 3, 0, 5, 1, 4)).reshape(depth, g, t * h, t * h)
    wr = jnp.transpose(pb_re[::-1], (1, 2, 0, 4, 3)).reshape(depth, g, t * h, p)
    wi = jnp.transpose(pb_im[::-1], (1, 2, 0, 4, 3)).reshape(depth, g, t * h, p)
    wis = jnp.concatenate([wr, wi, wi, wr], -1)
    cp_re = c_re[None] * pw_re[1:, :, :, None, :] - c_im[None] * pw_im[1:, :, :, None, :]
    cp_im = c_re[None] * pw_im[1:, :, :, None, :] + c_im[None] * pw_re[1:, :, :, None, :]
    so_re = jnp.transpose(cp_re, (1, 2, 4, 0, 3)).reshape(depth, g, p, t * h)
    so_im = jnp.transpose(-cp_im, (1, 2, 4, 0, 3)).reshape(depth, g, p, t * h)
    wso = jnp.concatenate([so_re, so_im], 2)
    a_re, a_im = pw_re[t], pw_im[t]
    av = jnp.stack([jnp.concatenate([a_re, a_re], -1), jnp.concatenate([-a_im, a_im], -1),
                    jnp.concatenate([a_im, -a_im], -1)], 2)
    av = jnp.pad(av, ((0, 0), (0, 0), (0, 5), (0, 0)))
    dg = d.reshape(depth, g, 1, h)
    d_t = jnp.tile(dg, (1, 1, 1, t))
    eye = jnp.eye(g, dtype=F32)
    bbr = jnp.einsum('lgph,gk->lghkp', bb_re, eye).reshape(depth, g * h, g * p)
    bbi = jnp.einsum('lgph,gk->lghkp', bb_im, eye).reshape(depth, g * h, g * p)
    ccr = jnp.einsum('lghp,gk->lgpkh', c_re, eye).reshape(depth, g * p, g * h)
    cci = jnp.einsum('lghp,gk->lgpkh', -c_im, eye).reshape(depth, g * p, g * h)
    return dict(s5_m=m.astype(BF16), s5_wis=wis.astype(BF16), s5_wso=wso.astype(BF16), s5_av=av, s5_d=d_t,
                s5_bb=jnp.concatenate([bbr, bbi], -1).astype(BF16),
                s5_cc=jnp.concatenate([ccr, cci], 1).astype(BF16),
                s5_ar=ab_re.reshape(depth, 1, g * p), s5_ai=ab_im.reshape(depth, 1, g * p),
                s5_dflat=d.reshape(depth, 1, g * h))


def _rope_tables(pos):
    half = MLA_ROPE_DIM // 2
    inv = 1.0 / (ROPE_THETA ** (jnp.arange(0, MLA_ROPE_DIM, 2, dtype=F32) / MLA_ROPE_DIM))
    ang = pos.astype(F32)[:, None] * inv[None, :]
    cos, sin = jnp.cos(ang), jnp.sin(ang)
    n = pos.shape[0]
    one, zero = jnp.ones((n, ROPE_LANE0), F32), jnp.zeros((n, ROPE_LANE0), F32)
    z16, z32 = jnp.zeros((n, half), F32), jnp.zeros((n, LANES - ROPE_LANE0 - MLA_ROPE_DIM), F32)
    rc = jnp.concatenate([one, cos, cos, z32], 1)
    rs1 = jnp.concatenate([zero, -sin, z16, z32], 1)
    rs2 = jnp.concatenate([zero, z16, sin, z32], 1)
    return rc, rs1, rs2


def _group_patterns(tt):
    r = np.arange(tt)
    same = (r[:, None] // HG_GROUP) == (r[None, :] // HG_GROUP)
    lcum = same & (r[None, :] <= r[:, None])
    return jnp.asarray(lcum, BF16), jnp.asarray(same, BF16)


def _prepare(p):
    depth = p["w_in"].shape[0]
    w = {}
    w["ffn_w_gu"] = p["ffn_w_gu"].astype(BF16)
    w["ffn_w_down"] = p["ffn_w_down"].astype(BF16)
    w_in = p["w_in"]
    n_main = Z_CKV[1]
    zpad = lambda n: jnp.zeros((depth, D_MODEL, n), w_in.dtype)
    w["w_in"] = jnp.concatenate(
        [w_in[:, :, :n_main], zpad(ROPE_LANE0), w_in[:, :, n_main:], zpad(LANES - ROPE_LANE0 - MLA_ROPE_DIM)],
        -1).astype(BF16)
    w["w_uq"] = _pad_heads(p["mla_w_uq"], MLA_NOPE_DIM + MLA_ROPE_DIM).astype(BF16)
    w["w_uk"] = _pad_heads(p["mla_w_uk"], MLA_NOPE_DIM).astype(BF16)
    w["w_uv"] = _pad_heads(p["mla_w_uv"], MLA_V_DIM).astype(BF16)
    w["w_uv_t"] = jnp.swapaxes(p["mla_w_uv"], 1, 2).astype(BF16)
    w["q_norm"] = p["mla_q_norm"].reshape(depth, 1, MLA_Q_LORA)
    w["kv_norm"] = p["mla_kv_norm"].reshape(depth, 1, MLA_KV_LORA)
    w["w_glu"] = p["ssm_w_glu"].astype(BF16)
    w["n_ssm"] = p["norm_ssm"].reshape(depth, 1, SSM_WIDTH)
    w["n_hg"] = p["norm_hgrn"].reshape(depth, 1, HG_WIDTH)
    w["n_mla"] = jnp.broadcast_to(p["norm_mla"][:, :, None], (depth, MLA_HEADS * MLA_V_DIM, LANES))
    w_out = p["w_out"]
    split = SSM_WIDTH + HG_WIDTH
    w["w_out_a"] = w_out[:, :split].astype(BF16)
    w["w_out_b"] = w_out[:, split:].astype(BF16)
    w.update(_s5_tables(p["ssm_lambda_re"].astype(F32), p["ssm_lambda_im"].astype(F32),
                        p["ssm_b_re"].astype(F32), p["ssm_b_im"].astype(F32),
                        p["ssm_c_re"].astype(F32), p["ssm_c_im"].astype(F32),
                        p["ssm_d"].astype(F32), p["ssm_log_step"].astype(F32)))
    sm = jax.nn.softmax(p["hgrn_lb_logits"].astype(F32), axis=0)
    lb = jnp.clip(jnp.cumsum(sm, axis=0) - sm[0:1], 0.0, 1.0 - 1e-6)
    rows = jnp.stack([jnp.log(jnp.maximum(lb, LB_FLOOR)), jnp.log1p(-lb), 1.0 - lb], 1)
    w["hg_lbp"] = jnp.pad(rows, ((0, 0), (0, 5), (0, 0)))
    head = np.arange(HG_WIDTH) // HG_DK
    same_head = head[:, None] == head[None, :]
    w["hg_e"] = jnp.asarray(same_head, BF16)
    w["hg_bd"] = jnp.asarray(same_head, F32)
    w["hg_e64"] = jnp.asarray(same_head / HG_DK, BF16)
    return w


def _hg_state_in(s):
    b = s.shape[0]
    eye = jnp.eye(HG_HEADS, dtype=s.dtype)
    return jnp.einsum('bhkv,hg->bhvgk', s, eye).reshape(b, HG_WIDTH, HG_WIDTH)


def _hg_state_out(st):
    b = st.shape[0]
    st = st.reshape(b, HG_HEADS, HG_DK, HG_HEADS, HG_DK)
    diag = jnp.stack([st[:, h, :, h, :] for h in range(HG_HEADS)], 1)
    return jnp.transpose(diag, (0, 1, 3, 2))


def _run_prompt(x, grp, w, p, rope):
    batch, seq = grp.batch, grp.seq
    depth = p["w_in"].shape[0]
    nc = seq // S5_CHUNK
    tt = min(256, seq)
    tq = min(512, seq)
    ln_g = p["ln_g"].reshape(depth, 3, 1, D_MODEL)
    ln_b = p["ln_b"].reshape(depth, 3, 1, D_MODEL)
    x0 = jnp.zeros((SSM_GROUPS, batch, 2 * LANES), F32)
    s0 = jnp.zeros((batch, HG_WIDTH, HG_WIDTH), F32)
    ckv_l, kpe_l, sre_l, sim_l, hg_l = [], [], [], [], []
    for l in range(depth):
        x = _ffn_call(grp, x, l, 0, 0, w["ffn_w_gu"], w["ffn_w_down"], ln_g, ln_b)
        u, zh, qh, kh, vt, ckv, kpe = _mix_in_call(grp, x, l, w, rope)
        u_g = jnp.transpose(u.reshape(batch, nc, S5_CHUNK, SSM_GROUPS, SSM_GROUP), (3, 1, 0, 2, 4))
        y_g, xf = _s5_call(u_g.reshape(SSM_GROUPS, nc * batch, S5_CHUNK * SSM_GROUP), x0, w, l, batch, nc)
        y = jnp.transpose(y_g.reshape(SSM_GROUPS, nc, batch, S5_CHUNK, SSM_GROUP), (2, 1, 3, 0, 4))
        y = y.reshape(batch * seq, SSM_WIDTH)
        o_h, st = _hgrn_call(zh, s0, w, l, batch, seq, tt, tt)
        o_m = _flash_call(qh, kh, vt, batch, seq, tq)
        x = _mix_out_call(grp, x, l, y, zh, o_h, o_m, w, ln_g, ln_b)
        x = _ffn_call(grp, x, l, 1, 2, w["ffn_w_gu"], w["ffn_w_down"], ln_g, ln_b)
        ckv_l.append(ckv.reshape(batch, seq, MLA_KV_LORA))
        kpe_l.append(kpe.reshape(batch, seq, MLA_ROPE_DIM))
        xf = jnp.transpose(xf, (1, 0, 2))
        sre_l.append(xf[..., :SSM_STATE])
        sim_l.append(xf[..., SSM_STATE:])
        hg_l.append(_hg_state_out(st))
    return (x.reshape(batch, seq, D_MODEL), jnp.stack(ckv_l), jnp.stack(kpe_l), jnp.stack(sre_l),
            jnp.stack(sim_l), jnp.stack(hg_l))


def _run_sample(x, grp, w, p, rope, cache_ckv, cache_kpe, page_table, ssm_re, ssm_im, hg_state):
    batch = grp.batch
    depth = p["w_in"].shape[0]
    ln_g = p["ln_g"].reshape(depth, 3, 1, D_MODEL)
    ln_b = p["ln_b"].reshape(depth, 3, 1, D_MODEL)
    n = SSM_GROUPS * SSM_STATE
    lane = np.arange(MLA_PAD)
    sel = np.zeros((MLA_PAD, MLA_ROPE_DIM), np.float32)
    for i in range(MLA_ROPE_DIM):
        sel[(lane % HEAD_PAD) == ROPE_LANE0 + i, i] = 1.0
    sel = jnp.asarray(sel, BF16)
    hm = jnp.asarray((lane[None, :] // HEAD_PAD) == np.arange(MLA_HEADS)[:, None], F32)
    cache_kpe = jnp.swapaxes(cache_kpe, 2, 3)
    ckv_l, kpe_l, sre_l, sim_l, hg_l = [], [], [], [], []
    for l in range(depth):
        x = _ffn_call(grp, x, l, 0, 0, w["ffn_w_gu"], w["ffn_w_down"], ln_g, ln_b)
        u, zh, qh, kh, _, ckv, kpe = _mix_in_call(grp, x, l, w, rope)
        y, nr, ni = _s5_step_call(u, ssm_re[l].reshape(batch, n), ssm_im[l].reshape(batch, n), w, l)
        zh_pad = jnp.pad(zh[:, None, :], ((0, 0), (0, HG_GROUP - 1), (0, 0))).reshape(batch * HG_GROUP, -1)
        o_hp, st = _hgrn_call(zh_pad, _hg_state_in(hg_state[l]), w, l, batch, HG_GROUP, HG_GROUP, 1)
        o_h = o_hp.reshape(batch, HG_GROUP, HG_WIDTH)[:, 0]
        o_m = _decode_call(page_table, qh, ckv, kpe, cache_ckv, cache_kpe, w, l, sel, hm)
        o_m = o_m.reshape(batch, MLA_HEADS, HEAD_PAD)[:, :, :MLA_V_DIM].reshape(batch, -1).T[None]
        x = _mix_out_call(grp, x, l, y, zh, o_h, o_m, w, ln_g, ln_b)
        x = _ffn_call(grp, x, l, 1, 2, w["ffn_w_gu"], w["ffn_w_down"], ln_g, ln_b)
        ckv_l.append(ckv.reshape(batch, 1, MLA_KV_LORA))
        kpe_l.append(kpe.reshape(batch, 1, MLA_ROPE_DIM))
        sre_l.append(nr.reshape(batch, SSM_GROUPS, SSM_STATE))
        sim_l.append(ni.reshape(batch, SSM_GROUPS, SSM_STATE))
        hg_l.append(_hg_state_out(st))
    return (x.reshape(batch, 1, D_MODEL), jnp.stack(ckv_l), jnp.stack(kpe_l), jnp.stack(sre_l),
            jnp.stack(sim_l), jnp.stack(hg_l))


def kernel(x_prompt, x_sample, cache_kv_latent, cache_k_rope, state_ssm_re, state_ssm_im, state_hgrn, page_table, c_prompt, c_sample, w_ada, b_ada, ln_g, ln_b, ffn_w_gu, ffn_w_down, w_in, w_out, ssm_lambda_re, ssm_lambda_im, ssm_b_re, ssm_b_im, ssm_c_re, ssm_c_im, ssm_d, ssm_log_step, ssm_w_glu, norm_ssm, hgrn_lb_logits, norm_hgrn, mla_q_norm, mla_w_uq, mla_kv_norm, mla_w_uk, mla_w_uv, norm_mla):
    p = dict(ln_g=ln_g, ln_b=ln_b, ffn_w_gu=ffn_w_gu, ffn_w_down=ffn_w_down, w_in=w_in, w_out=w_out,
             ssm_lambda_re=ssm_lambda_re, ssm_lambda_im=ssm_lambda_im, ssm_b_re=ssm_b_re, ssm_b_im=ssm_b_im,
             ssm_c_re=ssm_c_re, ssm_c_im=ssm_c_im, ssm_d=ssm_d, ssm_log_step=ssm_log_step, ssm_w_glu=ssm_w_glu,
             norm_ssm=norm_ssm, hgrn_lb_logits=hgrn_lb_logits, norm_hgrn=norm_hgrn, mla_q_norm=mla_q_norm,
             mla_w_uq=mla_w_uq, mla_kv_norm=mla_kv_norm, mla_w_uk=mla_w_uk, mla_w_uv=mla_w_uv, norm_mla=norm_mla)
    depth = w_in.shape[0]
    assert depth == DEPTH_ and x_sample.shape[1] == 1
    bp, seq, _ = x_prompt.shape
    bs = x_sample.shape[0]
    past_len = page_table.shape[1] * PAGE_SIZE
    w = _prepare(p)
    w["hg_lcum"], w["hg_lall"] = {}, {}
    for tt in {min(256, seq), HG_GROUP}:
        w["hg_lcum"][tt], w["hg_lall"][tt] = _group_patterns(tt)

    mod = _ada_call(jnp.concatenate([c_prompt, c_sample], 0), w_ada, b_ada)
    mod = mod.reshape(depth, bp + bs, 9, D_MODEL)
    grp_p = _Group(mod[:, :bp], bp, seq, min(512, seq))
    grp_s = _Group(mod[:, bp:], bs, 1, bs)

    rope_p = _rope_tables(jnp.arange(seq))
    rope_s = _rope_tables(past_len + jnp.arange(1))
    y_p, ckv_p, kpe_p, sre_p, sim_p, hg_p = _run_prompt(x_prompt.reshape(bp * seq, D_MODEL), grp_p, w, p, rope_p)
    y_s, ckv_s, kpe_s, sre_s, sim_s, hg_s = _run_sample(
        x_sample.reshape(bs, D_MODEL), grp_s, w, p, rope_s, cache_kv_latent, cache_k_rope, page_table,
        state_ssm_re, state_ssm_im, state_hgrn)
    return (y_p, y_s, ckv_p, ckv_s, kpe_p, kpe_s, sre_p, sre_s, sim_p, sim_s, hg_p, hg_s)
```

```python
import functools
import math

import numpy as np
import jax
import jax.numpy as jnp
from jax import lax
from jax.experimental import pallas as pl
from jax.experimental.pallas import tpu as pltpu

F32 = jnp.float32
BF16 = jnp.bfloat16

D_MODEL = 1024
D_FF = 2816
SSM_WIDTH = 256
SSM_GROUP = 16
SSM_GROUPS = 16
SSM_STATE = 64
HG_WIDTH = 256
HG_HEADS = 4
HG_DK = 64
LB_FLOOR = 1e-30
MLA_HEADS = 8
MLA_V_DIM = 64
MLA_NOPE_DIM = 64
MLA_ROPE_DIM = 32
MLA_Q_LORA = 384
MLA_KV_LORA = 256
MLA_SCALE = (MLA_NOPE_DIM + MLA_ROPE_DIM) ** -0.5
ROPE_THETA = 10000.0
NEG_MASK = -1e30
PAGE_SIZE = 128

LANES = 128
HEAD_PAD = LANES
MLA_PAD = MLA_HEADS * HEAD_PAD
N_IN_PAD = 2048
ROPE_LANE0 = MLA_NOPE_DIM
S5_CHUNK = 16
HG_GROUP = 16
VMEM_LIMIT = 56 * 1024 * 1024
DEPTH_ = 4
ALPHA = (2 * DEPTH_) ** 0.25


def _cparams(sem):
    return pltpu.CompilerParams(dimension_semantics=sem, vmem_limit_bytes=VMEM_LIMIT)


def _dot(a, b):
    return jnp.dot(a, b, preferred_element_type=F32)


def _dot_nt(a, b):
    return lax.dot_general(a, b, (((1,), (1,)), ((), ())), preferred_element_type=F32)


def _dot_tn(a, b):
    return lax.dot_general(a, b, (((0,), (0,)), ((), ())), preferred_element_type=F32)


def _split_dot(w, x, terms):
    acc = None
    r = x
    for _ in range(terms):
        p = r.astype(BF16)
        d = _dot(w, p)
        acc = d if acc is None else acc + d
        r = r - p.astype(F32)
    return acc


def _split_dot_r(x, w, terms):
    acc = None
    r = x
    for _ in range(terms):
        p = r.astype(BF16)
        d = _dot(p, w)
        acc = d if acc is None else acc + d
        r = r - p.astype(F32)
    return acc


def _layer_norm(y, g, b):
    mu = jnp.mean(y, -1, keepdims=True)
    d = y - mu
    var = jnp.mean(d * d, -1, keepdims=True)
    return d * lax.rsqrt(var + 1e-5) * g + b


def _sigmoid(x):
    return 1.0 / (1.0 + jnp.exp(-x))


def _silu(x):
    return x * _sigmoid(x)


def _gelu_tanh(x):
    c = math.sqrt(2.0 / math.pi)
    return 0.5 * x * (1.0 + jnp.tanh(c * (x + 0.044715 * (x * x * x))))


def _ada_kernel(c_ref, w_ref, b_ref, o_ref):
    c = c_ref[...]
    s = _silu(c).astype(BF16)
    o_ref[...] = _dot(s, w_ref[...].astype(BF16)) + b_ref[...]


def _ada_call(c_all, w_ada, b_ada):
    rows = c_all.shape[0]
    depth, d, n = w_ada.shape
    tn = 1152
    return pl.pallas_call(
        _ada_kernel,
        grid=(depth, n // tn),
        in_specs=[pl.BlockSpec((rows, d), lambda l, j: (0, 0)),
                  pl.BlockSpec((None, d, tn), lambda l, j: (l, 0, j)),
                  pl.BlockSpec((None, 1, tn), lambda l, j: (l, 0, j))],
        out_specs=pl.BlockSpec((None, rows, tn), lambda l, j: (l, 0, j)),
        out_shape=jax.ShapeDtypeStruct((depth, rows, n), F32),
        compiler_params=_cparams(("parallel", "parallel")),
        name="ada",
    )(c_all, w_ada, b_ada.reshape(depth, 1, n))


class _Group:
    def __init__(self, mod, batch, seq, tm):
        self.batch, self.seq, self.tm = batch, seq, tm
        self.rows = batch * seq
        self.per_row = seq == 1
        depth = mod.shape[0]
        if self.per_row:
            self.mod = jnp.transpose(mod, (0, 2, 1, 3))
        else:
            self.mod = mod.reshape(depth, batch, 9, 1, D_MODEL)
            self.tiles_per_batch = seq // tm

    def mod_spec(self, l, j):
        if self.per_row:
            return pl.BlockSpec((None, None, self.tm, D_MODEL), lambda i: (l, j, i, 0))
        tpb = self.tiles_per_batch
        return pl.BlockSpec((None, None, None, 1, D_MODEL), lambda i: (l, i // tpb, j, 0, 0))

    def pos_spec(self, width):
        if self.per_row:
            return pl.BlockSpec((1, width), lambda i: (0, 0))
        tpb = self.tiles_per_batch
        return pl.BlockSpec((self.tm, width), lambda i: (i % tpb, 0))

    def row_spec(self, width, col=0):
        return pl.BlockSpec((self.tm, width), lambda i: (i, col))

    def tcol_shape(self, height):
        return (1, height, self.batch) if self.per_row else (self.batch, height, self.seq)

    def tcol_spec(self, height):
        if self.per_row:
            return pl.BlockSpec((None, height, self.tm), lambda i: (0, 0, 0))
        tpb = self.tiles_per_batch
        return pl.BlockSpec((None, height, self.tm), lambda i: (i // tpb, 0, i % tpb))

    @property
    def grid(self):
        return (self.rows // self.tm,)


def _const_spec(shape, index):
    return pl.BlockSpec(shape, lambda i: index)


FF_CHUNK = 256


def _ffn_kernel(x_ref, sh_ref, sc_ref, gt_ref, wgu_ref, wd_ref, lg_ref, lb_ref, o_ref, a_ref):
    x = x_ref[...]
    h = (x * (1.0 + sc_ref[...]) + sh_ref[...]).astype(BF16)
    for c in range(D_FF // FF_CHUNK):
        lo = c * FF_CHUNK
        g = _dot(h, wgu_ref[:, lo:lo + FF_CHUNK])
        u = _dot(h, wgu_ref[:, D_FF + lo:D_FF + lo + FF_CHUNK])
        a_ref[:, lo:lo + FF_CHUNK] = (_silu(g) * u).astype(BF16)
    f = _dot(a_ref[...], wd_ref[...])
    y = ALPHA * x + 0.5 * (1.0 + gt_ref[...]) * f
    o_ref[...] = _layer_norm(y, lg_ref[...], lb_ref[...])


def _ffn_call(grp, x, l, which, mod_idx, wgu, wd, ln_g, ln_b):
    tm = grp.tm
    return pl.pallas_call(
        _ffn_kernel,
        grid=grp.grid,
        in_specs=[grp.row_spec(D_MODEL),
                  grp.mod_spec(l, 3 * mod_idx), grp.mod_spec(l, 3 * mod_idx + 1), grp.mod_spec(l, 3 * mod_idx + 2),
                  _const_spec((None, None, D_MODEL, 2 * D_FF), (l, which, 0, 0)),
                  _const_spec((None, None, D_FF, D_MODEL), (l, which, 0, 0)),
                  _const_spec((None, None, 1, D_MODEL), (l, mod_idx, 0, 0)),
                  _const_spec((None, None, 1, D_MODEL), (l, mod_idx, 0, 0))],
        out_specs=grp.row_spec(D_MODEL),
        out_shape=jax.ShapeDtypeStruct((grp.rows, D_MODEL), F32),
        scratch_shapes=[pltpu.VMEM((tm, D_FF), BF16)],
        compiler_params=_cparams(("parallel",)),
        name="ffn",
    )(x, grp.mod, grp.mod, grp.mod, wgu, wd, ln_g, ln_b)


Z_U = (0, 256)
Z_H = (256, 1280)
Z_CQ = (1280, 1664)
Z_CKV = (1664, 1920)
Z_KPE = (1920, 2048)


def _rope_group(t, c, s1, s2):
    return t * c + pltpu.roll(t, LANES - MLA_ROPE_DIM // 2, 1) * s1 + pltpu.roll(t, MLA_ROPE_DIM // 2, 1) * s2


def _mix_in_kernel(x_ref, sh_ref, sc_ref, win_ref, qn_ref, wuq_ref, kvn_ref, wuk_ref, wuv_ref,
                   rc_ref, rs1_ref, rs2_ref,
                   u_ref, zh_ref, qh_ref, kh_ref, vt_ref, ckv_ref, kpe_ref):
    x = x_ref[...]
    h = (x * (1.0 + sc_ref[...]) + sh_ref[...]).astype(BF16)
    u_ref[...] = _dot(h, win_ref[:, Z_U[0]:Z_U[1]])
    zh_ref[...] = _dot(h, win_ref[:, Z_H[0]:Z_H[1]])
    rc, rs1, rs2 = rc_ref[...], rs1_ref[...], rs2_ref[...]

    zcq = _dot(h, win_ref[:, Z_CQ[0]:Z_CQ[1]])
    cq = zcq * lax.rsqrt(jnp.mean(zcq * zcq, -1, keepdims=True) + 1e-6) * qn_ref[...]
    q = _dot(cq.astype(BF16), wuq_ref[...]) * MLA_SCALE
    for hd in range(MLA_HEADS):
        lo = hd * HEAD_PAD
        qh_ref[:, lo:lo + HEAD_PAD] = _rope_group(q[:, lo:lo + HEAD_PAD], rc, rs1, rs2).astype(BF16)

    zckv = _dot(h, win_ref[:, Z_CKV[0]:Z_CKV[1]])
    ckv = zckv * lax.rsqrt(jnp.mean(zckv * zckv, -1, keepdims=True) + 1e-6) * kvn_ref[...]
    ckv_ref[...] = ckv
    ckv_b = ckv.astype(BF16)
    kpe = _rope_group(_dot(h, win_ref[:, Z_KPE[0]:Z_KPE[1]]), rc, rs1, rs2)
    kpe_ref[...] = pltpu.roll(kpe, LANES - ROPE_LANE0, 1)[:, :MLA_ROPE_DIM]
    kn = _dot(ckv_b, wuk_ref[...])
    for hd in range(MLA_HEADS):
        lo = hd * HEAD_PAD
        kh_ref[:, lo:lo + HEAD_PAD] = (kn[:, lo:lo + HEAD_PAD] + kpe).astype(BF16)
    vt_ref[...] = _dot_nt(wuv_ref[...], ckv_b).astype(BF16)


def _mix_in_call(grp, x, l, w, rope):
    rows = grp.rows
    v_rows = MLA_HEADS * MLA_V_DIM
    outs = [((rows, SSM_WIDTH), F32), ((rows, 4 * HG_WIDTH), F32), ((rows, MLA_PAD), BF16),
            ((rows, MLA_PAD), BF16), (grp.tcol_shape(v_rows), BF16), ((rows, MLA_KV_LORA), F32),
            ((rows, MLA_ROPE_DIM), F32)]
    out_specs = [grp.row_spec(s[1]) for s, _ in outs]
    out_specs[4] = grp.tcol_spec(v_rows)
    return pl.pallas_call(
        _mix_in_kernel,
        grid=grp.grid,
        in_specs=[grp.row_spec(D_MODEL), grp.mod_spec(l, 3), grp.mod_spec(l, 4),
                  _const_spec((None, D_MODEL, N_IN_PAD), (l, 0, 0)),
                  _const_spec((None, 1, MLA_Q_LORA), (l, 0, 0)),
                  _const_spec((None, MLA_Q_LORA, MLA_PAD), (l, 0, 0)),
                  _const_spec((None, 1, MLA_KV_LORA), (l, 0, 0)),
                  _const_spec((None, MLA_KV_LORA, MLA_PAD), (l, 0, 0)),
                  _const_spec((None, v_rows, MLA_KV_LORA), (l, 0, 0)),
                  grp.pos_spec(LANES), grp.pos_spec(LANES), grp.pos_spec(LANES)],
        out_specs=out_specs,
        out_shape=[jax.ShapeDtypeStruct(s, d) for s, d in outs],
        compiler_params=_cparams(("parallel",)),
        name="mix_in",
    )(x, grp.mod, grp.mod, w["w_in"], w["q_norm"], w["w_uq"], w["kv_norm"], w["w_uk"], w["w_uv_t"], *rope)


def _s5_kernel(u_ref, m_ref, wis_ref, wso_ref, av_ref, d_ref, x0_ref, y_ref, xf_ref, sv_scr, sw_scr, x_scr,
               *, nb, nc):
    u = u_ref[...]
    ub = u.astype(BF16)
    sv_scr[...] = _dot(ub, wis_ref[:, 0:LANES])
    sw_scr[...] = _dot(ub, wis_ref[:, LANES:2 * LANES])
    a1, a2, a3 = av_ref[0:1, :], av_ref[1:2, :], av_ref[2:3, :]

    def body(c, carry):
        v, w = carry
        rows_c = pl.ds(c, nb, stride=nc)
        x_scr[rows_c, :] = v
        return a1 * v + a2 * w + sv_scr[rows_c, :], a1 * w + a3 * v + sw_scr[rows_c, :]

    v, _ = lax.fori_loop(0, nc, body, (x0_ref[:, 0:LANES], x0_ref[:, LANES:2 * LANES]), unroll=8)
    xf_ref[...] = v
    y_ref[...] = _dot(ub, m_ref[...]) + _dot(x_scr[...].astype(BF16), wso_ref[...]) + u * d_ref[...]


def _s5_pack_kernel(u_ref, sel_ref, o_ref, lo_scr, hi_scr, *, nch):
    g = SSM_GROUPS
    lo_scr[...] = u_ref[:, 0:LANES]
    hi_scr[...] = u_ref[:, LANES:2 * LANES]
    lane_grp = lax.broadcasted_iota(jnp.int32, (g * nch, SSM_WIDTH), 1) // SSM_GROUP
    row_grp = lax.broadcasted_iota(jnp.int32, (g * nch, SSM_WIDTH), 0) // nch
    keep = lane_grp == row_grp
    acc = jnp.zeros((g * nch, SSM_WIDTH), F32)
    for t in range(S5_CHUNK):
        step_rows = pl.ds(t, nch, stride=S5_CHUNK)
        rows = jnp.concatenate([lo_scr[step_rows, :], hi_scr[step_rows, :]], 1)
        lhs = jnp.where(keep, jnp.tile(rows, (g, 1)), 0.0).astype(BF16)
        acc = acc + _dot(lhs, sel_ref[t])
    o_ref[...] = acc.reshape(g, nch, SSM_WIDTH).astype(o_ref.dtype)


def _s5_unpack_kernel(y_ref, sel_ref, o_ref, lo_scr, hi_scr, *, nch):
    g = SSM_GROUPS
    lane_grp = lax.broadcasted_iota(jnp.int32, (g * nch, SSM_WIDTH), 1) // SSM_GROUP
    row_grp = lax.broadcasted_iota(jnp.int32, (g * nch, SSM_WIDTH), 0) // nch
    keep = lane_grp == row_grp
    yb = y_ref[...].reshape(g * nch, SSM_WIDTH).astype(BF16)
    for t in range(S5_CHUNK):
        z = jnp.where(keep, _dot(yb, sel_ref[t]), 0.0)
        tok = jnp.sum(z.reshape(g, nch, SSM_WIDTH), 0)
        step_rows = pl.ds(t, nch, stride=S5_CHUNK)
        lo_scr[step_rows, :] = tok[:, 0:LANES]
        hi_scr[step_rows, :] = tok[:, LANES:2 * LANES]
    o_ref[...] = jnp.concatenate([lo_scr[...], hi_scr[...]], 1)


def _s5_relayout_call(kern, x, sel, grp, to_groups):
    nch = grp.tm // S5_CHUNK
    tpb = grp.tiles_per_batch
    tok_spec = grp.row_spec(SSM_WIDTH)
    grp_spec = pl.BlockSpec((SSM_GROUPS, None, nch, SSM_WIDTH), lambda i: (0, i // tpb, i % tpb, 0))
    grp_shape = (SSM_GROUPS, grp.batch, grp.seq // S5_CHUNK, SSM_WIDTH)
    return pl.pallas_call(
        functools.partial(kern, nch=nch),
        grid=grp.grid,
        in_specs=[tok_spec if to_groups else grp_spec,
                  _const_spec((S5_CHUNK, SSM_WIDTH, SSM_WIDTH), (0, 0, 0))],
        out_specs=grp_spec if to_groups else tok_spec,
        out_shape=jax.ShapeDtypeStruct(grp_shape, BF16) if to_groups
        else jax.ShapeDtypeStruct((grp.rows, SSM_WIDTH), F32),
        scratch_shapes=[pltpu.VMEM((grp.tm, LANES), F32)] * 2,
        compiler_params=_cparams(("parallel",)),
        name="s5_pack" if to_groups else "s5_unpack",
    )(x, sel)


def _s5_call(u_g, x0, w, l, nb, nc):
    g = SSM_GROUPS
    rows = nb * nc
    wide = S5_CHUNK * SSM_GROUP
    u_g = u_g.reshape(g, rows, wide)
    return pl.pallas_call(
        functools.partial(_s5_kernel, nb=nb, nc=nc),
        grid=(g,),
        in_specs=[pl.BlockSpec((None, rows, wide), lambda i: (i, 0, 0)),
                  pl.BlockSpec((None, None, wide, wide), lambda i: (l, i, 0, 0)),
                  pl.BlockSpec((None, None, wide, wide), lambda i: (l, i, 0, 0)),
                  pl.BlockSpec((None, None, LANES, wide), lambda i: (l, i, 0, 0)),
                  pl.BlockSpec((None, None, 8, LANES), lambda i: (l, i, 0, 0)),
                  pl.BlockSpec((None, None, 1, wide), lambda i: (l, i, 0, 0)),
                  pl.BlockSpec((None, nb, wide), lambda i: (i, 0, 0))],
        out_specs=[pl.BlockSpec((None, rows, wide), lambda i: (i, 0, 0)),
                   pl.BlockSpec((None, nb, LANES), lambda i: (i, 0, 0))],
        out_shape=[jax.ShapeDtypeStruct((g, rows, wide), F32), jax.ShapeDtypeStruct((g, nb, LANES), F32)],
        scratch_shapes=[pltpu.VMEM((rows, LANES), F32)] * 3,
        compiler_params=_cparams(("parallel",)),
        name="s5_chunks",
    )(u_g, w["s5_m"], w["s5_wis"], w["s5_wso"], w["s5_av"], w["s5_d"], x0)


def _s5_step_kernel(u_ref, xr_ref, xi_ref, bb_ref, cc_ref, ar_ref, ai_ref, d_ref, y_ref, or_ref, oi_ref):
    u = u_ref[...]
    n = SSM_GROUPS * SSM_STATE
    bu = _dot(u.astype(BF16), bb_ref[...])
    xr, xi, ar, ai = xr_ref[...], xi_ref[...], ar_ref[...], ai_ref[...]
    nr = ar * xr - ai * xi + bu[:, :n]
    ni = ar * xi + ai * xr + bu[:, n:]
    or_ref[...] = nr
    oi_ref[...] = ni
    y_ref[...] = (_dot(nr.astype(BF16), cc_ref[0:n, :]) + _dot(ni.astype(BF16), cc_ref[n:2 * n, :])
                  + u * d_ref[...])


def _s5_step_call(u, xr, xi, w, l):
    b = u.shape[0]
    n = SSM_GROUPS * SSM_STATE
    full = lambda shape: pl.BlockSpec(shape, lambda i: (0,) * len(shape))
    lsel = lambda shape: pl.BlockSpec((None,) + shape, lambda i: (l,) + (0,) * len(shape))
    return pl.pallas_call(
        _s5_step_kernel,
        grid=(1,),
        in_specs=[full((b, SSM_WIDTH)), full((b, n)), full((b, n)),
                  lsel((SSM_WIDTH, 2 * n)), lsel((2 * n, SSM_WIDTH)), lsel((1, n)), lsel((1, n)),
                  lsel((1, SSM_WIDTH))],
        out_specs=[full((b, SSM_WIDTH)), full((b, n)), full((b, n))],
        out_shape=[jax.ShapeDtypeStruct((b, SSM_WIDTH), F32), jax.ShapeDtypeStruct((b, n), F32),
                   jax.ShapeDtypeStruct((b, n), F32)],
        compiler_params=_cparams(("arbitrary",)),
        name="s5_step",
    )(u, xr, xi, w["s5_bb"], w["s5_cc"], w["s5_ar"], w["s5_ai"], w["s5_dflat"])


def _hgrn_kernel(zh_ref, s0_ref, lbp_ref, e_ref, bd_ref, lcum_ref, lall_ref, o_ref, sf_ref,
                 fpad, kpad, vpad, st_ref, oi_ref, *, tt, n_valid):
    j = pl.program_id(1)

    @pl.when(j == 0)
    def _():
        st_ref[...] = s0_ref[...]

    w = HG_WIDTH
    zf = zh_ref[:, 0:w]
    q = zh_ref[:, w:2 * w]
    v = zh_ref[:, 2 * w:3 * w]
    la, l1, oml = lbp_ref[0:1, :], lbp_ref[1:2, :], lbp_ref[2:3, :]
    lsig = jnp.minimum(zf, 0.0) - jnp.log1p(jnp.exp(-jnp.abs(zf)))
    t2 = l1 + lsig
    lf = jnp.maximum(la, t2) + jnp.log1p(jnp.exp(-jnp.abs(la - t2)))
    k = oml * _sigmoid(-zf)
    row = lax.broadcasted_iota(jnp.int32, (tt, w), 0)
    if n_valid < tt:
        live = row < n_valid
        lf = jnp.where(live, lf, 0.0)
        k = jnp.where(live, k, 0.0)
    f = jnp.exp(lf)

    pad = HG_GROUP
    zero_pad = jnp.zeros((pad, w), F32)
    fpad[0:pad, :] = zero_pad
    kpad[0:pad, :] = zero_pad
    vpad[0:pad, :] = zero_pad
    fpad[pad:pad + tt, :] = f
    kpad[pad:pad + tt, :] = k
    vpad[pad:pad + tt, :] = v

    in_group = row & (HG_GROUP - 1)
    e = e_ref[...]
    acc = _dot((q * k).astype(BF16), e) * v
    decay = jnp.ones((tt, w), F32)
    for d in range(1, HG_GROUP):
        decay = decay * fpad[pad - d + 1:pad - d + 1 + tt, :]
        p = jnp.where(in_group >= d, q * kpad[pad - d:pad - d + tt, :] * decay, 0.0)
        acc = acc + _dot(p.astype(BF16), e) * vpad[pad - d:pad - d + tt, :]

    b = _split_dot(lcum_ref[...], lf, 3)
    bl = _split_dot(lall_ref[...], lf, 3)
    qe = (q * jnp.exp(b)).astype(BF16)
    kx = (k * jnp.exp(bl - b)).astype(BF16)
    ebl = jnp.exp(bl)
    vb = v.astype(BF16)
    bd = bd_ref[...]
    for g in range(tt // HG_GROUP):
        lo = g * HG_GROUP
        st = st_ref[...]
        oi_ref[lo:lo + HG_GROUP, :] = _dot_nt(qe[lo:lo + HG_GROUP], st.astype(BF16))
        upd = _dot_tn(vb[lo:lo + HG_GROUP], kx[lo:lo + HG_GROUP])
        st_ref[...] = st * ebl[lo:lo + 1, :] + upd * bd
    o_ref[...] = acc + oi_ref[...]

    @pl.when(j == pl.num_programs(1) - 1)
    def _():
        sf_ref[...] = st_ref[...]


def _hgrn_call(zh, s0t, w, l, batch, seq_rows, tt, n_valid):
    wd = HG_WIDTH
    nj = seq_rows // tt
    return pl.pallas_call(
        functools.partial(_hgrn_kernel, tt=tt, n_valid=n_valid),
        grid=(batch, nj),
        in_specs=[pl.BlockSpec((tt, 4 * wd), lambda b, j: (b * nj + j, 0)),
                  pl.BlockSpec((None, wd, wd), lambda b, j: (b, 0, 0)),
                  pl.BlockSpec((None, 8, wd), lambda b, j: (l, 0, 0)),
                  pl.BlockSpec((wd, wd), lambda b, j: (0, 0)),
                  pl.BlockSpec((wd, wd), lambda b, j: (0, 0)),
                  pl.BlockSpec((tt, tt), lambda b, j: (0, 0)),
                  pl.BlockSpec((tt, tt), lambda b, j: (0, 0))],
        out_specs=[pl.BlockSpec((tt, wd), lambda b, j: (b * nj + j, 0)),
                   pl.BlockSpec((None, wd, wd), lambda b, j: (b, 0, 0))],
        out_shape=[jax.ShapeDtypeStruct((batch * seq_rows, wd), F32),
                   jax.ShapeDtypeStruct((batch, wd, wd), F32)],
        scratch_shapes=[pltpu.VMEM((HG_GROUP + tt, wd), F32)] * 3
                       + [pltpu.VMEM((wd, wd), F32), pltpu.VMEM((tt, wd), F32)],
        compiler_params=_cparams(("parallel", "arbitrary")),
        name="hgrn",
    )(zh, s0t, w["hg_lbp"], w["hg_e"], w["hg_bd"], w["hg_lcum"][tt], w["hg_lall"][tt])


def _flash_kernel(q_ref, k_ref, vt_ref, o_ref, m_ref, l_ref, acc_ref):
    qi = pl.program_id(1)
    ki = pl.program_id(2)

    @pl.when(ki == 0)
    def _():
        m_ref[...] = jnp.full(m_ref.shape, NEG_MASK, F32)
        l_ref[...] = jnp.zeros(l_ref.shape, F32)
        acc_ref[...] = jnp.zeros(acc_ref.shape, F32)

    def step(diagonal):
        for hd in range(MLA_HEADS):
            sl = slice(hd * HEAD_PAD, (hd + 1) * HEAD_PAD)
            vs = slice(hd * MLA_V_DIM, (hd + 1) * MLA_V_DIM)
            st = _dot_nt(k_ref[:, sl], q_ref[:, sl])
            if diagonal:
                key = lax.broadcasted_iota(jnp.int32, st.shape, 0)
                qry = lax.broadcasted_iota(jnp.int32, st.shape, 1)
                st = jnp.where(key <= qry, st, NEG_MASK)
            m_prev = m_ref[hd:hd + 1, :]
            m_new = jnp.maximum(m_prev, jnp.max(st, 0, keepdims=True))
            alpha = jnp.exp(m_prev - m_new)
            p = jnp.exp(st - m_new)
            l_ref[hd:hd + 1, :] = alpha * l_ref[hd:hd + 1, :] + jnp.sum(p, 0, keepdims=True)
            acc_ref[vs, :] = alpha * acc_ref[vs, :] + _dot(vt_ref[vs, :], p.astype(BF16))
            m_ref[hd:hd + 1, :] = m_new

    @pl.when(ki < qi)
    def _():
        step(False)

    @pl.when(ki == qi)
    def _():
        step(True)
        for hd in range(MLA_HEADS):
            vs = slice(hd * MLA_V_DIM, (hd + 1) * MLA_V_DIM)
            o_ref[vs, :] = acc_ref[vs, :] / l_ref[hd:hd + 1, :]


def _flash_call(qh, kh, vt, batch, seq, tq):
    nq = seq // tq
    v_rows = MLA_HEADS * MLA_V_DIM
    q_spec = pl.BlockSpec((None, tq, MLA_PAD), lambda b, qi, ki: (b, qi, 0))
    k_spec = pl.BlockSpec((None, tq, MLA_PAD), lambda b, qi, ki: (b, jnp.minimum(ki, qi), 0))
    vt_spec = pl.BlockSpec((None, v_rows, tq), lambda b, qi, ki: (b, 0, jnp.minimum(ki, qi)))
    shp = (batch, seq, MLA_PAD)
    return pl.pallas_call(
        _flash_kernel,
        grid=(batch, nq, nq),
        in_specs=[q_spec, k_spec, vt_spec],
        out_specs=pl.BlockSpec((None, v_rows, tq), lambda b, qi, ki: (b, 0, qi)),
        out_shape=jax.ShapeDtypeStruct((batch, v_rows, seq), F32),
        scratch_shapes=[pltpu.VMEM((MLA_HEADS, tq), F32), pltpu.VMEM((MLA_HEADS, tq), F32),
                        pltpu.VMEM((v_rows, tq), F32)],
        compiler_params=_cparams(("parallel", "parallel", "arbitrary")),
        name="flash",
    )(qh.reshape(shp), kh.reshape(shp), vt)


PAGES_PER_STEP = 16


def _decode_kernel(pt_ref, qh_ref, ckvn_ref, kpen_ref, wuk_ref, wuv_ref, sel_ref, hm_ref, ckv_hbm, kpe_hbm,
                   o_ref, ckv_buf, kpe_buf, sem_c, sem_k, *, layer, npg, ng):
    b = pl.program_id(0)
    hm = hm_ref[...]

    def copies(seq_i, grp_i, slot):
        out = []
        for i in range(npg):
            page = pt_ref[seq_i, grp_i * npg + i]
            out.append(pltpu.make_async_copy(ckv_hbm.at[layer, page], ckv_buf.at[slot, i], sem_c.at[slot]))
            out.append(pltpu.make_async_copy(kpe_hbm.at[layer, page], kpe_buf.at[slot, i], sem_k.at[slot]))
        return out

    @pl.when(b == 0)
    def _():
        for c in copies(0, 0, 0):
            c.start()

    qbd = jnp.where(hm > 0, jnp.broadcast_to(qh_ref[...], hm.shape), jnp.zeros_like(hm)).astype(BF16)
    qa = _dot_nt(qbd, wuk_ref[...]).astype(BF16)
    qp = _dot(qbd, sel_ref[...]).astype(BF16)
    m = jnp.full((MLA_HEADS, 1), NEG_MASK, F32)
    l = jnp.zeros((MLA_HEADS, 1), F32)
    acc = jnp.zeros((MLA_HEADS, MLA_KV_LORA), F32)
    for j in range(ng):
        slot = lax.rem(b * ng + j, 2)
        if j + 1 < ng:
            for c in copies(b, j + 1, 1 - slot):
                c.start()
        else:
            @pl.when(b + 1 < pl.num_programs(0))
            def _():
                for c in copies(b + 1, 0, 1 - slot):
                    c.start()
        for c in copies(b, j, slot):
            c.wait()
        pages = [ckv_buf[slot, i].astype(BF16) for i in range(npg)]
        s = jnp.concatenate(
            [_dot_nt(qa, pg) + _dot(qp, kpe_buf[slot, i].astype(BF16)) for i, pg in enumerate(pages)], axis=1)
        m_new = jnp.maximum(m, jnp.max(s, -1, keepdims=True))
        alpha = jnp.exp(m - m_new)
        p = jnp.exp(s - m_new)
        l = alpha * l + jnp.sum(p, -1, keepdims=True)
        pb = p.astype(BF16)
        acc = alpha * acc
        for i, pg in enumerate(pages):
            acc = acc + _dot(pb[:, i * PAGE_SIZE:(i + 1) * PAGE_SIZE], pg)
        m = m_new

    ckvn = ckvn_ref[...]
    s_new = (jnp.sum(qa.astype(F32) * ckvn.astype(BF16).astype(F32), -1, keepdims=True)
             + jnp.sum(qp.astype(F32) * kpen_ref[...].astype(BF16).astype(F32), -1, keepdims=True))
    m_fin = jnp.maximum(m, s_new)
    a = jnp.exp(m - m_fin)
    p_new = jnp.exp(s_new - m_fin)
    l_fin = a * l + p_new
    lat = a * acc + p_new.astype(BF16).astype(F32) * ckvn.astype(BF16).astype(F32)
    lat = lat / l_fin
    o = _dot(lat.astype(BF16), wuv_ref[...])
    o_ref[...] = jnp.sum(jnp.where(hm > 0, o, 0.0), 0, keepdims=True)


def _decode_call(page_table, qh, ckv_new, kpe_new, cache_ckv, cache_kpe, w, l, sel, hm):
    batch, n_pages = page_table.shape
    npg = min(PAGES_PER_STEP, n_pages)
    ng = n_pages // npg
    row3 = lambda width: pl.BlockSpec((None, 1, width), lambda b, pt: (b, 0, 0))
    wsel = lambda shape: pl.BlockSpec((None,) + shape, lambda b, pt: (l,) + (0,) * len(shape))
    full = lambda shape: pl.BlockSpec(shape, lambda b, pt: (0,) * len(shape))
    hbm = pl.BlockSpec(memory_space=pl.ANY)
    grid_spec = pltpu.PrefetchScalarGridSpec(
        num_scalar_prefetch=1,
        grid=(batch,),
        in_specs=[row3(MLA_PAD), row3(MLA_KV_LORA), row3(MLA_ROPE_DIM),
                  wsel((MLA_KV_LORA, MLA_PAD)), wsel((MLA_KV_LORA, MLA_PAD)),
                  full((MLA_PAD, MLA_ROPE_DIM)), full((MLA_HEADS, MLA_PAD)), hbm, hbm],
        out_specs=row3(MLA_PAD),
        scratch_shapes=[pltpu.VMEM((2, npg, PAGE_SIZE, MLA_KV_LORA), cache_ckv.dtype),
                        pltpu.VMEM((2, npg, MLA_ROPE_DIM, PAGE_SIZE), cache_kpe.dtype),
                        pltpu.SemaphoreType.DMA((2,)), pltpu.SemaphoreType.DMA((2,))],
    )
    out = pl.pallas_call(
        functools.partial(_decode_kernel, layer=l, npg=npg, ng=ng),
        grid_spec=grid_spec,
        out_shape=jax.ShapeDtypeStruct((batch, 1, MLA_PAD), F32),
        compiler_params=_cparams(("arbitrary",)),
        name="decode",
    )(page_table, qh.reshape(batch, 1, MLA_PAD), ckv_new.reshape(batch, 1, MLA_KV_LORA),
      kpe_new.reshape(batch, 1, MLA_ROPE_DIM), w["w_uk"], w["w_uv"], sel, hm, cache_ckv, cache_kpe)
    return out.reshape(batch, MLA_PAD)


def _mix_out_kernel(x_ref, gt_ref, y_ref, zg_ref, oh_ref, om_ref, wglu_ref, nssm_ref, nhg_ref, nmla_ref,
                    e64_ref, woa_ref, wob_ref, lg_ref, lb_ref, o_ref, cat_ref):
    w = SSM_WIDTH
    gab = _dot(_gelu_tanh(y_ref[...]).astype(BF16), wglu_ref[...])
    t = gab[:, :w] * _sigmoid(gab[:, w:])
    cat_ref[:, 0:w] = (t * lax.rsqrt(jnp.mean(t * t, -1, keepdims=True) + 1e-6) * nssm_ref[...]).astype(BF16)

    oh = oh_ref[...]
    msq = _split_dot_r(oh * oh, e64_ref[...], 2)
    cat_ref[:, w:2 * w] = (oh * lax.rsqrt(msq + 1e-6) * nhg_ref[...] * _silu(zg_ref[...])).astype(BF16)

    om = om_ref[...]
    tm = om.shape[1]
    nm = nmla_ref[...]
    nm = nm[:, :tm] if tm <= LANES else jnp.tile(nm, (1, tm // LANES))
    ms = jnp.mean(om * om, 0, keepdims=True)
    o_mla = (om * lax.rsqrt(ms + 1e-6) * nm).astype(BF16)

    mix = _dot(cat_ref[...], woa_ref[...]) + _dot_tn(o_mla, wob_ref[...])
    y = ALPHA * x_ref[...] + (1.0 + gt_ref[...]) * mix
    o_ref[...] = _layer_norm(y, lg_ref[...], lb_ref[...])


def _mix_out_call(grp, x, l, y_ssm, zh, o_h, o_m, w, ln_g, ln_b):
    tm = grp.tm
    v_rows = MLA_HEADS * MLA_V_DIM
    return pl.pallas_call(
        _mix_out_kernel,
        grid=grp.grid,
        in_specs=[grp.row_spec(D_MODEL), grp.mod_spec(l, 5),
                  grp.row_spec(SSM_WIDTH), grp.row_spec(HG_WIDTH, col=3), grp.row_spec(HG_WIDTH),
                  grp.tcol_spec(v_rows),
                  _const_spec((None, SSM_WIDTH, 2 * SSM_WIDTH), (l, 0, 0)),
                  _const_spec((None, 1, SSM_WIDTH), (l, 0, 0)),
                  _const_spec((None, 1, HG_WIDTH), (l, 0, 0)),
                  _const_spec((None, v_rows, LANES), (l, 0, 0)),
                  _const_spec((HG_WIDTH, HG_WIDTH), (0, 0)),
                  _const_spec((None, SSM_WIDTH + HG_WIDTH, D_MODEL), (l, 0, 0)),
                  _const_spec((None, v_rows, D_MODEL), (l, 0, 0)),
                  _const_spec((None, None, 1, D_MODEL), (l, 1, 0, 0)),
                  _const_spec((None, None, 1, D_MODEL), (l, 1, 0, 0))],
        out_specs=grp.row_spec(D_MODEL),
        out_shape=jax.ShapeDtypeStruct((grp.rows, D_MODEL), F32),
        scratch_shapes=[pltpu.VMEM((tm, SSM_WIDTH + HG_WIDTH), BF16)],
        compiler_params=_cparams(("parallel",)),
        name="mix_out",
    )(x, grp.mod, y_ssm, zh, o_h, o_m, w["w_glu"], w["n_ssm"], w["n_hg"], w["n_mla"], w["hg_e64"],
      w["w_out_a"], w["w_out_b"], ln_g, ln_b)


def _pad_heads(wt, per_head):
    lead = wt.shape[:-1]
    wt = wt.reshape(lead + (MLA_HEADS, per_head))
    wt = jnp.pad(wt, [(0, 0)] * len(lead) + [(0, 0), (0, HEAD_PAD - per_head)])
    return wt.reshape(lead + (MLA_PAD,))


def _s5_tables(lam_re, lam_im, b_re, b_im, c_re, c_im, d, log_step):
    hp = lax.Precision.HIGHEST
    depth, g, p = lam_re.shape
    h, t = SSM_GROUP, S5_CHUNK
    step = jnp.exp(log_step)[..., None]
    mag = jnp.exp(lam_re * step)
    ab_re, ab_im = mag * jnp.cos(lam_im * step), mag * jnp.sin(lam_im * step)
    den = lam_re * lam_re + lam_im * lam_im
    nr = ab_re - 1.0
    coef_re = (nr * lam_re + ab_im * lam_im) / den
    coef_im = (ab_im * lam_re - nr * lam_im) / den
    bb_re = coef_re[..., None] * b_re - coef_im[..., None] * b_im
    bb_im = coef_re[..., None] * b_im + coef_im[..., None] * b_re
    n = jnp.arange(t + 1, dtype=F32)[:, None, None, None]
    pmag = jnp.exp(n * lam_re * step)
    pw_re, pw_im = pmag * jnp.cos(n * lam_im * step), pmag * jnp.sin(n * lam_im * step)
    pb_re = pw_re[:t, ..., None] * bb_re - pw_im[:t, ..., None] * bb_im
    pb_im = pw_re[:t, ..., None] * bb_im + pw_im[:t, ..., None] * bb_re
    kern = (jnp.einsum('lgop,nlgpi->nlgoi', c_re, pb_re, precision=hp)
            - jnp.einsum('lgop,nlgpi->nlgoi', c_im, pb_im, precision=hp))
    kern = jnp.concatenate([kern, jnp.zeros_like(kern[:1])], 0)
    s_idx, t_idx = np.arange(t)[:, None], np.arange(t)[None, :]
    lag = np.where(t_idx >= s_idx, t_idx - s_idx, t)
    m = kern[lag]
    m = jnp.transpose(m, (2, 3, 0, 5, 1, 4)).reshape(depth, g, t * h, t * h)
    wr = jnp.transpose(pb_re[::-1], (1, 2, 0, 4, 3)).reshape(depth, g, t * h, p)
    wi = jnp.transpose(pb_im[::-1], (1, 2, 0, 4, 3)).reshape(depth, g, t * h, p)
    wis = jnp.concatenate([wr, wi, wi, wr], -1)
    cp_re = c_re[None] * pw_re[1:, :, :, None, :] - c_im[None] * pw_im[1:, :, :, None, :]
    cp_im = c_re[None] * pw_im[1:, :, :, None, :] + c_im[None] * pw_re[1:, :, :, None, :]
    so_re = jnp.transpose(cp_re, (1, 2, 4, 0, 3)).reshape(depth, g, p, t * h)
    so_im = jnp.transpose(-cp_im, (1, 2, 4, 0, 3)).reshape(depth, g, p, t * h)
    wso = jnp.concatenate([so_re, so_im], 2)
    a_re, a_im = pw_re[t], pw_im[t]
    av = jnp.stack([jnp.concatenate([a_re, a_re], -1), jnp.concatenate([-a_im, a_im], -1),
                    jnp.concatenate([a_im, -a_im], -1)], 2)
    av = jnp.pad(av, ((0, 0), (0, 0), (0, 5), (0, 0)))
    dg = d.reshape(depth, g, 1, h)
    d_t = jnp.tile(dg, (1, 1, 1, t))
    eye = jnp.eye(g, dtype=F32)
    bbr = jnp.einsum('lgph,gk->lghkp', bb_re, eye).reshape(depth, g * h, g * p)
    bbi = jnp.einsum('lgph,gk->lghkp', bb_im, eye).reshape(depth, g * h, g * p)
    ccr = jnp.einsum('lghp,gk->lgpkh', c_re, eye).reshape(depth, g * p, g * h)
    cci = jnp.einsum('lghp,gk->lgpkh', -c_im, eye).reshape(depth, g * p, g * h)
    return dict(s5_m=m.astype(BF16), s5_wis=wis.astype(BF16), s5_wso=wso.astype(BF16), s5_av=av, s5_d=d_t,
                s5_bb=jnp.concatenate([bbr, bbi], -1).astype(BF16),
                s5_cc=jnp.concatenate([ccr, cci], 1).astype(BF16),
                s5_ar=ab_re.reshape(depth, 1, g * p), s5_ai=ab_im.reshape(depth, 1, g * p),
                s5_dflat=d.reshape(depth, 1, g * h))


def _rope_tables(pos):
    half = MLA_ROPE_DIM // 2
    inv = 1.0 / (ROPE_THETA ** (jnp.arange(0, MLA_ROPE_DIM, 2, dtype=F32) / MLA_ROPE_DIM))
    ang = pos.astype(F32)[:, None] * inv[None, :]
    cos, sin = jnp.cos(ang), jnp.sin(ang)
    n = pos.shape[0]
    one, zero = jnp.ones((n, ROPE_LANE0), F32), jnp.zeros((n, ROPE_LANE0), F32)
    z16, z32 = jnp.zeros((n, half), F32), jnp.zeros((n, LANES - ROPE_LANE0 - MLA_ROPE_DIM), F32)
    rc = jnp.concatenate([one, cos, cos, z32], 1)
    rs1 = jnp.concatenate([zero, -sin, z16, z32], 1)
    rs2 = jnp.concatenate([zero, z16, sin, z32], 1)
    return rc, rs1, rs2


def _group_patterns(tt):
    r = np.arange(tt)
    same = (r[:, None] // HG_GROUP) == (r[None, :] // HG_GROUP)
    lcum = same & (r[None, :] <= r[:, None])
    return jnp.asarray(lcum, BF16), jnp.asarray(same, BF16)


def _prepare(p):
    depth = p["w_in"].shape[0]
    w = {}
    w["ffn_w_gu"] = p["ffn_w_gu"].astype(BF16)
    w["ffn_w_down"] = p["ffn_w_down"].astype(BF16)
    w_in = p["w_in"]
    n_main = Z_CKV[1]
    zpad = lambda n: jnp.zeros((depth, D_MODEL, n), w_in.dtype)
    w["w_in"] = jnp.concatenate(
        [w_in[:, :, :n_main], zpad(ROPE_LANE0), w_in[:, :, n_main:], zpad(LANES - ROPE_LANE0 - MLA_ROPE_DIM)],
        -1).astype(BF16)
    w["w_uq"] = _pad_heads(p["mla_w_uq"], MLA_NOPE_DIM + MLA_ROPE_DIM).astype(BF16)
    w["w_uk"] = _pad_heads(p["mla_w_uk"], MLA_NOPE_DIM).astype(BF16)
    w["w_uv"] = _pad_heads(p["mla_w_uv"], MLA_V_DIM).astype(BF16)
    w["w_uv_t"] = jnp.swapaxes(p["mla_w_uv"], 1, 2).astype(BF16)
    w["q_norm"] = p["mla_q_norm"].reshape(depth, 1, MLA_Q_LORA)
    w["kv_norm"] = p["mla_kv_norm"].reshape(depth, 1, MLA_KV_LORA)
    w["w_glu"] = p["ssm_w_glu"].astype(BF16)
    w["n_ssm"] = p["norm_ssm"].reshape(depth, 1, SSM_WIDTH)
    w["n_hg"] = p["norm_hgrn"].reshape(depth, 1, HG_WIDTH)
    w["n_mla"] = jnp.broadcast_to(p["norm_mla"][:, :, None], (depth, MLA_HEADS * MLA_V_DIM, LANES))
    w_out = p["w_out"]
    split = SSM_WIDTH + HG_WIDTH
    w["w_out_a"] = w_out[:, :split].astype(BF16)
    w["w_out_b"] = w_out[:, split:].astype(BF16)
    w.update(_s5_tables(p["ssm_lambda_re"].astype(F32), p["ssm_lambda_im"].astype(F32),
                        p["ssm_b_re"].astype(F32), p["ssm_b_im"].astype(F32),
                        p["ssm_c_re"].astype(F32), p["ssm_c_im"].astype(F32),
                        p["ssm_d"].astype(F32), p["ssm_log_step"].astype(F32)))
    lane = np.arange(SSM_WIDTH)
    step_of, chan_of = lane // SSM_GROUP, lane % SSM_GROUP
    t_idx = np.arange(S5_CHUNK)[:, None, None]
    fwd = (step_of[None, None, :] == t_idx) & (chan_of[None, None, :] == chan_of[None, :, None])
    w["s5_sel_fwd"] = jnp.asarray(fwd, BF16)
    w["s5_sel_bwd"] = jnp.asarray(np.swapaxes(fwd, 1, 2), BF16)
    sm = jax.nn.softmax(p["hgrn_lb_logits"].astype(F32), axis=0)
    lb = jnp.clip(jnp.cumsum(sm, axis=0) - sm[0:1], 0.0, 1.0 - 1e-6)
    rows = jnp.stack([jnp.log(jnp.maximum(lb, LB_FLOOR)), jnp.log1p(-lb), 1.0 - lb], 1)
    w["hg_lbp"] = jnp.pad(rows, ((0, 0), (0, 5), (0, 0)))
    head = np.arange(HG_WIDTH) // HG_DK
    same_head = head[:, None] == head[None, :]
    w["hg_e"] = jnp.asarray(same_head, BF16)
    w["hg_bd"] = jnp.asarray(same_head, F32)
    w["hg_e64"] = jnp.asarray(same_head / HG_DK, BF16)
    return w


def _hg_state_in(s):
    b = s.shape[0]
    eye = jnp.eye(HG_HEADS, dtype=s.dtype)
    return jnp.einsum('bhkv,hg->bhvgk', s, eye).reshape(b, HG_WIDTH, HG_WIDTH)


def _hg_state_out(st):
    b = st.shape[0]
    st = st.reshape(b, HG_HEADS, HG_DK, HG_HEADS, HG_DK)
    diag = jnp.stack([st[:, h, :, h, :] for h in range(HG_HEADS)], 1)
    return jnp.transpose(diag, (0, 1, 3, 2))


def _run_prompt(x, grp, w, p, rope):
    batch, seq = grp.batch, grp.seq
    depth = p["w_in"].shape[0]
    nc = seq // S5_CHUNK
    tt = min(256, seq)
    tq = min(512, seq)
    ln_g = p["ln_g"].reshape(depth, 3, 1, D_MODEL)
    ln_b = p["ln_b"].reshape(depth, 3, 1, D_MODEL)
    x0 = jnp.zeros((SSM_GROUPS, batch, 2 * LANES), F32)
    s0 = jnp.zeros((batch, HG_WIDTH, HG_WIDTH), F32)
    ckv_l, kpe_l, sre_l, sim_l, hg_l = [], [], [], [], []
    for l in range(depth):
        x = _ffn_call(grp, x, l, 0, 0, w["ffn_w_gu"], w["ffn_w_down"], ln_g, ln_b)
        u, zh, qh, kh, vt, ckv, kpe = _mix_in_call(grp, x, l, w, rope)
        u_g = _s5_relayout_call(_s5_pack_kernel, u, w["s5_sel_fwd"], grp, True)
        y_g, xf = _s5_call(u_g, x0, w, l, batch, nc)
        y = _s5_relayout_call(_s5_unpack_kernel, y_g.reshape(u_g.shape), w["s5_sel_bwd"], grp, False)
        o_h, st = _hgrn_call(zh, s0, w, l, batch, seq, tt, tt)
        o_m = _flash_call(qh, kh, vt, batch, seq, tq)
        x = _mix_out_call(grp, x, l, y, zh, o_h, o_m, w, ln_g, ln_b)
        x = _ffn_call(grp, x, l, 1, 2, w["ffn_w_gu"], w["ffn_w_down"], ln_g, ln_b)
        ckv_l.append(ckv.reshape(batch, seq, MLA_KV_LORA))
        kpe_l.append(kpe.reshape(batch, seq, MLA_ROPE_DIM))
        xf = jnp.transpose(xf, (1, 0, 2))
        sre_l.append(xf[..., :SSM_STATE])
        sim_l.append(xf[..., SSM_STATE:])
        hg_l.append(_hg_state_out(st))
    return (x.reshape(batch, seq, D_MODEL), jnp.stack(ckv_l), jnp.stack(kpe_l), jnp.stack(sre_l),
            jnp.stack(sim_l), jnp.stack(hg_l))


def _run_sample(x, grp, w, p, rope, cache_ckv, cache_kpe, page_table, ssm_re, ssm_im, hg_state):
    batch = grp.batch
    depth = p["w_in"].shape[0]
    ln_g = p["ln_g"].reshape(depth, 3, 1, D_MODEL)
    ln_b = p["ln_b"].reshape(depth, 3, 1, D_MODEL)
    n = SSM_GROUPS * SSM_STATE
    lane = np.arange(MLA_PAD)
    sel = np.zeros((MLA_PAD, MLA_ROPE_DIM), np.float32)
    for i in range(MLA_ROPE_DIM):
        sel[(lane % HEAD_PAD) == ROPE_LANE0 + i, i] = 1.0
    sel = jnp.asarray(sel, BF16)
    hm = jnp.asarray((lane[None, :] // HEAD_PAD) == np.arange(MLA_HEADS)[:, None], F32)
    cache_kpe = jnp.swapaxes(cache_kpe, 2, 3)
    ckv_l, kpe_l, sre_l, sim_l, hg_l = [], [], [], [], []
    for l in range(depth):
        x = _ffn_call(grp, x, l, 0, 0, w["ffn_w_gu"], w["ffn_w_down"], ln_g, ln_b)
        u, zh, qh, kh, _, ckv, kpe = _mix_in_call(grp, x, l, w, rope)
        y, nr, ni = _s5_step_call(u, ssm_re[l].reshape(batch, n), ssm_im[l].reshape(batch, n), w, l)
        zh_pad = jnp.pad(zh[:, None, :], ((0, 0), (0, HG_GROUP - 1), (0, 0))).reshape(batch * HG_GROUP, -1)
        o_hp, st = _hgrn_call(zh_pad, _hg_state_in(hg_state[l]), w, l, batch, HG_GROUP, HG_GROUP, 1)
        o_h = o_hp.reshape(batch, HG_GROUP, HG_WIDTH)[:, 0]
        o_m = _decode_call(page_table, qh, ckv, kpe, cache_ckv, cache_kpe, w, l, sel, hm)
        o_m = o_m.reshape(batch, MLA_HEADS, HEAD_PAD)[:, :, :MLA_V_DIM].reshape(batch, -1).T[None]
        x = _mix_out_call(grp, x, l, y, zh, o_h, o_m, w, ln_g, ln_b)
        x = _ffn_call(grp, x, l, 1, 2, w["ffn_w_gu"], w["ffn_w_down"], ln_g, ln_b)
        ckv_l.append(ckv.reshape(batch, 1, MLA_KV_LORA))
        kpe_l.append(kpe.reshape(batch, 1, MLA_ROPE_DIM))
        sre_l.append(nr.reshape(batch, SSM_GROUPS, SSM_STATE))
        sim_l.append(ni.reshape(batch, SSM_GROUPS, SSM_STATE))
        hg_l.append(_hg_state_out(st))
    return (x.reshape(batch, 1, D_MODEL), jnp.stack(ckv_l), jnp.stack(kpe_l), jnp.stack(sre_l),
            jnp.stack(sim_l), jnp.stack(hg_l))


def kernel(x_prompt, x_sample, cache_kv_latent, cache_k_rope, state_ssm_re, state_ssm_im, state_hgrn, page_table, c_prompt, c_sample, w_ada, b_ada, ln_g, ln_b, ffn_w_gu, ffn_w_down, w_in, w_out, ssm_lambda_re, ssm_lambda_im, ssm_b_re, ssm_b_im, ssm_c_re, ssm_c_im, ssm_d, ssm_log_step, ssm_w_glu, norm_ssm, hgrn_lb_logits, norm_hgrn, mla_q_norm, mla_w_uq, mla_kv_norm, mla_w_uk, mla_w_uv, norm_mla):
    p = dict(ln_g=ln_g, ln_b=ln_b, ffn_w_gu=ffn_w_gu, ffn_w_down=ffn_w_down, w_in=w_in, w_out=w_out,
             ssm_lambda_re=ssm_lambda_re, ssm_lambda_im=ssm_lambda_im, ssm_b_re=ssm_b_re, ssm_b_im=ssm_b_im,
             ssm_c_re=ssm_c_re, ssm_c_im=ssm_c_im, ssm_d=ssm_d, ssm_log_step=ssm_log_step, ssm_w_glu=ssm_w_glu,
             norm_ssm=norm_ssm, hgrn_lb_logits=hgrn_lb_logits, norm_hgrn=norm_hgrn, mla_q_norm=mla_q_norm,
             mla_w_uq=mla_w_uq, mla_kv_norm=mla_kv_norm, mla_w_uk=mla_w_uk, mla_w_uv=mla_w_uv, norm_mla=norm_mla)
    depth = w_in.shape[0]
    assert depth == DEPTH_ and x_sample.shape[1] == 1
    bp, seq, _ = x_prompt.shape
    bs = x_sample.shape[0]
    past_len = page_table.shape[1] * PAGE_SIZE
    w = _prepare(p)
    w["hg_lcum"], w["hg_lall"] = {}, {}
    for tt in {min(256, seq), HG_GROUP}:
        w["hg_lcum"][tt], w["hg_lall"][tt] = _group_patterns(tt)

    mod = _ada_call(jnp.concatenate([c_prompt, c_sample], 0), w_ada, b_ada)
    mod = mod.reshape(depth, bp + bs, 9, D_MODEL)
    grp_p = _Group(mod[:, :bp], bp, seq, min(512, seq))
    grp_s = _Group(mod[:, bp:], bs, 1, bs)

    rope_p = _rope_tables(jnp.arange(seq))
    rope_s = _rope_tables(past_len + jnp.arange(1))
    y_p, ckv_p, kpe_p, sre_p, sim_p, hg_p = _run_prompt(x_prompt.reshape(bp * seq, D_MODEL), grp_p, w, p, rope_p)
    y_s, ckv_s, kpe_s, sre_s, sim_s, hg_s = _run_sample(
        x_sample.reshape(bs, D_MODEL), grp_s, w, p, rope_s, cache_kv_latent, cache_k_rope, page_table,
        state_ssm_re, state_ssm_im, state_hgrn)
    return (y_p, y_s, ckv_p, ckv_s, kpe_p, kpe_s, sre_p, sre_s, sim_p, sim_s, hg_p, hg_s)
```

```python
import functools
import math

import numpy as np
import jax
import jax.numpy as jnp
from jax import lax
from jax.experimental import pallas as pl
from jax.experimental.pallas import tpu as pltpu

F32 = jnp.float32
BF16 = jnp.bfloat16

D_MODEL = 1024
D_FF = 2816
SSM_WIDTH = 256
SSM_GROUP = 16
SSM_GROUPS = 16
SSM_STATE = 64
HG_WIDTH = 256
HG_HEADS = 4
HG_DK = 64
LB_FLOOR = 1e-30
MLA_HEADS = 8
MLA_V_DIM = 64
MLA_NOPE_DIM = 64
MLA_ROPE_DIM = 32
MLA_Q_LORA = 384
MLA_KV_LORA = 256
MLA_SCALE = (MLA_NOPE_DIM + MLA_ROPE_DIM) ** -0.5
ROPE_THETA = 10000.0
NEG_MASK = -1e30
PAGE_SIZE = 128

LANES = 128
HEAD_PAD = LANES
MLA_PAD = MLA_HEADS * HEAD_PAD
N_IN_PAD = 2048
ROPE_LANE0 = MLA_NOPE_DIM
S5_CHUNK = 16
HG_GROUP = 16
VMEM_LIMIT = 56 * 1024 * 1024
DEPTH_ = 4
ALPHA = (2 * DEPTH_) ** 0.25


def _cparams(sem):
    return pltpu.CompilerParams(dimension_semantics=sem, vmem_limit_bytes=VMEM_LIMIT)


def _dot(a, b):
    return jnp.dot(a, b, preferred_element_type=F32)


def _dot_nt(a, b):
    return lax.dot_general(a, b, (((1,), (1,)), ((), ())), preferred_element_type=F32)


def _dot_tn(a, b):
    return lax.dot_general(a, b, (((0,), (0,)), ((), ())), preferred_element_type=F32)


def _split_dot(w, x, terms):
    acc = None
    r = x
    for _ in range(terms):
        p = r.astype(BF16)
        d = _dot(w, p)
        acc = d if acc is None else acc + d
        r = r - p.astype(F32)
    return acc


def _split_dot_r(x, w, terms):
    acc = None
    r = x
    for _ in range(terms):
        p = r.astype(BF16)
        d = _dot(p, w)
        acc = d if acc is None else acc + d
        r = r - p.astype(F32)
    return acc


def _layer_norm(y, g, b):
    mu = jnp.mean(y, -1, keepdims=True)
    d = y - mu
    var = jnp.mean(d * d, -1, keepdims=True)
    return d * lax.rsqrt(var + 1e-5) * g + b


def _sigmoid(x):
    return 1.0 / (1.0 + jnp.exp(-x))


def _silu(x):
    return x * _sigmoid(x)


def _gelu_tanh(x):
    c = math.sqrt(2.0 / math.pi)
    return 0.5 * x * (1.0 + jnp.tanh(c * (x + 0.044715 * (x * x * x))))


def _ada_kernel(c_ref, w_ref, b_ref, o_ref):
    c = c_ref[...]
    s = _silu(c).astype(BF16)
    o_ref[...] = _dot(s, w_ref[...].astype(BF16)) + b_ref[...]


def _ada_call(c_all, w_ada, b_ada):
    rows = c_all.shape[0]
    depth, d, n = w_ada.shape
    tn = 1152
    return pl.pallas_call(
        _ada_kernel,
        grid=(depth, n // tn),
        in_specs=[pl.BlockSpec((rows, d), lambda l, j: (0, 0)),
                  pl.BlockSpec((None, d, tn), lambda l, j: (l, 0, j)),
                  pl.BlockSpec((None, 1, tn), lambda l, j: (l, 0, j))],
        out_specs=pl.BlockSpec((None, rows, tn), lambda l, j: (l, 0, j)),
        out_shape=jax.ShapeDtypeStruct((depth, rows, n), F32),
        compiler_params=_cparams(("parallel", "parallel")),
        name="ada",
    )(c_all, w_ada, b_ada.reshape(depth, 1, n))


class _Group:
    def __init__(self, mod, batch, seq, tm):
        self.batch, self.seq, self.tm = batch, seq, tm
        self.rows = batch * seq
        self.per_row = seq == 1
        depth = mod.shape[0]
        if self.per_row:
            self.mod = jnp.transpose(mod, (0, 2, 1, 3))
        else:
            self.mod = mod.reshape(depth, batch, 9, 1, D_MODEL)
            self.tiles_per_batch = seq // tm

    def mod_spec(self, l, j):
        if self.per_row:
            return pl.BlockSpec((None, None, self.tm, D_MODEL), lambda i: (l, j, i, 0))
        tpb = self.tiles_per_batch
        return pl.BlockSpec((None, None, None, 1, D_MODEL), lambda i: (l, i // tpb, j, 0, 0))

    def pos_spec(self, width):
        if self.per_row:
            return pl.BlockSpec((1, width), lambda i: (0, 0))
        tpb = self.tiles_per_batch
        return pl.BlockSpec((self.tm, width), lambda i: (i % tpb, 0))

    def row_spec(self, width, col=0):
        return pl.BlockSpec((self.tm, width), lambda i: (i, col))

    def tcol_shape(self, height):
        return (1, height, self.batch) if self.per_row else (self.batch, height, self.seq)

    def tcol_spec(self, height):
        if self.per_row:
            return pl.BlockSpec((None, height, self.tm), lambda i: (0, 0, 0))
        tpb = self.tiles_per_batch
        return pl.BlockSpec((None, height, self.tm), lambda i: (i // tpb, 0, i % tpb))

    @property
    def grid(self):
        return (self.rows // self.tm,)


def _const_spec(shape, index):
    return pl.BlockSpec(shape, lambda i: index)


FF_CHUNK = 256


def _ffn_kernel(x_ref, sh_ref, sc_ref, gt_ref, wgu_ref, wd_ref, lg_ref, lb_ref, o_ref, a_ref):
    x = x_ref[...]
    h = (x * (1.0 + sc_ref[...]) + sh_ref[...]).astype(BF16)
    for c in range(D_FF // FF_CHUNK):
        lo = c * FF_CHUNK
        g = _dot(h, wgu_ref[:, lo:lo + FF_CHUNK])
        u = _dot(h, wgu_ref[:, D_FF + lo:D_FF + lo + FF_CHUNK])
        a_ref[:, lo:lo + FF_CHUNK] = (_silu(g) * u).astype(BF16)
    f = _dot(a_ref[...], wd_ref[...])
    y = ALPHA * x + 0.5 * (1.0 + gt_ref[...]) * f
    o_ref[...] = _layer_norm(y, lg_ref[...], lb_ref[...])


def _ffn_call(grp, x, l, which, mod_idx, wgu, wd, ln_g, ln_b):
    tm = grp.tm
    return pl.pallas_call(
        _ffn_kernel,
        grid=grp.grid,
        in_specs=[grp.row_spec(D_MODEL),
                  grp.mod_spec(l, 3 * mod_idx), grp.mod_spec(l, 3 * mod_idx + 1), grp.mod_spec(l, 3 * mod_idx + 2),
                  _const_spec((None, None, D_MODEL, 2 * D_FF), (l, which, 0, 0)),
                  _const_spec((None, None, D_FF, D_MODEL), (l, which, 0, 0)),
                  _const_spec((None, None, 1, D_MODEL), (l, mod_idx, 0, 0)),
                  _const_spec((None, None, 1, D_MODEL), (l, mod_idx, 0, 0))],
        out_specs=grp.row_spec(D_MODEL),
        out_shape=jax.ShapeDtypeStruct((grp.rows, D_MODEL), F32),
        scratch_shapes=[pltpu.VMEM((tm, D_FF), BF16)],
        compiler_params=_cparams(("parallel",)),
        name="ffn",
    )(x, grp.mod, grp.mod, grp.mod, wgu, wd, ln_g, ln_b)


Z_U = (0, 256)
Z_H = (256, 1280)
Z_CQ = (1280, 1664)
Z_CKV = (1664, 1920)
Z_KPE = (1920, 2048)


def _rope_group(t, c, s1, s2):
    return t * c + pltpu.roll(t, LANES - MLA_ROPE_DIM // 2, 1) * s1 + pltpu.roll(t, MLA_ROPE_DIM // 2, 1) * s2


def _mix_in_kernel(x_ref, sh_ref, sc_ref, win_ref, qn_ref, wuq_ref, kvn_ref, wuk_ref, wuv_ref,
                   rc_ref, rs1_ref, rs2_ref,
                   u_ref, zh_ref, qh_ref, kh_ref, vt_ref, ckv_ref, kpe_ref):
    x = x_ref[...]
    h = (x * (1.0 + sc_ref[...]) + sh_ref[...]).astype(BF16)
    u_ref[...] = _dot(h, win_ref[:, Z_U[0]:Z_U[1]])
    zh_ref[...] = _dot(h, win_ref[:, Z_H[0]:Z_H[1]])
    rc, rs1, rs2 = rc_ref[...], rs1_ref[...], rs2_ref[...]

    zcq = _dot(h, win_ref[:, Z_CQ[0]:Z_CQ[1]])
    cq = zcq * lax.rsqrt(jnp.mean(zcq * zcq, -1, keepdims=True) + 1e-6) * qn_ref[...]
    q = _dot(cq.astype(BF16), wuq_ref[...]) * MLA_SCALE
    for hd in range(MLA_HEADS):
        lo = hd * HEAD_PAD
        qh_ref[:, lo:lo + HEAD_PAD] = _rope_group(q[:, lo:lo + HEAD_PAD], rc, rs1, rs2).astype(BF16)

    zckv = _dot(h, win_ref[:, Z_CKV[0]:Z_CKV[1]])
    ckv = zckv * lax.rsqrt(jnp.mean(zckv * zckv, -1, keepdims=True) + 1e-6) * kvn_ref[...]
    ckv_ref[...] = ckv
    ckv_b = ckv.astype(BF16)
    kpe = _rope_group(_dot(h, win_ref[:, Z_KPE[0]:Z_KPE[1]]), rc, rs1, rs2)
    kpe_ref[...] = pltpu.roll(kpe, LANES - ROPE_LANE0, 1)[:, :MLA_ROPE_DIM]
    kn = _dot(ckv_b, wuk_ref[...])
    for hd in range(MLA_HEADS):
        lo = hd * HEAD_PAD
        kh_ref[:, lo:lo + HEAD_PAD] = (kn[:, lo:lo + HEAD_PAD] + kpe).astype(BF16)
    vt_ref[...] = _dot_nt(wuv_ref[...], ckv_b).astype(BF16)


def _mix_in_call(grp, x, l, w, rope):
    rows = grp.rows
    v_rows = MLA_HEADS * MLA_V_DIM
    outs = [((rows, SSM_WIDTH), F32), ((rows, 4 * HG_WIDTH), F32), ((rows, MLA_PAD), BF16),
            ((rows, MLA_PAD), BF16), (grp.tcol_shape(v_rows), BF16), ((rows, MLA_KV_LORA), F32),
            ((rows, MLA_ROPE_DIM), F32)]
    out_specs = [grp.row_spec(s[1]) for s, _ in outs]
    out_specs[4] = grp.tcol_spec(v_rows)
    return pl.pallas_call(
        _mix_in_kernel,
        grid=grp.grid,
        in_specs=[grp.row_spec(D_MODEL), grp.mod_spec(l, 3), grp.mod_spec(l, 4),
                  _const_spec((None, D_MODEL, N_IN_PAD), (l, 0, 0)),
                  _const_spec((None, 1, MLA_Q_LORA), (l, 0, 0)),
                  _const_spec((None, MLA_Q_LORA, MLA_PAD), (l, 0, 0)),
                  _const_spec((None, 1, MLA_KV_LORA), (l, 0, 0)),
                  _const_spec((None, MLA_KV_LORA, MLA_PAD), (l, 0, 0)),
                  _const_spec((None, v_rows, MLA_KV_LORA), (l, 0, 0)),
                  grp.pos_spec(LANES), grp.pos_spec(LANES), grp.pos_spec(LANES)],
        out_specs=out_specs,
        out_shape=[jax.ShapeDtypeStruct(s, d) for s, d in outs],
        compiler_params=_cparams(("parallel",)),
        name="mix_in",
    )(x, grp.mod, grp.mod, w["w_in"], w["q_norm"], w["w_uq"], w["kv_norm"], w["w_uk"], w["w_uv_t"], *rope)


def _s5_kernel(u_ref, m_ref, wis_ref, wso_ref, av_ref, d_ref, x0_ref, y_ref, xf_ref, sv_scr, sw_scr, x_scr,
               *, nb, nc):
    u = u_ref[...]
    ub = u.astype(BF16)
    sv_scr[...] = _dot(ub, wis_ref[:, 0:LANES])
    sw_scr[...] = _dot(ub, wis_ref[:, LANES:2 * LANES])
    a1, a2, a3 = av_ref[0:1, :], av_ref[1:2, :], av_ref[2:3, :]

    def body(c, carry):
        v, w = carry
        rows_c = pl.ds(c, nb, stride=nc)
        x_scr[rows_c, :] = v
        return a1 * v + a2 * w + sv_scr[rows_c, :], a1 * w + a3 * v + sw_scr[rows_c, :]

    v, _ = lax.fori_loop(0, nc, body, (x0_ref[:, 0:LANES], x0_ref[:, LANES:2 * LANES]), unroll=8)
    xf_ref[...] = v
    y_ref[...] = _dot(ub, m_ref[...]) + _dot(x_scr[...].astype(BF16), wso_ref[...]) + u * d_ref[...]


def _s5_pack_kernel(u_ref, sel_ref, o_ref, lo_scr, hi_scr, *, nch):
    g = SSM_GROUPS
    lo_scr[...] = u_ref[:, 0:LANES]
    hi_scr[...] = u_ref[:, LANES:2 * LANES]
    lane_grp = lax.broadcasted_iota(jnp.int32, (g * nch, SSM_WIDTH), 1) // SSM_GROUP
    row_grp = lax.broadcasted_iota(jnp.int32, (g * nch, SSM_WIDTH), 0) // nch
    keep = lane_grp == row_grp
    acc = jnp.zeros((g * nch, SSM_WIDTH), F32)
    for t in range(S5_CHUNK):
        step_rows = pl.ds(t, nch, stride=S5_CHUNK)
        rows = jnp.concatenate([lo_scr[step_rows, :], hi_scr[step_rows, :]], 1)
        lhs = jnp.where(keep, jnp.tile(rows, (g, 1)), 0.0).astype(BF16)
        acc = acc + _dot(lhs, sel_ref[t])
    o_ref[...] = acc.reshape(g, nch, SSM_WIDTH).astype(o_ref.dtype)


def _s5_unpack_kernel(y_ref, sel_ref, o_ref, lo_scr, hi_scr, *, nch):
    g = SSM_GROUPS
    lane_grp = lax.broadcasted_iota(jnp.int32, (g * nch, SSM_WIDTH), 1) // SSM_GROUP
    row_grp = lax.broadcasted_iota(jnp.int32, (g * nch, SSM_WIDTH), 0) // nch
    keep = lane_grp == row_grp
    yb = y_ref[...].reshape(g * nch, SSM_WIDTH).astype(BF16)
    for t in range(S5_CHUNK):
        z = jnp.where(keep, _dot(yb, sel_ref[t]), 0.0)
        tok = jnp.sum(z.reshape(g, nch, SSM_WIDTH), 0)
        step_rows = pl.ds(t, nch, stride=S5_CHUNK)
        lo_scr[step_rows, :] = tok[:, 0:LANES]
        hi_scr[step_rows, :] = tok[:, LANES:2 * LANES]
    o_ref[...] = jnp.concatenate([lo_scr[...], hi_scr[...]], 1)


def _s5_relayout_call(kern, x, sel, grp, to_groups):
    nch = grp.tm // S5_CHUNK
    tpb = grp.tiles_per_batch
    tok_spec = grp.row_spec(SSM_WIDTH)
    grp_spec = pl.BlockSpec((SSM_GROUPS, None, nch, SSM_WIDTH), lambda i: (0, i // tpb, i % tpb, 0))
    grp_shape = (SSM_GROUPS, grp.batch, grp.seq // S5_CHUNK, SSM_WIDTH)
    return pl.pallas_call(
        functools.partial(kern, nch=nch),
        grid=grp.grid,
        in_specs=[tok_spec if to_groups else grp_spec,
                  _const_spec((S5_CHUNK, SSM_WIDTH, SSM_WIDTH), (0, 0, 0))],
        out_specs=grp_spec if to_groups else tok_spec,
        out_shape=jax.ShapeDtypeStruct(grp_shape, BF16) if to_groups
        else jax.ShapeDtypeStruct((grp.rows, SSM_WIDTH), F32),
        scratch_shapes=[pltpu.VMEM((grp.tm, LANES), F32)] * 2,
        compiler_params=_cparams(("parallel",)),
        name="s5_pack" if to_groups else "s5_unpack",
    )(x, sel)


def _s5_call(u_g, x0, w, l, nb, nc):
    g = SSM_GROUPS
    rows = nb * nc
    wide = S5_CHUNK * SSM_GROUP
    u_g = u_g.reshape(g, rows, wide)
    return pl.pallas_call(
        functools.partial(_s5_kernel, nb=nb, nc=nc),
        grid=(g,),
        in_specs=[pl.BlockSpec((None, rows, wide), lambda i: (i, 0, 0)),
                  pl.BlockSpec((None, None, wide, wide), lambda i: (l, i, 0, 0)),
                  pl.BlockSpec((None, None, wide, wide), lambda i: (l, i, 0, 0)),
                  pl.BlockSpec((None, None, LANES, wide), lambda i: (l, i, 0, 0)),
                  pl.BlockSpec((None, None, 8, LANES), lambda i: (l, i, 0, 0)),
                  pl.BlockSpec((None, None, 1, wide), lambda i: (l, i, 0, 0)),
                  pl.BlockSpec((None, nb, wide), lambda i: (i, 0, 0))],
        out_specs=[pl.BlockSpec((None, rows, wide), lambda i: (i, 0, 0)),
                   pl.BlockSpec((None, nb, LANES), lambda i: (i, 0, 0))],
        out_shape=[jax.ShapeDtypeStruct((g, rows, wide), F32), jax.ShapeDtypeStruct((g, nb, LANES), F32)],
        scratch_shapes=[pltpu.VMEM((rows, LANES), F32)] * 3,
        compiler_params=_cparams(("parallel",)),
        name="s5_chunks",
    )(u_g, w["s5_m"], w["s5_wis"], w["s5_wso"], w["s5_av"], w["s5_d"], x0)


def _s5_step_kernel(u_ref, xr_ref, xi_ref, bb_ref, cc_ref, ar_ref, ai_ref, d_ref, y_ref, or_ref, oi_ref):
    u = u_ref[...]
    n = SSM_GROUPS * SSM_STATE
    bu = _dot(u.astype(BF16), bb_ref[...])
    xr, xi, ar, ai = xr_ref[...], xi_ref[...], ar_ref[...], ai_ref[...]
    nr = ar * xr - ai * xi + bu[:, :n]
    ni = ar * xi + ai * xr + bu[:, n:]
    or_ref[...] = nr
    oi_ref[...] = ni
    y_ref[...] = (_dot(nr.astype(BF16), cc_ref[0:n, :]) + _dot(ni.astype(BF16), cc_ref[n:2 * n, :])
                  + u * d_ref[...])


def _s5_step_call(u, xr, xi, w, l):
    b = u.shape[0]
    n = SSM_GROUPS * SSM_STATE
    full = lambda shape: pl.BlockSpec(shape, lambda i: (0,) * len(shape))
    lsel = lambda shape: pl.BlockSpec((None,) + shape, lambda i: (l,) + (0,) * len(shape))
    return pl.pallas_call(
        _s5_step_kernel,
        grid=(1,),
        in_specs=[full((b, SSM_WIDTH)), full((b, n)), full((b, n)),
                  lsel((SSM_WIDTH, 2 * n)), lsel((2 * n, SSM_WIDTH)), lsel((1, n)), lsel((1, n)),
                  lsel((1, SSM_WIDTH))],
        out_specs=[full((b, SSM_WIDTH)), full((b, n)), full((b, n))],
        out_shape=[jax.ShapeDtypeStruct((b, SSM_WIDTH), F32), jax.ShapeDtypeStruct((b, n), F32),
                   jax.ShapeDtypeStruct((b, n), F32)],
        compiler_params=_cparams(("arbitrary",)),
        name="s5_step",
    )(u, xr, xi, w["s5_bb"], w["s5_cc"], w["s5_ar"], w["s5_ai"], w["s5_dflat"])


def _hgrn_kernel(zh_ref, s0_ref, lbp_ref, e_ref, bd_ref, lcum_ref, lall_ref, o_ref, sf_ref,
                 perm_scr, acc_scr, st_ref, oi_ref, *, tt, n_valid):
    j = pl.program_id(1)

    @pl.when(j == 0)
    def _():
        st_ref[...] = s0_ref[...]

    w = HG_WIDTH
    ng = tt // HG_GROUP
    zf = zh_ref[:, 0:w]
    q = zh_ref[:, w:2 * w]
    v = zh_ref[:, 2 * w:3 * w]
    lbm, oml = lbp_ref[0:1, :], lbp_ref[1:2, :]
    sig = 1.0 / (1.0 + jnp.exp(-zf))
    f = lbm + oml * sig
    k = oml * (1.0 - sig)
    if n_valid < tt:
        live = lax.broadcasted_iota(jnp.int32, (tt, w), 0) < n_valid
        f = jnp.where(live, f, 1.0)
        k = jnp.where(live, k, 0.0)
    lf = jnp.log(f)

    def regroup(i, x):
        if ng == 1:
            return x
        perm_scr[2 * i] = x[:, 0:LANES]
        perm_scr[2 * i + 1] = x[:, LANES:2 * LANES]
        blocks = []
        for pos in range(HG_GROUP):
            rows = pl.ds(pos, ng, stride=HG_GROUP)
            blocks.append(jnp.concatenate([perm_scr[2 * i, rows, :], perm_scr[2 * i + 1, rows, :]], 1))
        return jnp.concatenate(blocks, 0)

    qg, kg, vg, fg = regroup(0, q), regroup(1, k), regroup(2, v), regroup(3, f)
    e = e_ref[...]
    acc_scr[...] = _dot((qg * kg).astype(BF16), e) * vg
    decay = None
    for d in range(1, HG_GROUP):
        n_rows = (HG_GROUP - d) * ng
        f_blk = fg[ng:ng + n_rows]
        decay = f_blk if decay is None else decay[ng:] * f_blk
        p = qg[d * ng:] * kg[:n_rows] * decay
        acc_scr[d * ng:, :] += _dot(p.astype(BF16), e) * vg[:n_rows]
    if ng == 1:
        acc = acc_scr[...]
    else:
        for pos in range(HG_GROUP):
            rows = pl.ds(pos, ng, stride=HG_GROUP)
            perm_scr[0, rows, :] = acc_scr[pos * ng:(pos + 1) * ng, 0:LANES]
            perm_scr[1, rows, :] = acc_scr[pos * ng:(pos + 1) * ng, LANES:2 * LANES]
        acc = jnp.concatenate([perm_scr[0], perm_scr[1]], 1)

    b = _split_dot(lcum_ref[...], lf, 3)
    bl = _split_dot(lall_ref[...], lf, 3)
    qe = (q * jnp.exp(b)).astype(BF16)
    kx = (k * jnp.exp(bl - b)).astype(BF16)
    ebl = jnp.exp(bl)
    vb = v.astype(BF16)
    bd = bd_ref[...]
    for g in range(tt // HG_GROUP):
        lo = g * HG_GROUP
        st = st_ref[...]
        oi_ref[lo:lo + HG_GROUP, :] = _dot_nt(qe[lo:lo + HG_GROUP], st.astype(BF16))
        upd = _dot_tn(vb[lo:lo + HG_GROUP], kx[lo:lo + HG_GROUP])
        st_ref[...] = st * ebl[lo:lo + 1, :] + upd * bd
    o_ref[...] = acc + oi_ref[...]

    @pl.when(j == pl.num_programs(1) - 1)
    def _():
        sf_ref[...] = st_ref[...]


def _hgrn_call(zh, s0t, w, l, batch, seq_rows, tt, n_valid):
    wd = HG_WIDTH
    nj = seq_rows // tt
    return pl.pallas_call(
        functools.partial(_hgrn_kernel, tt=tt, n_valid=n_valid),
        grid=(batch, nj),
        in_specs=[pl.BlockSpec((tt, 4 * wd), lambda b, j: (b * nj + j, 0)),
                  pl.BlockSpec((None, wd, wd), lambda b, j: (b, 0, 0)),
                  pl.BlockSpec((None, 8, wd), lambda b, j: (l, 0, 0)),
                  pl.BlockSpec((wd, wd), lambda b, j: (0, 0)),
                  pl.BlockSpec((wd, wd), lambda b, j: (0, 0)),
                  pl.BlockSpec((tt, tt), lambda b, j: (0, 0)),
                  pl.BlockSpec((tt, tt), lambda b, j: (0, 0))],
        out_specs=[pl.BlockSpec((tt, wd), lambda b, j: (b * nj + j, 0)),
                   pl.BlockSpec((None, wd, wd), lambda b, j: (b, 0, 0))],
        out_shape=[jax.ShapeDtypeStruct((batch * seq_rows, wd), F32),
                   jax.ShapeDtypeStruct((batch, wd, wd), F32)],
        scratch_shapes=[pltpu.VMEM((8, tt, LANES), F32), pltpu.VMEM((tt, wd), F32),
                        pltpu.VMEM((wd, wd), F32), pltpu.VMEM((tt, wd), F32)],
        compiler_params=_cparams(("parallel", "arbitrary")),
        name="hgrn",
    )(zh, s0t, w["hg_lbp"], w["hg_e"], w["hg_bd"], w["hg_lcum"][tt], w["hg_lall"][tt])


def _flash_kernel(q_ref, k_ref, vt_ref, o_ref, m_ref, l_ref, acc_ref):
    qi = pl.program_id(1)
    ki = pl.program_id(2)

    @pl.when(ki == 0)
    def _():
        m_ref[...] = jnp.full(m_ref.shape, NEG_MASK, F32)
        l_ref[...] = jnp.zeros(l_ref.shape, F32)
        acc_ref[...] = jnp.zeros(acc_ref.shape, F32)

    def step(diagonal):
        for hd in range(MLA_HEADS):
            sl = slice(hd * HEAD_PAD, (hd + 1) * HEAD_PAD)
            vs = slice(hd * MLA_V_DIM, (hd + 1) * MLA_V_DIM)
            st = _dot_nt(k_ref[:, sl], q_ref[:, sl])
            if diagonal:
                key = lax.broadcasted_iota(jnp.int32, st.shape, 0)
                qry = lax.broadcasted_iota(jnp.int32, st.shape, 1)
                st = jnp.where(key <= qry, st, NEG_MASK)
            m_prev = m_ref[hd:hd + 1, :]
            m_new = jnp.maximum(m_prev, jnp.max(st, 0, keepdims=True))
            alpha = jnp.exp(m_prev - m_new)
            p = jnp.exp(st - m_new)
            l_ref[hd:hd + 1, :] = alpha * l_ref[hd:hd + 1, :] + jnp.sum(p, 0, keepdims=True)
            acc_ref[vs, :] = alpha * acc_ref[vs, :] + _dot(vt_ref[vs, :], p.astype(BF16))
            m_ref[hd:hd + 1, :] = m_new

    @pl.when(ki < qi)
    def _():
        step(False)

    @pl.when(ki == qi)
    def _():
        step(True)
        for hd in range(MLA_HEADS):
            vs = slice(hd * MLA_V_DIM, (hd + 1) * MLA_V_DIM)
            o_ref[vs, :] = acc_ref[vs, :] / l_ref[hd:hd + 1, :]


def _flash_call(qh, kh, vt, batch, seq, tq):
    nq = seq // tq
    v_rows = MLA_HEADS * MLA_V_DIM
    q_spec = pl.BlockSpec((None, tq, MLA_PAD), lambda b, qi, ki: (b, qi, 0))
    k_spec = pl.BlockSpec((None, tq, MLA_PAD), lambda b, qi, ki: (b, jnp.minimum(ki, qi), 0))
    vt_spec = pl.BlockSpec((None, v_rows, tq), lambda b, qi, ki: (b, 0, jnp.minimum(ki, qi)))
    shp = (batch, seq, MLA_PAD)
    return pl.pallas_call(
        _flash_kernel,
        grid=(batch, nq, nq),
        in_specs=[q_spec, k_spec, vt_spec],
        out_specs=pl.BlockSpec((None, v_rows, tq), lambda b, qi, ki: (b, 0, qi)),
        out_shape=jax.ShapeDtypeStruct((batch, v_rows, seq), F32),
        scratch_shapes=[pltpu.VMEM((MLA_HEADS, tq), F32), pltpu.VMEM((MLA_HEADS, tq), F32),
                        pltpu.VMEM((v_rows, tq), F32)],
        compiler_params=_cparams(("parallel", "parallel", "arbitrary")),
        name="flash",
    )(qh.reshape(shp), kh.reshape(shp), vt)


PAGES_PER_STEP = 16


def _decode_kernel(pt_ref, qh_ref, ckvn_ref, kpen_ref, wuk_ref, wuv_ref, sel_ref, hm_ref, ckv_hbm, kpe_hbm,
                   o_ref, ckv_buf, kpe_buf, sem_c, sem_k, *, layer, npg, ng):
    b = pl.program_id(0)
    hm = hm_ref[...]

    def copies(seq_i, grp_i, slot):
        out = []
        for i in range(npg):
            page = pt_ref[seq_i, grp_i * npg + i]
            out.append(pltpu.make_async_copy(ckv_hbm.at[layer, page], ckv_buf.at[slot, i], sem_c.at[slot]))
            out.append(pltpu.make_async_copy(kpe_hbm.at[layer, page], kpe_buf.at[slot, i], sem_k.at[slot]))
        return out

    @pl.when(b == 0)
    def _():
        for c in copies(0, 0, 0):
            c.start()

    qbd = jnp.where(hm > 0, jnp.broadcast_to(qh_ref[...], hm.shape), jnp.zeros_like(hm)).astype(BF16)
    qa = _dot_nt(qbd, wuk_ref[...]).astype(BF16)
    qp = _dot(qbd, sel_ref[...]).astype(BF16)
    m = jnp.full((MLA_HEADS, 1), NEG_MASK, F32)
    l = jnp.zeros((MLA_HEADS, 1), F32)
    acc = jnp.zeros((MLA_HEADS, MLA_KV_LORA), F32)
    for j in range(ng):
        slot = lax.rem(b * ng + j, 2)
        if j + 1 < ng:
            for c in copies(b, j + 1, 1 - slot):
                c.start()
        else:
            @pl.when(b + 1 < pl.num_programs(0))
            def _():
                for c in copies(b + 1, 0, 1 - slot):
                    c.start()
        for c in copies(b, j, slot):
            c.wait()
        pages = [ckv_buf[slot, i].astype(BF16) for i in range(npg)]
        s = jnp.concatenate(
            [_dot_nt(qa, pg) + _dot(qp, kpe_buf[slot, i].astype(BF16)) for i, pg in enumerate(pages)], axis=1)
        m_new = jnp.maximum(m, jnp.max(s, -1, keepdims=True))
        alpha = jnp.exp(m - m_new)
        p = jnp.exp(s - m_new)
        l = alpha * l + jnp.sum(p, -1, keepdims=True)
        pb = p.astype(BF16)
        acc = alpha * acc
        for i, pg in enumerate(pages):
            acc = acc + _dot(pb[:, i * PAGE_SIZE:(i + 1) * PAGE_SIZE], pg)
        m = m_new

    ckvn = ckvn_ref[...]
    s_new = (jnp.sum(qa.astype(F32) * ckvn.astype(BF16).astype(F32), -1, keepdims=True)
             + jnp.sum(qp.astype(F32) * kpen_ref[...].astype(BF16).astype(F32), -1, keepdims=True))
    m_fin = jnp.maximum(m, s_new)
    a = jnp.exp(m - m_fin)
    p_new = jnp.exp(s_new - m_fin)
    l_fin = a * l + p_new
    lat = a * acc + p_new.astype(BF16).astype(F32) * ckvn.astype(BF16).astype(F32)
    lat = lat / l_fin
    o = _dot(lat.astype(BF16), wuv_ref[...])
    o_ref[...] = jnp.sum(jnp.where(hm > 0, o, 0.0), 0, keepdims=True)


def _decode_call(page_table, qh, ckv_new, kpe_new, cache_ckv, cache_kpe, w, l, sel, hm):
    batch, n_pages = page_table.shape
    npg = min(PAGES_PER_STEP, n_pages)
    ng = n_pages // npg
    row3 = lambda width: pl.BlockSpec((None, 1, width), lambda b, pt: (b, 0, 0))
    wsel = lambda shape: pl.BlockSpec((None,) + shape, lambda b, pt: (l,) + (0,) * len(shape))
    full = lambda shape: pl.BlockSpec(shape, lambda b, pt: (0,) * len(shape))
    hbm = pl.BlockSpec(memory_space=pl.ANY)
    grid_spec = pltpu.PrefetchScalarGridSpec(
        num_scalar_prefetch=1,
        grid=(batch,),
        in_specs=[row3(MLA_PAD), row3(MLA_KV_LORA), row3(MLA_ROPE_DIM),
                  wsel((MLA_KV_LORA, MLA_PAD)), wsel((MLA_KV_LORA, MLA_PAD)),
                  full((MLA_PAD, MLA_ROPE_DIM)), full((MLA_HEADS, MLA_PAD)), hbm, hbm],
        out_specs=row3(MLA_PAD),
        scratch_shapes=[pltpu.VMEM((2, npg, PAGE_SIZE, MLA_KV_LORA), cache_ckv.dtype),
                        pltpu.VMEM((2, npg, MLA_ROPE_DIM, PAGE_SIZE), cache_kpe.dtype),
                        pltpu.SemaphoreType.DMA((2,)), pltpu.SemaphoreType.DMA((2,))],
    )
    out = pl.pallas_call(
        functools.partial(_decode_kernel, layer=l, npg=npg, ng=ng),
        grid_spec=grid_spec,
        out_shape=jax.ShapeDtypeStruct((batch, 1, MLA_PAD), F32),
        compiler_params=_cparams(("arbitrary",)),
        name="decode",
    )(page_table, qh.reshape(batch, 1, MLA_PAD), ckv_new.reshape(batch, 1, MLA_KV_LORA),
      kpe_new.reshape(batch, 1, MLA_ROPE_DIM), w["w_uk"], w["w_uv"], sel, hm, cache_ckv, cache_kpe)
    return out.reshape(batch, MLA_PAD)


def _mix_out_kernel(x_ref, gt_ref, y_ref, zg_ref, oh_ref, om_ref, wglu_ref, nssm_ref, nhg_ref, nmla_ref,
                    e64_ref, woa_ref, wob_ref, lg_ref, lb_ref, o_ref, cat_ref):
    w = SSM_WIDTH
    gab = _dot(_gelu_tanh(y_ref[...]).astype(BF16), wglu_ref[...])
    t = gab[:, :w] * _sigmoid(gab[:, w:])
    cat_ref[:, 0:w] = (t * lax.rsqrt(jnp.mean(t * t, -1, keepdims=True) + 1e-6) * nssm_ref[...]).astype(BF16)

    oh = oh_ref[...]
    msq = _split_dot_r(oh * oh, e64_ref[...], 2)
    cat_ref[:, w:2 * w] = (oh * lax.rsqrt(msq + 1e-6) * nhg_ref[...] * _silu(zg_ref[...])).astype(BF16)

    om = om_ref[...]
    tm = om.shape[1]
    nm = nmla_ref[...]
    nm = nm[:, :tm] if tm <= LANES else jnp.tile(nm, (1, tm // LANES))
    ms = jnp.mean(om * om, 0, keepdims=True)
    o_mla = (om * lax.rsqrt(ms + 1e-6) * nm).astype(BF16)

    mix = _dot(cat_ref[...], woa_ref[...]) + _dot_tn(o_mla, wob_ref[...])
    y = ALPHA * x_ref[...] + (1.0 + gt_ref[...]) * mix
    o_ref[...] = _layer_norm(y, lg_ref[...], lb_ref[...])


def _mix_out_call(grp, x, l, y_ssm, zh, o_h, o_m, w, ln_g, ln_b):
    tm = grp.tm
    v_rows = MLA_HEADS * MLA_V_DIM
    return pl.pallas_call(
        _mix_out_kernel,
        grid=grp.grid,
        in_specs=[grp.row_spec(D_MODEL), grp.mod_spec(l, 5),
                  grp.row_spec(SSM_WIDTH), grp.row_spec(HG_WIDTH, col=3), grp.row_spec(HG_WIDTH),
                  grp.tcol_spec(v_rows),
                  _const_spec((None, SSM_WIDTH, 2 * SSM_WIDTH), (l, 0, 0)),
                  _const_spec((None, 1, SSM_WIDTH), (l, 0, 0)),
                  _const_spec((None, 1, HG_WIDTH), (l, 0, 0)),
                  _const_spec((None, v_rows, LANES), (l, 0, 0)),
                  _const_spec((HG_WIDTH, HG_WIDTH), (0, 0)),
                  _const_spec((None, SSM_WIDTH + HG_WIDTH, D_MODEL), (l, 0, 0)),
                  _const_spec((None, v_rows, D_MODEL), (l, 0, 0)),
                  _const_spec((None, None, 1, D_MODEL), (l, 1, 0, 0)),
                  _const_spec((None, None, 1, D_MODEL), (l, 1, 0, 0))],
        out_specs=grp.row_spec(D_MODEL),
        out_shape=jax.ShapeDtypeStruct((grp.rows, D_MODEL), F32),
        scratch_shapes=[pltpu.VMEM((tm, SSM_WIDTH + HG_WIDTH), BF16)],
        compiler_params=_cparams(("parallel",)),
        name="mix_out",
    )(x, grp.mod, y_ssm, zh, o_h, o_m, w["w_glu"], w["n_ssm"], w["n_hg"], w["n_mla"], w["hg_e64"],
      w["w_out_a"], w["w_out_b"], ln_g, ln_b)


def _pad_heads(wt, per_head):
    lead = wt.shape[:-1]
    wt = wt.reshape(lead + (MLA_HEADS, per_head))
    wt = jnp.pad(wt, [(0, 0)] * len(lead) + [(0, 0), (0, HEAD_PAD - per_head)])
    return wt.reshape(lead + (MLA_PAD,))


def _s5_tables(lam_re, lam_im, b_re, b_im, c_re, c_im, d, log_step):
    hp = lax.Precision.HIGHEST
    depth, g, p = lam_re.shape
    h, t = SSM_GROUP, S5_CHUNK
    step = jnp.exp(log_step)[..., None]
    mag = jnp.exp(lam_re * step)
    ab_re, ab_im = mag * jnp.cos(lam_im * step), mag * jnp.sin(lam_im * step)
    den = lam_re * lam_re + lam_im * lam_im
    nr = ab_re - 1.0
    coef_re = (nr * lam_re + ab_im * lam_im) / den
    coef_im = (ab_im * lam_re - nr * lam_im) / den
    bb_re = coef_re[..., None] * b_re - coef_im[..., None] * b_im
    bb_im = coef_re[..., None] * b_im + coef_im[..., None] * b_re
    n = jnp.arange(t + 1, dtype=F32)[:, None, None, None]
    pmag = jnp.exp(n * lam_re * step)
    pw_re, pw_im = pmag * jnp.cos(n * lam_im * step), pmag * jnp.sin(n * lam_im * step)
    pb_re = pw_re[:t, ..., None] * bb_re - pw_im[:t, ..., None] * bb_im
    pb_im = pw_re[:t, ..., None] * bb_im + pw_im[:t, ..., None] * bb_re
    kern = (jnp.einsum('lgop,nlgpi->nlgoi', c_re, pb_re, precision=hp)
            - jnp.einsum('lgop,nlgpi->nlgoi', c_im, pb_im, precision=hp))
    kern = jnp.concatenate([kern, jnp.zeros_like(kern[:1])], 0)
    s_idx, t_idx = np.arange(t)[:, None], np.arange(t)[None, :]
    lag = np.where(t_idx >= s_idx, t_idx - s_idx, t)
    m = kern[lag]
    m = jnp.transpose(m, (2, 3, 0, 5, 1, 4)).reshape(depth, g, t * h, t * h)
    wr = jnp.transpose(pb_re[::-1], (1, 2, 0, 4, 3)).reshape(depth, g, t * h, p)
    wi = jnp.transpose(pb_im[::-1], (1, 2, 0, 4, 3)).reshape(depth, g, t * h, p)
    wis = jnp.concatenate([wr, wi, wi, wr], -1)
    cp_re = c_re[None] * pw_re[1:, :, :, None, :] - c_im[None] * pw_im[1:, :, :, None, :]
    cp_im = c_re[None] * pw_im[1:, :, :, None, :] + c_im[None] * pw_re[1:, :, :, None, :]
    so_re = jnp.transpose(cp_re, (1, 2, 4, 0, 3)).reshape(depth, g, p, t * h)
    so_im = jnp.transpose(-cp_im, (1, 2, 4, 0, 3)).reshape(depth, g, p, t * h)
    wso = jnp.concatenate([so_re, so_im], 2)
    a_re, a_im = pw_re[t], pw_im[t]
    av = jnp.stack([jnp.concatenate([a_re, a_re], -1), jnp.concatenate([-a_im, a_im], -1),
                    jnp.concatenate([a_im, -a_im], -1)], 2)
    av = jnp.pad(av, ((0, 0), (0, 0), (0, 5), (0, 0)))
    dg = d.reshape(depth, g, 1, h)
    d_t = jnp.tile(dg, (1, 1, 1, t))
    eye = jnp.eye(g, dtype=F32)
    bbr = jnp.einsum('lgph,gk->lghkp', bb_re, eye).reshape(depth, g * h, g * p)
    bbi = jnp.einsum('lgph,gk->lghkp', bb_im, eye).reshape(depth, g * h, g * p)
    ccr = jnp.einsum('lghp,gk->lgpkh', c_re, eye).reshape(depth, g * p, g * h)
    cci = jnp.einsum('lghp,gk->lgpkh', -c_im, eye).reshape(depth, g * p, g * h)
    return dict(s5_m=m.astype(BF16), s5_wis=wis.astype(BF16), s5_wso=wso.astype(BF16), s5_av=av, s5_d=d_t,
                s5_bb=jnp.concatenate([bbr, bbi], -1).astype(BF16),
                s5_cc=jnp.concatenate([ccr, cci], 1).astype(BF16),
                s5_ar=ab_re.reshape(depth, 1, g * p), s5_ai=ab_im.reshape(depth, 1, g * p),
                s5_dflat=d.reshape(depth, 1, g * h))


def _rope_tables(pos):
    half = MLA_ROPE_DIM // 2
    inv = 1.0 / (ROPE_THETA ** (jnp.arange(0, MLA_ROPE_DIM, 2, dtype=F32) / MLA_ROPE_DIM))
    ang = pos.astype(F32)[:, None] * inv[None, :]
    cos, sin = jnp.cos(ang), jnp.sin(ang)
    n = pos.shape[0]
    one, zero = jnp.ones((n, ROPE_LANE0), F32), jnp.zeros((n, ROPE_LANE0), F32)
    z16, z32 = jnp.zeros((n, half), F32), jnp.zeros((n, LANES - ROPE_LANE0 - MLA_ROPE_DIM), F32)
    rc = jnp.concatenate([one, cos, cos, z32], 1)
    rs1 = jnp.concatenate([zero, -sin, z16, z32], 1)
    rs2 = jnp.concatenate([zero, z16, sin, z32], 1)
    return rc, rs1, rs2


def _group_patterns(tt):
    r = np.arange(tt)
    same = (r[:, None] // HG_GROUP) == (r[None, :] // HG_GROUP)
    lcum = same & (r[None, :] <= r[:, None])
    return jnp.asarray(lcum, BF16), jnp.asarray(same, BF16)


def _prepare(p):
    depth = p["w_in"].shape[0]
    w = {}
    w["ffn_w_gu"] = p["ffn_w_gu"].astype(BF16)
    w["ffn_w_down"] = p["ffn_w_down"].astype(BF16)
    w_in = p["w_in"]
    n_main = Z_CKV[1]
    zpad = lambda n: jnp.zeros((depth, D_MODEL, n), w_in.dtype)
    w["w_in"] = jnp.concatenate(
        [w_in[:, :, :n_main], zpad(ROPE_LANE0), w_in[:, :, n_main:], zpad(LANES - ROPE_LANE0 - MLA_ROPE_DIM)],
        -1).astype(BF16)
    w["w_uq"] = _pad_heads(p["mla_w_uq"], MLA_NOPE_DIM + MLA_ROPE_DIM).astype(BF16)
    w["w_uk"] = _pad_heads(p["mla_w_uk"], MLA_NOPE_DIM).astype(BF16)
    w["w_uv"] = _pad_heads(p["mla_w_uv"], MLA_V_DIM).astype(BF16)
    w["w_uv_t"] = jnp.swapaxes(p["mla_w_uv"], 1, 2).astype(BF16)
    w["q_norm"] = p["mla_q_norm"].reshape(depth, 1, MLA_Q_LORA)
    w["kv_norm"] = p["mla_kv_norm"].reshape(depth, 1, MLA_KV_LORA)
    w["w_glu"] = p["ssm_w_glu"].astype(BF16)
    w["n_ssm"] = p["norm_ssm"].reshape(depth, 1, SSM_WIDTH)
    w["n_hg"] = p["norm_hgrn"].reshape(depth, 1, HG_WIDTH)
    w["n_mla"] = jnp.broadcast_to(p["norm_mla"][:, :, None], (depth, MLA_HEADS * MLA_V_DIM, LANES))
    w_out = p["w_out"]
    split = SSM_WIDTH + HG_WIDTH
    w["w_out_a"] = w_out[:, :split].astype(BF16)
    w["w_out_b"] = w_out[:, split:].astype(BF16)
    w.update(_s5_tables(p["ssm_lambda_re"].astype(F32), p["ssm_lambda_im"].astype(F32),
                        p["ssm_b_re"].astype(F32), p["ssm_b_im"].astype(F32),
                        p["ssm_c_re"].astype(F32), p["ssm_c_im"].astype(F32),
                        p["ssm_d"].astype(F32), p["ssm_log_step"].astype(F32)))
    lane = np.arange(SSM_WIDTH)
    step_of, chan_of = lane // SSM_GROUP, lane % SSM_GROUP
    t_idx = np.arange(S5_CHUNK)[:, None, None]
    fwd = (step_of[None, None, :] == t_idx) & (chan_of[None, None, :] == chan_of[None, :, None])
    w["s5_sel_fwd"] = jnp.asarray(fwd, BF16)
    w["s5_sel_bwd"] = jnp.asarray(np.swapaxes(fwd, 1, 2), BF16)
    sm = jax.nn.softmax(p["hgrn_lb_logits"].astype(F32), axis=0)
    lb = jnp.clip(jnp.cumsum(sm, axis=0) - sm[0:1], 0.0, 1.0 - 1e-6)
    rows = jnp.stack([jnp.maximum(lb, LB_FLOOR), 1.0 - lb], 1)
    w["hg_lbp"] = jnp.pad(rows, ((0, 0), (0, 6), (0, 0)))
    head = np.arange(HG_WIDTH) // HG_DK
    same_head = head[:, None] == head[None, :]
    w["hg_e"] = jnp.asarray(same_head, BF16)
    w["hg_bd"] = jnp.asarray(same_head, F32)
    w["hg_e64"] = jnp.asarray(same_head / HG_DK, BF16)
    return w


def _hg_state_in(s):
    b = s.shape[0]
    eye = jnp.eye(HG_HEADS, dtype=s.dtype)
    return jnp.einsum('bhkv,hg->bhvgk', s, eye).reshape(b, HG_WIDTH, HG_WIDTH)


def _hg_state_out(st):
    b = st.shape[0]
    st = st.reshape(b, HG_HEADS, HG_DK, HG_HEADS, HG_DK)
    diag = jnp.stack([st[:, h, :, h, :] for h in range(HG_HEADS)], 1)
    return jnp.transpose(diag, (0, 1, 3, 2))


def _run_prompt(x, grp, w, p, rope):
    batch, seq = grp.batch, grp.seq
    depth = p["w_in"].shape[0]
    nc = seq // S5_CHUNK
    tt = min(256, seq)
    tq = min(1024, seq)
    ln_g = p["ln_g"].reshape(depth, 3, 1, D_MODEL)
    ln_b = p["ln_b"].reshape(depth, 3, 1, D_MODEL)
    x0 = jnp.zeros((SSM_GROUPS, batch, 2 * LANES), F32)
    s0 = jnp.zeros((batch, HG_WIDTH, HG_WIDTH), F32)
    ckv_l, kpe_l, sre_l, sim_l, hg_l = [], [], [], [], []
    for l in range(depth):
        x = _ffn_call(grp, x, l, 0, 0, w["ffn_w_gu"], w["ffn_w_down"], ln_g, ln_b)
        u, zh, qh, kh, vt, ckv, kpe = _mix_in_call(grp, x, l, w, rope)
        u_g = _s5_relayout_call(_s5_pack_kernel, u, w["s5_sel_fwd"], grp, True)
        y_g, xf = _s5_call(u_g, x0, w, l, batch, nc)
        y = _s5_relayout_call(_s5_unpack_kernel, y_g.reshape(u_g.shape), w["s5_sel_bwd"], grp, False)
        o_h, st = _hgrn_call(zh, s0, w, l, batch, seq, tt, tt)
        o_m = _flash_call(qh, kh, vt, batch, seq, tq)
        x = _mix_out_call(grp, x, l, y, zh, o_h, o_m, w, ln_g, ln_b)
        x = _ffn_call(grp, x, l, 1, 2, w["ffn_w_gu"], w["ffn_w_down"], ln_g, ln_b)
        ckv_l.append(ckv.reshape(batch, seq, MLA_KV_LORA))
        kpe_l.append(kpe.reshape(batch, seq, MLA_ROPE_DIM))
        xf = jnp.transpose(xf, (1, 0, 2))
        sre_l.append(xf[..., :SSM_STATE])
        sim_l.append(xf[..., SSM_STATE:])
        hg_l.append(_hg_state_out(st))
    return (x.reshape(batch, seq, D_MODEL), jnp.stack(ckv_l), jnp.stack(kpe_l), jnp.stack(sre_l),
            jnp.stack(sim_l), jnp.stack(hg_l))


def _run_sample(x, grp, w, p, rope, cache_ckv, cache_kpe, page_table, ssm_re, ssm_im, hg_state):
    batch = grp.batch
    depth = p["w_in"].shape[0]
    ln_g = p["ln_g"].reshape(depth, 3, 1, D_MODEL)
    ln_b = p["ln_b"].reshape(depth, 3, 1, D_MODEL)
    n = SSM_GROUPS * SSM_STATE
    lane = np.arange(MLA_PAD)
    sel = np.zeros((MLA_PAD, MLA_ROPE_DIM), np.float32)
    for i in range(MLA_ROPE_DIM):
        sel[(lane % HEAD_PAD) == ROPE_LANE0 + i, i] = 1.0
    sel = jnp.asarray(sel, BF16)
    hm = jnp.asarray((lane[None, :] // HEAD_PAD) == np.arange(MLA_HEADS)[:, None], F32)
    cache_kpe = jnp.swapaxes(cache_kpe, 2, 3)
    ckv_l, kpe_l, sre_l, sim_l, hg_l = [], [], [], [], []
    for l in range(depth):
        x = _ffn_call(grp, x, l, 0, 0, w["ffn_w_gu"], w["ffn_w_down"], ln_g, ln_b)
        u, zh, qh, kh, _, ckv, kpe = _mix_in_call(grp, x, l, w, rope)
        y, nr, ni = _s5_step_call(u, ssm_re[l].reshape(batch, n), ssm_im[l].reshape(batch, n), w, l)
        zh_pad = jnp.pad(zh[:, None, :], ((0, 0), (0, HG_GROUP - 1), (0, 0))).reshape(batch * HG_GROUP, -1)
        o_hp, st = _hgrn_call(zh_pad, _hg_state_in(hg_state[l]), w, l, batch, HG_GROUP, HG_GROUP, 1)
        o_h = o_hp.reshape(batch, HG_GROUP, HG_WIDTH)[:, 0]
        o_m = _decode_call(page_table, qh, ckv, kpe, cache_ckv, cache_kpe, w, l, sel, hm)
        o_m = o_m.reshape(batch, MLA_HEADS, HEAD_PAD)[:, :, :MLA_V_DIM].reshape(batch, -1).T[None]
        x = _mix_out_call(grp, x, l, y, zh, o_h, o_m, w, ln_g, ln_b)
        x = _ffn_call(grp, x, l, 1, 2, w["ffn_w_gu"], w["ffn_w_down"], ln_g, ln_b)
        ckv_l.append(ckv.reshape(batch, 1, MLA_KV_LORA))
        kpe_l.append(kpe.reshape(batch, 1, MLA_ROPE_DIM))
        sre_l.append(nr.reshape(batch, SSM_GROUPS, SSM_STATE))
        sim_l.append(ni.reshape(batch, SSM_GROUPS, SSM_STATE))
        hg_l.append(_hg_state_out(st))
    return (x.reshape(batch, 1, D_MODEL), jnp.stack(ckv_l), jnp.stack(kpe_l), jnp.stack(sre_l),
            jnp.stack(sim_l), jnp.stack(hg_l))


def kernel(x_prompt, x_sample, cache_kv_latent, cache_k_rope, state_ssm_re, state_ssm_im, state_hgrn, page_table, c_prompt, c_sample, w_ada, b_ada, ln_g, ln_b, ffn_w_gu, ffn_w_down, w_in, w_out, ssm_lambda_re, ssm_lambda_im, ssm_b_re, ssm_b_im, ssm_c_re, ssm_c_im, ssm_d, ssm_log_step, ssm_w_glu, norm_ssm, hgrn_lb_logits, norm_hgrn, mla_q_norm, mla_w_uq, mla_kv_norm, mla_w_uk, mla_w_uv, norm_mla):
    p = dict(ln_g=ln_g, ln_b=ln_b, ffn_w_gu=ffn_w_gu, ffn_w_down=ffn_w_down, w_in=w_in, w_out=w_out,
             ssm_lambda_re=ssm_lambda_re, ssm_lambda_im=ssm_lambda_im, ssm_b_re=ssm_b_re, ssm_b_im=ssm_b_im,
             ssm_c_re=ssm_c_re, ssm_c_im=ssm_c_im, ssm_d=ssm_d, ssm_log_step=ssm_log_step, ssm_w_glu=ssm_w_glu,
             norm_ssm=norm_ssm, hgrn_lb_logits=hgrn_lb_logits, norm_hgrn=norm_hgrn, mla_q_norm=mla_q_norm,
             mla_w_uq=mla_w_uq, mla_kv_norm=mla_kv_norm, mla_w_uk=mla_w_uk, mla_w_uv=mla_w_uv, norm_mla=norm_mla)
    depth = w_in.shape[0]
    assert depth == DEPTH_ and x_sample.shape[1] == 1
    bp, seq, _ = x_prompt.shape
    bs = x_sample.shape[0]
    past_len = page_table.shape[1] * PAGE_SIZE
    w = _prepare(p)
    w["hg_lcum"], w["hg_lall"] = {}, {}
    for tt in {min(256, seq), HG_GROUP}:
        w["hg_lcum"][tt], w["hg_lall"][tt] = _group_patterns(tt)

    mod = _ada_call(jnp.concatenate([c_prompt, c_sample], 0), w_ada, b_ada)
    mod = mod.reshape(depth, bp + bs, 9, D_MODEL)
    grp_p = _Group(mod[:, :bp], bp, seq, min(512, seq))
    grp_s = _Group(mod[:, bp:], bs, 1, bs)

    rope_p = _rope_tables(jnp.arange(seq))
    rope_s = _rope_tables(past_len + jnp.arange(1))
    y_p, ckv_p, kpe_p, sre_p, sim_p, hg_p = _run_prompt(x_prompt.reshape(bp * seq, D_MODEL), grp_p, w, p, rope_p)
    y_s, ckv_s, kpe_s, sre_s, sim_s, hg_s = _run_sample(
        x_sample.reshape(bs, D_MODEL), grp_s, w, p, rope_s, cache_kv_latent, cache_k_rope, page_table,
        state_ssm_re, state_ssm_im, state_hgrn)
    return (y_p, y_s, ckv_p, ckv_s, kpe_p, kpe_s, sre_p, sre_s, sim_p, sim_s, hg_p, hg_s)
```

```python
import functools
import math

import numpy as np
import jax
import jax.numpy as jnp
from jax import lax
from jax.experimental import pallas as pl
from jax.experimental.pallas import tpu as pltpu

F32 = jnp.float32
BF16 = jnp.bfloat16

D_MODEL = 1024
D_FF = 2816
SSM_WIDTH = 256
SSM_GROUP = 16
SSM_GROUPS = 16
SSM_STATE = 64
HG_WIDTH = 256
HG_HEADS = 4
HG_DK = 64
LB_FLOOR = 1e-30
MLA_HEADS = 8
MLA_V_DIM = 64
MLA_NOPE_DIM = 64
MLA_ROPE_DIM = 32
MLA_Q_LORA = 384
MLA_KV_LORA = 256
MLA_SCALE = (MLA_NOPE_DIM + MLA_ROPE_DIM) ** -0.5
ROPE_THETA = 10000.0
NEG_MASK = -1e30
PAGE_SIZE = 128

LANES = 128
HEAD_PAD = LANES
MLA_PAD = MLA_HEADS * HEAD_PAD
N_IN_PAD = 2048
ROPE_LANE0 = MLA_NOPE_DIM
S5_CHUNK = 16
HG_GROUP = 16
VMEM_LIMIT = 56 * 1024 * 1024
DEPTH_ = 4
ALPHA = (2 * DEPTH_) ** 0.25


def _cparams(sem):
    return pltpu.CompilerParams(dimension_semantics=sem, vmem_limit_bytes=VMEM_LIMIT)


def _dot(a, b):
    return jnp.dot(a, b, preferred_element_type=F32)


def _dot_nt(a, b):
    return lax.dot_general(a, b, (((1,), (1,)), ((), ())), preferred_element_type=F32)


def _dot_tn(a, b):
    return lax.dot_general(a, b, (((0,), (0,)), ((), ())), preferred_element_type=F32)


def _split_dot(w, x, terms):
    acc = None
    r = x
    for _ in range(terms):
        p = r.astype(BF16)
        d = _dot(w, p)
        acc = d if acc is None else acc + d
        r = r - p.astype(F32)
    return acc


def _split_dot_r(x, w, terms):
    acc = None
    r = x
    for _ in range(terms):
        p = r.astype(BF16)
        d = _dot(p, w)
        acc = d if acc is None else acc + d
        r = r - p.astype(F32)
    return acc


def _layer_norm(y, g, b):
    mu = jnp.mean(y, -1, keepdims=True)
    d = y - mu
    var = jnp.mean(d * d, -1, keepdims=True)
    return d * lax.rsqrt(var + 1e-5) * g + b


def _sigmoid(x):
    return 1.0 / (1.0 + jnp.exp(-x))


def _silu(x):
    return x * _sigmoid(x)


def _gelu_tanh(x):
    c = math.sqrt(2.0 / math.pi)
    return 0.5 * x * (1.0 + jnp.tanh(c * (x + 0.044715 * (x * x * x))))


def _ada_kernel(c_ref, w_ref, b_ref, o_ref):
    c = c_ref[...]
    s = _silu(c).astype(BF16)
    o_ref[...] = _dot(s, w_ref[...].astype(BF16)) + b_ref[...]


def _ada_call(c_all, w_ada, b_ada):
    rows = c_all.shape[0]
    depth, d, n = w_ada.shape
    tn = 1152
    return pl.pallas_call(
        _ada_kernel,
        grid=(depth, n // tn),
        in_specs=[pl.BlockSpec((rows, d), lambda l, j: (0, 0)),
                  pl.BlockSpec((None, d, tn), lambda l, j: (l, 0, j)),
                  pl.BlockSpec((None, 1, tn), lambda l, j: (l, 0, j))],
        out_specs=pl.BlockSpec((None, rows, tn), lambda l, j: (l, 0, j)),
        out_shape=jax.ShapeDtypeStruct((depth, rows, n), F32),
        compiler_params=_cparams(("parallel", "parallel")),
        name="ada",
    )(c_all, w_ada, b_ada.reshape(depth, 1, n))


class _Group:
    def __init__(self, mod, batch, seq, tm):
        self.batch, self.seq, self.tm = batch, seq, tm
        self.rows = batch * seq
        self.per_row = seq == 1
        depth = mod.shape[0]
        if self.per_row:
            self.mod = jnp.transpose(mod, (0, 2, 1, 3))
        else:
            self.mod = mod.reshape(depth, batch, 9, 1, D_MODEL)
            self.tiles_per_batch = seq // tm

    def mod_spec(self, l, j):
        if self.per_row:
            return pl.BlockSpec((None, None, self.tm, D_MODEL), lambda i: (l, j, i, 0))
        tpb = self.tiles_per_batch
        return pl.BlockSpec((None, None, None, 1, D_MODEL), lambda i: (l, i // tpb, j, 0, 0))

    def pos_spec(self, width):
        if self.per_row:
            return pl.BlockSpec((1, width), lambda i: (0, 0))
        tpb = self.tiles_per_batch
        return pl.BlockSpec((self.tm, width), lambda i: (i % tpb, 0))

    def row_spec(self, width, col=0):
        return pl.BlockSpec((self.tm, width), lambda i: (i, col))

    def tcol_shape(self, height):
        return (1, height, self.batch) if self.per_row else (self.batch, height, self.seq)

    def tcol_spec(self, height):
        if self.per_row:
            return pl.BlockSpec((None, height, self.tm), lambda i: (0, 0, 0))
        tpb = self.tiles_per_batch
        return pl.BlockSpec((None, height, self.tm), lambda i: (i // tpb, 0, i % tpb))

    @property
    def grid(self):
        return (self.rows // self.tm,)


def _const_spec(shape, index):
    return pl.BlockSpec(shape, lambda i: index)


FF_CHUNK = 256


def _ffn_kernel(x_ref, sh_ref, sc_ref, gt_ref, wgu_ref, wd_ref, lg_ref, lb_ref, o_ref, a_ref):
    x = x_ref[...]
    h = (x * (1.0 + sc_ref[...]) + sh_ref[...]).astype(BF16)
    for c in range(D_FF // FF_CHUNK):
        lo = c * FF_CHUNK
        g = _dot(h, wgu_ref[:, lo:lo + FF_CHUNK])
        u = _dot(h, wgu_ref[:, D_FF + lo:D_FF + lo + FF_CHUNK])
        a_ref[:, lo:lo + FF_CHUNK] = (_silu(g) * u).astype(BF16)
    f = _dot(a_ref[...], wd_ref[...])
    y = ALPHA * x + 0.5 * (1.0 + gt_ref[...]) * f
    o_ref[...] = _layer_norm(y, lg_ref[...], lb_ref[...])


def _ffn_call(grp, x, l, which, mod_idx, wgu, wd, ln_g, ln_b):
    tm = grp.tm
    return pl.pallas_call(
        _ffn_kernel,
        grid=grp.grid,
        in_specs=[grp.row_spec(D_MODEL),
                  grp.mod_spec(l, 3 * mod_idx), grp.mod_spec(l, 3 * mod_idx + 1), grp.mod_spec(l, 3 * mod_idx + 2),
                  _const_spec((None, None, D_MODEL, 2 * D_FF), (l, which, 0, 0)),
                  _const_spec((None, None, D_FF, D_MODEL), (l, which, 0, 0)),
                  _const_spec((None, None, 1, D_MODEL), (l, mod_idx, 0, 0)),
                  _const_spec((None, None, 1, D_MODEL), (l, mod_idx, 0, 0))],
        out_specs=grp.row_spec(D_MODEL),
        out_shape=jax.ShapeDtypeStruct((grp.rows, D_MODEL), F32),
        scratch_shapes=[pltpu.VMEM((tm, D_FF), BF16)],
        compiler_params=_cparams(("parallel",)),
        name="ffn",
    )(x, grp.mod, grp.mod, grp.mod, wgu, wd, ln_g, ln_b)


Z_U = (0, 256)
Z_H = (256, 1280)
Z_CQ = (1280, 1664)
Z_CKV = (1664, 1920)
Z_KPE = (1920, 2048)


def _rope_group(t, c, s1, s2):
    return t * c + pltpu.roll(t, LANES - MLA_ROPE_DIM // 2, 1) * s1 + pltpu.roll(t, MLA_ROPE_DIM // 2, 1) * s2


def _mix_in_kernel(x_ref, sh_ref, sc_ref, win_ref, qn_ref, wuq_ref, kvn_ref, wuk_ref, wuv_ref,
                   rc_ref, rs1_ref, rs2_ref,
                   u_ref, zh_ref, qh_ref, kh_ref, vt_ref, ckv_ref, kpe_ref):
    x = x_ref[...]
    h = (x * (1.0 + sc_ref[...]) + sh_ref[...]).astype(BF16)
    u_ref[...] = _dot(h, win_ref[:, Z_U[0]:Z_U[1]])
    zh_ref[...] = _dot(h, win_ref[:, Z_H[0]:Z_H[1]])
    rc, rs1, rs2 = rc_ref[...], rs1_ref[...], rs2_ref[...]

    zcq = _dot(h, win_ref[:, Z_CQ[0]:Z_CQ[1]])
    cq = zcq * lax.rsqrt(jnp.mean(zcq * zcq, -1, keepdims=True) + 1e-6) * qn_ref[...]
    q = _dot(cq.astype(BF16), wuq_ref[...]) * MLA_SCALE
    for hd in range(MLA_HEADS):
        lo = hd * HEAD_PAD
        qh_ref[:, lo:lo + HEAD_PAD] = _rope_group(q[:, lo:lo + HEAD_PAD], rc, rs1, rs2).astype(BF16)

    zckv = _dot(h, win_ref[:, Z_CKV[0]:Z_CKV[1]])
    ckv = zckv * lax.rsqrt(jnp.mean(zckv * zckv, -1, keepdims=True) + 1e-6) * kvn_ref[...]
    ckv_ref[...] = ckv
    ckv_b = ckv.astype(BF16)
    kpe = _rope_group(_dot(h, win_ref[:, Z_KPE[0]:Z_KPE[1]]), rc, rs1, rs2)
    kpe_ref[...] = pltpu.roll(kpe, LANES - ROPE_LANE0, 1)[:, :MLA_ROPE_DIM]
    kn = _dot(ckv_b, wuk_ref[...])
    for hd in range(MLA_HEADS):
        lo = hd * HEAD_PAD
        kh_ref[:, lo:lo + HEAD_PAD] = (kn[:, lo:lo + HEAD_PAD] + kpe).astype(BF16)
    vt_ref[...] = _dot_nt(wuv_ref[...], ckv_b).astype(BF16)


def _mix_in_call(grp, x, l, w, rope):
    rows = grp.rows
    v_rows = MLA_HEADS * MLA_V_DIM
    outs = [((rows, SSM_WIDTH), F32), ((rows, 4 * HG_WIDTH), F32), ((rows, MLA_PAD), BF16),
            ((rows, MLA_PAD), BF16), (grp.tcol_shape(v_rows), BF16), ((rows, MLA_KV_LORA), F32),
            ((rows, MLA_ROPE_DIM), F32)]
    out_specs = [grp.row_spec(s[1]) for s, _ in outs]
    out_specs[4] = grp.tcol_spec(v_rows)
    return pl.pallas_call(
        _mix_in_kernel,
        grid=grp.grid,
        in_specs=[grp.row_spec(D_MODEL), grp.mod_spec(l, 3), grp.mod_spec(l, 4),
                  _const_spec((None, D_MODEL, N_IN_PAD), (l, 0, 0)),
                  _const_spec((None, 1, MLA_Q_LORA), (l, 0, 0)),
                  _const_spec((None, MLA_Q_LORA, MLA_PAD), (l, 0, 0)),
                  _const_spec((None, 1, MLA_KV_LORA), (l, 0, 0)),
                  _const_spec((None, MLA_KV_LORA, MLA_PAD), (l, 0, 0)),
                  _const_spec((None, v_rows, MLA_KV_LORA), (l, 0, 0)),
                  grp.pos_spec(LANES), grp.pos_spec(LANES), grp.pos_spec(LANES)],
        out_specs=out_specs,
        out_shape=[jax.ShapeDtypeStruct(s, d) for s, d in outs],
        compiler_params=_cparams(("parallel",)),
        name="mix_in",
    )(x, grp.mod, grp.mod, w["w_in"], w["q_norm"], w["w_uq"], w["kv_norm"], w["w_uk"], w["w_uv_t"], *rope)


def _s5_kernel(u_ref, m_ref, wis_ref, wso_ref, av_ref, d_ref, x0_ref, y_ref, xf_ref, sv_scr, sw_scr, x_scr,
               *, nb, nc):
    u = u_ref[...]
    ub = u.astype(BF16)
    sv_scr[...] = _dot(ub, wis_ref[:, 0:LANES])
    sw_scr[...] = _dot(ub, wis_ref[:, LANES:2 * LANES])
    a1, a2, a3 = av_ref[0:1, :], av_ref[1:2, :], av_ref[2:3, :]

    def body(c, carry):
        v, w = carry
        rows_c = pl.ds(c, nb, stride=nc)
        x_scr[rows_c, :] = v
        return a1 * v + a2 * w + sv_scr[rows_c, :], a1 * w + a3 * v + sw_scr[rows_c, :]

    v, _ = lax.fori_loop(0, nc, body, (x0_ref[:, 0:LANES], x0_ref[:, LANES:2 * LANES]), unroll=8)
    xf_ref[...] = v
    y_ref[...] = _dot(ub, m_ref[...]) + _dot(x_scr[...].astype(BF16), wso_ref[...]) + u * d_ref[...]


def _s5_pack_kernel(u_ref, sel_ref, o_ref, lo_scr, hi_scr, *, nch):
    g = SSM_GROUPS
    lo_scr[...] = u_ref[:, 0:LANES]
    hi_scr[...] = u_ref[:, LANES:2 * LANES]
    lane_grp = lax.broadcasted_iota(jnp.int32, (g * nch, SSM_WIDTH), 1) // SSM_GROUP
    row_grp = lax.broadcasted_iota(jnp.int32, (g * nch, SSM_WIDTH), 0) // nch
    keep = lane_grp == row_grp
    acc = jnp.zeros((g * nch, SSM_WIDTH), F32)
    for t in range(S5_CHUNK):
        step_rows = pl.ds(t, nch, stride=S5_CHUNK)
        rows = jnp.concatenate([lo_scr[step_rows, :], hi_scr[step_rows, :]], 1)
        lhs = jnp.where(keep, jnp.tile(rows, (g, 1)), 0.0).astype(BF16)
        acc = acc + _dot(lhs, sel_ref[t])
    o_ref[...] = acc.reshape(g, nch, SSM_WIDTH).astype(o_ref.dtype)


def _s5_unpack_kernel(y_ref, sel_ref, o_ref, lo_scr, hi_scr, *, nch):
    g = SSM_GROUPS
    lane_grp = lax.broadcasted_iota(jnp.int32, (g * nch, SSM_WIDTH), 1) // SSM_GROUP
    row_grp = lax.broadcasted_iota(jnp.int32, (g * nch, SSM_WIDTH), 0) // nch
    keep = lane_grp == row_grp
    yb = y_ref[...].reshape(g * nch, SSM_WIDTH).astype(BF16)
    for t in range(S5_CHUNK):
        z = jnp.where(keep, _dot(yb, sel_ref[t]), 0.0)
        tok = jnp.sum(z.reshape(g, nch, SSM_WIDTH), 0)
        step_rows = pl.ds(t, nch, stride=S5_CHUNK)
        lo_scr[step_rows, :] = tok[:, 0:LANES]
        hi_scr[step_rows, :] = tok[:, LANES:2 * LANES]
    o_ref[...] = jnp.concatenate([lo_scr[...], hi_scr[...]], 1)


def _s5_relayout_call(kern, x, sel, grp, to_groups):
    nch = grp.tm // S5_CHUNK
    tpb = grp.tiles_per_batch
    tok_spec = grp.row_spec(SSM_WIDTH)
    grp_spec = pl.BlockSpec((SSM_GROUPS, None, nch, SSM_WIDTH), lambda i: (0, i // tpb, i % tpb, 0))
    grp_shape = (SSM_GROUPS, grp.batch, grp.seq // S5_CHUNK, SSM_WIDTH)
    return pl.pallas_call(
        functools.partial(kern, nch=nch),
        grid=grp.grid,
        in_specs=[tok_spec if to_groups else grp_spec,
                  _const_spec((S5_CHUNK, SSM_WIDTH, SSM_WIDTH), (0, 0, 0))],
        out_specs=grp_spec if to_groups else tok_spec,
        out_shape=jax.ShapeDtypeStruct(grp_shape, BF16) if to_groups
        else jax.ShapeDtypeStruct((grp.rows, SSM_WIDTH), F32),
        scratch_shapes=[pltpu.VMEM((grp.tm, LANES), F32)] * 2,
        compiler_params=_cparams(("parallel",)),
        name="s5_pack" if to_groups else "s5_unpack",
    )(x, sel)


def _s5_call(u_g, x0, w, l, nb, nc):
    g = SSM_GROUPS
    rows = nb * nc
    wide = S5_CHUNK * SSM_GROUP
    u_g = u_g.reshape(g, rows, wide)
    return pl.pallas_call(
        functools.partial(_s5_kernel, nb=nb, nc=nc),
        grid=(g,),
        in_specs=[pl.BlockSpec((None, rows, wide), lambda i: (i, 0, 0)),
                  pl.BlockSpec((None, None, wide, wide), lambda i: (l, i, 0, 0)),
                  pl.BlockSpec((None, None, wide, wide), lambda i: (l, i, 0, 0)),
                  pl.BlockSpec((None, None, LANES, wide), lambda i: (l, i, 0, 0)),
                  pl.BlockSpec((None, None, 8, LANES), lambda i: (l, i, 0, 0)),
                  pl.BlockSpec((None, None, 1, wide), lambda i: (l, i, 0, 0)),
                  pl.BlockSpec((None, nb, wide), lambda i: (i, 0, 0))],
        out_specs=[pl.BlockSpec((None, rows, wide), lambda i: (i, 0, 0)),
                   pl.BlockSpec((None, nb, LANES), lambda i: (i, 0, 0))],
        out_shape=[jax.ShapeDtypeStruct((g, rows, wide), F32), jax.ShapeDtypeStruct((g, nb, LANES), F32)],
        scratch_shapes=[pltpu.VMEM((rows, LANES), F32)] * 3,
        compiler_params=_cparams(("parallel",)),
        name="s5_chunks",
    )(u_g, w["s5_m"], w["s5_wis"], w["s5_wso"], w["s5_av"], w["s5_d"], x0)


def _s5_step_kernel(u_ref, xr_ref, xi_ref, bb_ref, cc_ref, ar_ref, ai_ref, d_ref, y_ref, or_ref, oi_ref):
    u = u_ref[...]
    n = SSM_GROUPS * SSM_STATE
    bu = _dot(u.astype(BF16), bb_ref[...])
    xr, xi, ar, ai = xr_ref[...], xi_ref[...], ar_ref[...], ai_ref[...]
    nr = ar * xr - ai * xi + bu[:, :n]
    ni = ar * xi + ai * xr + bu[:, n:]
    or_ref[...] = nr
    oi_ref[...] = ni
    y_ref[...] = (_dot(nr.astype(BF16), cc_ref[0:n, :]) + _dot(ni.astype(BF16), cc_ref[n:2 * n, :])
                  + u * d_ref[...])


def _s5_step_call(u, xr, xi, w, l):
    b = u.shape[0]
    n = SSM_GROUPS * SSM_STATE
    full = lambda shape: pl.BlockSpec(shape, lambda i: (0,) * len(shape))
    lsel = lambda shape: pl.BlockSpec((None,) + shape, lambda i: (l,) + (0,) * len(shape))
    return pl.pallas_call(
        _s5_step_kernel,
        grid=(1,),
        in_specs=[full((b, SSM_WIDTH)), full((b, n)), full((b, n)),
                  lsel((SSM_WIDTH, 2 * n)), lsel((2 * n, SSM_WIDTH)), lsel((1, n)), lsel((1, n)),
                  lsel((1, SSM_WIDTH))],
        out_specs=[full((b, SSM_WIDTH)), full((b, n)), full((b, n))],
        out_shape=[jax.ShapeDtypeStruct((b, SSM_WIDTH), F32), jax.ShapeDtypeStruct((b, n), F32),
                   jax.ShapeDtypeStruct((b, n), F32)],
        compiler_params=_cparams(("arbitrary",)),
        name="s5_step",
    )(u, xr, xi, w["s5_bb"], w["s5_cc"], w["s5_ar"], w["s5_ai"], w["s5_dflat"])


def _hgrn_kernel(zh_ref, s0_ref, lbp_ref, e_ref, bd_ref, lcum_ref, lall_ref, o_ref, sf_ref,
                 perm_scr, acc_scr, st_ref, oi_ref, *, tt, n_valid):
    j = pl.program_id(1)

    @pl.when(j == 0)
    def _():
        st_ref[...] = s0_ref[...]

    w = HG_WIDTH
    ng = tt // HG_GROUP
    zf = zh_ref[:, 0:w]
    q = zh_ref[:, w:2 * w]
    v = zh_ref[:, 2 * w:3 * w]
    lbm, oml = lbp_ref[0:1, :], lbp_ref[1:2, :]
    sig = 1.0 / (1.0 + jnp.exp(-zf))
    f = lbm + oml * sig
    k = oml * (1.0 - sig)
    if n_valid < tt:
        live = lax.broadcasted_iota(jnp.int32, (tt, w), 0) < n_valid
        f = jnp.where(live, f, 1.0)
        k = jnp.where(live, k, 0.0)
    lf = jnp.log(f)

    def regroup(i, x):
        if ng == 1:
            return x
        perm_scr[2 * i] = x[:, 0:LANES]
        perm_scr[2 * i + 1] = x[:, LANES:2 * LANES]
        blocks = []
        for pos in range(HG_GROUP):
            rows = pl.ds(pos, ng, stride=HG_GROUP)
            blocks.append(jnp.concatenate([perm_scr[2 * i, rows, :], perm_scr[2 * i + 1, rows, :]], 1))
        return jnp.concatenate(blocks, 0)

    qg, kg, vg, fg = regroup(0, q), regroup(1, k), regroup(2, v), regroup(3, f)
    e = e_ref[...]
    acc_scr[...] = _dot((qg * kg).astype(BF16), e) * vg
    decay = None
    for d in range(1, HG_GROUP):
        n_rows = (HG_GROUP - d) * ng
        f_blk = fg[ng:ng + n_rows]
        decay = f_blk if decay is None else decay[ng:] * f_blk
        p = qg[d * ng:] * kg[:n_rows] * decay
        acc_scr[d * ng:, :] += _dot(p.astype(BF16), e) * vg[:n_rows]
    if ng == 1:
        acc = acc_scr[...]
    else:
        for pos in range(HG_GROUP):
            rows = pl.ds(pos, ng, stride=HG_GROUP)
            perm_scr[0, rows, :] = acc_scr[pos * ng:(pos + 1) * ng, 0:LANES]
            perm_scr[1, rows, :] = acc_scr[pos * ng:(pos + 1) * ng, LANES:2 * LANES]
        acc = jnp.concatenate([perm_scr[0], perm_scr[1]], 1)

    b = _split_dot(lcum_ref[...], lf, 3)
    bl = _split_dot(lall_ref[...], lf, 3)
    qe = (q * jnp.exp(b)).astype(BF16)
    kx = (k * jnp.exp(bl - b)).astype(BF16)
    ebl = jnp.exp(bl)
    vb = v.astype(BF16)
    bd = bd_ref[...]
    for g in range(tt // HG_GROUP):
        lo = g * HG_GROUP
        st = st_ref[...]
        oi_ref[lo:lo + HG_GROUP, :] = _dot_nt(qe[lo:lo + HG_GROUP], st.astype(BF16))
        upd = _dot_tn(vb[lo:lo + HG_GROUP], kx[lo:lo + HG_GROUP])
        st_ref[...] = st * ebl[lo:lo + 1, :] + upd * bd
    o_ref[...] = acc + oi_ref[...]

    @pl.when(j == pl.num_programs(1) - 1)
    def _():
        sf_ref[...] = st_ref[...]


def _hgrn_call(zh, s0t, w, l, batch, seq_rows, tt, n_valid):
    wd = HG_WIDTH
    nj = seq_rows // tt
    return pl.pallas_call(
        functools.partial(_hgrn_kernel, tt=tt, n_valid=n_valid),
        grid=(batch, nj),
        in_specs=[pl.BlockSpec((tt, 4 * wd), lambda b, j: (b * nj + j, 0)),
                  pl.BlockSpec((None, wd, wd), lambda b, j: (b, 0, 0)),
                  pl.BlockSpec((None, 8, wd), lambda b, j: (l, 0, 0)),
                  pl.BlockSpec((wd, wd), lambda b, j: (0, 0)),
                  pl.BlockSpec((wd, wd), lambda b, j: (0, 0)),
                  pl.BlockSpec((tt, tt), lambda b, j: (0, 0)),
                  pl.BlockSpec((tt, tt), lambda b, j: (0, 0))],
        out_specs=[pl.BlockSpec((tt, wd), lambda b, j: (b * nj + j, 0)),
                   pl.BlockSpec((None, wd, wd), lambda b, j: (b, 0, 0))],
        out_shape=[jax.ShapeDtypeStruct((batch * seq_rows, wd), F32),
                   jax.ShapeDtypeStruct((batch, wd, wd), F32)],
        scratch_shapes=[pltpu.VMEM((8, tt, LANES), F32), pltpu.VMEM((tt, wd), F32),
                        pltpu.VMEM((wd, wd), F32), pltpu.VMEM((tt, wd), F32)],
        compiler_params=_cparams(("parallel", "arbitrary")),
        name="hgrn",
    )(zh, s0t, w["hg_lbp"], w["hg_e"], w["hg_bd"], w["hg_lcum"][tt], w["hg_lall"][tt])


def _flash_kernel(q_ref, k_ref, vt_ref, o_ref, m_ref, l_ref, acc_ref):
    qi = pl.program_id(1)
    ki = pl.program_id(2)

    @pl.when(ki == 0)
    def _():
        m_ref[...] = jnp.full(m_ref.shape, NEG_MASK, F32)
        l_ref[...] = jnp.zeros(l_ref.shape, F32)
        acc_ref[...] = jnp.zeros(acc_ref.shape, F32)

    tq = q_ref.shape[0]

    def step(keys, qrys, diagonal):
        for hd in range(MLA_HEADS):
            sl = slice(hd * HEAD_PAD, (hd + 1) * HEAD_PAD)
            vs = slice(hd * MLA_V_DIM, (hd + 1) * MLA_V_DIM)
            st = _dot_nt(k_ref[keys, sl], q_ref[qrys, sl])
            if diagonal:
                key = lax.broadcasted_iota(jnp.int32, st.shape, 0)
                qry = lax.broadcasted_iota(jnp.int32, st.shape, 1)
                st = jnp.where(key <= qry, st, NEG_MASK)
            m_prev = m_ref[hd:hd + 1, qrys]
            m_new = jnp.maximum(m_prev, jnp.max(st, 0, keepdims=True))
            alpha = jnp.exp(m_prev - m_new)
            p = jnp.exp(st - m_new)
            l_ref[hd:hd + 1, qrys] = alpha * l_ref[hd:hd + 1, qrys] + jnp.sum(p, 0, keepdims=True)
            acc_ref[vs, qrys] = alpha * acc_ref[vs, qrys] + _dot(vt_ref[vs, keys], p.astype(BF16))
            m_ref[hd:hd + 1, qrys] = m_new

    @pl.when(ki < qi)
    def _():
        step(slice(0, tq), slice(0, tq), False)

    @pl.when(ki == qi)
    def _():
        step(slice(0, tq), slice(0, tq), True)
        for hd in range(MLA_HEADS):
            vs = slice(hd * MLA_V_DIM, (hd + 1) * MLA_V_DIM)
            o_ref[vs, :] = acc_ref[vs, :] / l_ref[hd:hd + 1, :]


def _flash_call(qh, kh, vt, batch, seq, tq):
    nq = seq // tq
    v_rows = MLA_HEADS * MLA_V_DIM
    q_spec = pl.BlockSpec((None, tq, MLA_PAD), lambda b, qi, ki: (b, qi, 0))
    k_spec = pl.BlockSpec((None, tq, MLA_PAD), lambda b, qi, ki: (b, jnp.minimum(ki, qi), 0))
    vt_spec = pl.BlockSpec((None, v_rows, tq), lambda b, qi, ki: (b, 0, jnp.minimum(ki, qi)))
    shp = (batch, seq, MLA_PAD)
    return pl.pallas_call(
        _flash_kernel,
        grid=(batch, nq, nq),
        in_specs=[q_spec, k_spec, vt_spec],
        out_specs=pl.BlockSpec((None, v_rows, tq), lambda b, qi, ki: (b, 0, qi)),
        out_shape=jax.ShapeDtypeStruct((batch, v_rows, seq), F32),
        scratch_shapes=[pltpu.VMEM((MLA_HEADS, tq), F32), pltpu.VMEM((MLA_HEADS, tq), F32),
                        pltpu.VMEM((v_rows, tq), F32)],
        compiler_params=_cparams(("parallel", "parallel", "arbitrary")),
        name="flash",
    )(qh.reshape(shp), kh.reshape(shp), vt)


PAGES_PER_STEP = 64


def _decode_kernel(pt_ref, qh_ref, ckvn_ref, kpen_ref, wuk_ref, wuv_ref, sel_ref, hm_ref, ckv_hbm, kpe_hbm,
                   o_ref, ckv_buf, kpe_buf, sem_c, sem_k, *, layer, npg, ng):
    b = pl.program_id(0)
    hm = hm_ref[...]

    def copies(seq_i, grp_i, slot):
        out = []
        for i in range(npg):
            page = pt_ref[seq_i, grp_i * npg + i]
            out.append(pltpu.make_async_copy(ckv_hbm.at[layer, page], ckv_buf.at[slot, i], sem_c.at[slot]))
            out.append(pltpu.make_async_copy(kpe_hbm.at[layer, page], kpe_buf.at[slot, i], sem_k.at[slot]))
        return out

    @pl.when(b == 0)
    def _():
        for c in copies(0, 0, 0):
            c.start()

    qbd = jnp.where(hm > 0, jnp.broadcast_to(qh_ref[...], hm.shape), jnp.zeros_like(hm)).astype(BF16)
    qa = _dot_nt(qbd, wuk_ref[...]).astype(BF16)
    qp = _dot(qbd, sel_ref[...]).astype(BF16)
    m = jnp.full((MLA_HEADS, 1), NEG_MASK, F32)
    l = jnp.zeros((MLA_HEADS, 1), F32)
    acc = jnp.zeros((MLA_HEADS, MLA_KV_LORA), F32)
    for j in range(ng):
        slot = lax.rem(b * ng + j, 2)
        if j + 1 < ng:
            for c in copies(b, j + 1, 1 - slot):
                c.start()
        else:
            @pl.when(b + 1 < pl.num_programs(0))
            def _():
                for c in copies(b + 1, 0, 1 - slot):
                    c.start()
        for c in copies(b, j, slot):
            c.wait()
        pages = [ckv_buf[slot, i].astype(BF16) for i in range(npg)]
        s = jnp.concatenate(
            [_dot_nt(qa, pg) + _dot(qp, kpe_buf[slot, i].astype(BF16)) for i, pg in enumerate(pages)], axis=1)
        m_new = jnp.maximum(m, jnp.max(s, -1, keepdims=True))
        alpha = jnp.exp(m - m_new)
        p = jnp.exp(s - m_new)
        l = alpha * l + jnp.sum(p, -1, keepdims=True)
        pb = p.astype(BF16)
        parts = [None, None]
        for i, pg in enumerate(pages):
            d = _dot(pb[:, i * PAGE_SIZE:(i + 1) * PAGE_SIZE], pg)
            parts[i % 2] = d if parts[i % 2] is None else parts[i % 2] + d
        acc = alpha * acc + (parts[0] if parts[1] is None else parts[0] + parts[1])
        m = m_new

    ckvn = ckvn_ref[...]
    s_new = (jnp.sum(qa.astype(F32) * ckvn.astype(BF16).astype(F32), -1, keepdims=True)
             + jnp.sum(qp.astype(F32) * kpen_ref[...].astype(BF16).astype(F32), -1, keepdims=True))
    m_fin = jnp.maximum(m, s_new)
    a = jnp.exp(m - m_fin)
    p_new = jnp.exp(s_new - m_fin)
    l_fin = a * l + p_new
    lat = a * acc + p_new.astype(BF16).astype(F32) * ckvn.astype(BF16).astype(F32)
    lat = lat / l_fin
    o = _dot(lat.astype(BF16), wuv_ref[...])
    o_ref[...] = jnp.sum(jnp.where(hm > 0, o, 0.0), 0, keepdims=True)


def _decode_call(page_table, qh, ckv_new, kpe_new, cache_ckv, cache_kpe, w, l, sel, hm):
    batch, n_pages = page_table.shape
    npg = min(PAGES_PER_STEP, n_pages)
    ng = n_pages // npg
    row3 = lambda width: pl.BlockSpec((None, 1, width), lambda b, pt: (b, 0, 0))
    wsel = lambda shape: pl.BlockSpec((None,) + shape, lambda b, pt: (l,) + (0,) * len(shape))
    full = lambda shape: pl.BlockSpec(shape, lambda b, pt: (0,) * len(shape))
    hbm = pl.BlockSpec(memory_space=pl.ANY)
    grid_spec = pltpu.PrefetchScalarGridSpec(
        num_scalar_prefetch=1,
        grid=(batch,),
        in_specs=[row3(MLA_PAD), row3(MLA_KV_LORA), row3(MLA_ROPE_DIM),
                  wsel((MLA_KV_LORA, MLA_PAD)), wsel((MLA_KV_LORA, MLA_PAD)),
                  full((MLA_PAD, MLA_ROPE_DIM)), full((MLA_HEADS, MLA_PAD)), hbm, hbm],
        out_specs=row3(MLA_PAD),
        scratch_shapes=[pltpu.VMEM((2, npg, PAGE_SIZE, MLA_KV_LORA), cache_ckv.dtype),
                        pltpu.VMEM((2, npg, MLA_ROPE_DIM, PAGE_SIZE), cache_kpe.dtype),
                        pltpu.SemaphoreType.DMA((2,)), pltpu.SemaphoreType.DMA((2,))],
    )
    out = pl.pallas_call(
        functools.partial(_decode_kernel, layer=l, npg=npg, ng=ng),
        grid_spec=grid_spec,
        out_shape=jax.ShapeDtypeStruct((batch, 1, MLA_PAD), F32),
        compiler_params=_cparams(("arbitrary",)),
        name="decode",
    )(page_table, qh.reshape(batch, 1, MLA_PAD), ckv_new.reshape(batch, 1, MLA_KV_LORA),
      kpe_new.reshape(batch, 1, MLA_ROPE_DIM), w["w_uk"], w["w_uv"], sel, hm, cache_ckv, cache_kpe)
    return out.reshape(batch, MLA_PAD)


def _mix_out_kernel(x_ref, gt_ref, y_ref, zg_ref, oh_ref, om_ref, wglu_ref, nssm_ref, nhg_ref, nmla_ref,
                    e64_ref, woa_ref, wob_ref, lg_ref, lb_ref, o_ref, cat_ref):
    w = SSM_WIDTH
    gab = _dot(_gelu_tanh(y_ref[...]).astype(BF16), wglu_ref[...])
    t = gab[:, :w] * _sigmoid(gab[:, w:])
    cat_ref[:, 0:w] = (t * lax.rsqrt(jnp.mean(t * t, -1, keepdims=True) + 1e-6) * nssm_ref[...]).astype(BF16)

    oh = oh_ref[...]
    msq = _split_dot_r(oh * oh, e64_ref[...], 2)
    cat_ref[:, w:2 * w] = (oh * lax.rsqrt(msq + 1e-6) * nhg_ref[...] * _silu(zg_ref[...])).astype(BF16)

    om = om_ref[...]
    tm = om.shape[1]
    nm = nmla_ref[...]
    nm = nm[:, :tm] if tm <= LANES else jnp.tile(nm, (1, tm // LANES))
    ms = jnp.mean(om * om, 0, keepdims=True)
    o_mla = (om * lax.rsqrt(ms + 1e-6) * nm).astype(BF16)

    mix = _dot(cat_ref[...], woa_ref[...]) + _dot_tn(o_mla, wob_ref[...])
    y = ALPHA * x_ref[...] + (1.0 + gt_ref[...]) * mix
    o_ref[...] = _layer_norm(y, lg_ref[...], lb_ref[...])


def _mix_out_call(grp, x, l, y_ssm, zh, o_h, o_m, w, ln_g, ln_b):
    tm = grp.tm
    v_rows = MLA_HEADS * MLA_V_DIM
    return pl.pallas_call(
        _mix_out_kernel,
        grid=grp.grid,
        in_specs=[grp.row_spec(D_MODEL), grp.mod_spec(l, 5),
                  grp.row_spec(SSM_WIDTH), grp.row_spec(HG_WIDTH, col=3), grp.row_spec(HG_WIDTH),
                  grp.tcol_spec(v_rows),
                  _const_spec((None, SSM_WIDTH, 2 * SSM_WIDTH), (l, 0, 0)),
                  _const_spec((None, 1, SSM_WIDTH), (l, 0, 0)),
                  _const_spec((None, 1, HG_WIDTH), (l, 0, 0)),
                  _const_spec((None, v_rows, LANES), (l, 0, 0)),
                  _const_spec((HG_WIDTH, HG_WIDTH), (0, 0)),
                  _const_spec((None, SSM_WIDTH + HG_WIDTH, D_MODEL), (l, 0, 0)),
                  _const_spec((None, v_rows, D_MODEL), (l, 0, 0)),
                  _const_spec((None, None, 1, D_MODEL), (l, 1, 0, 0)),
                  _const_spec((None, None, 1, D_MODEL), (l, 1, 0, 0))],
        out_specs=grp.row_spec(D_MODEL),
        out_shape=jax.ShapeDtypeStruct((grp.rows, D_MODEL), F32),
        scratch_shapes=[pltpu.VMEM((tm, SSM_WIDTH + HG_WIDTH), BF16)],
        compiler_params=_cparams(("parallel",)),
        name="mix_out",
    )(x, grp.mod, y_ssm, zh, o_h, o_m, w["w_glu"], w["n_ssm"], w["n_hg"], w["n_mla"], w["hg_e64"],
      w["w_out_a"], w["w_out_b"], ln_g, ln_b)


def _pad_heads(wt, per_head):
    lead = wt.shape[:-1]
    wt = wt.reshape(lead + (MLA_HEADS, per_head))
    wt = jnp.pad(wt, [(0, 0)] * len(lead) + [(0, 0), (0, HEAD_PAD - per_head)])
    return wt.reshape(lead + (MLA_PAD,))


def _s5_tables(lam_re, lam_im, b_re, b_im, c_re, c_im, d, log_step):
    hp = lax.Precision.HIGHEST
    depth, g, p = lam_re.shape
    h, t = SSM_GROUP, S5_CHUNK
    step = jnp.exp(log_step)[..., None]
    mag = jnp.exp(lam_re * step)
    ab_re, ab_im = mag * jnp.cos(lam_im * step), mag * jnp.sin(lam_im * step)
    den = lam_re * lam_re + lam_im * lam_im
    nr = ab_re - 1.0
    coef_re = (nr * lam_re + ab_im * lam_im) / den
    coef_im = (ab_im * lam_re - nr * lam_im) / den
    bb_re = coef_re[..., None] * b_re - coef_im[..., None] * b_im
    bb_im = coef_re[..., None] * b_im + coef_im[..., None] * b_re
    n = jnp.arange(t + 1, dtype=F32)[:, None, None, None]
    pmag = jnp.exp(n * lam_re * step)
    pw_re, pw_im = pmag * jnp.cos(n * lam_im * step), pmag * jnp.sin(n * lam_im * step)
    pb_re = pw_re[:t, ..., None] * bb_re - pw_im[:t, ..., None] * bb_im
    pb_im = pw_re[:t, ..., None] * bb_im + pw_im[:t, ..., None] * bb_re
    kern = (jnp.einsum('lgop,nlgpi->nlgoi', c_re, pb_re, precision=hp)
            - jnp.einsum('lgop,nlgpi->nlgoi', c_im, pb_im, precision=hp))
    kern = jnp.concatenate([kern, jnp.zeros_like(kern[:1])], 0)
    s_idx, t_idx = np.arange(t)[:, None], np.arange(t)[None, :]
    lag = np.where(t_idx >= s_idx, t_idx - s_idx, t)
    m = kern[lag]
    m = jnp.transpose(m, (2, 3, 0, 5, 1, 4)).reshape(depth, g, t * h, t * h)
    wr = jnp.transpose(pb_re[::-1], (1, 2, 0, 4, 3)).reshape(depth, g, t * h, p)
    wi = jnp.transpose(pb_im[::-1], (1, 2, 0, 4, 3)).reshape(depth, g, t * h, p)
    wis = jnp.concatenate([wr, wi, wi, wr], -1)
    cp_re = c_re[None] * pw_re[1:, :, :, None, :] - c_im[None] * pw_im[1:, :, :, None, :]
    cp_im = c_re[None] * pw_im[1:, :, :, None, :] + c_im[None] * pw_re[1:, :, :, None, :]
    so_re = jnp.transpose(cp_re, (1, 2, 4, 0, 3)).reshape(depth, g, p, t * h)
    so_im = jnp.transpose(-cp_im, (1, 2, 4, 0, 3)).reshape(depth, g, p, t * h)
    wso = jnp.concatenate([so_re, so_im], 2)
    a_re, a_im = pw_re[t], pw_im[t]
    av = jnp.stack([jnp.concatenate([a_re, a_re], -1), jnp.concatenate([-a_im, a_im], -1),
                    jnp.concatenate([a_im, -a_im], -1)], 2)
    av = jnp.pad(av, ((0, 0), (0, 0), (0, 5), (0, 0)))
    dg = d.reshape(depth, g, 1, h)
    d_t = jnp.tile(dg, (1, 1, 1, t))
    eye = jnp.eye(g, dtype=F32)
    bbr = jnp.einsum('lgph,gk->lghkp', bb_re, eye).reshape(depth, g * h, g * p)
    bbi = jnp.einsum('lgph,gk->lghkp', bb_im, eye).reshape(depth, g * h, g * p)
    ccr = jnp.einsum('lghp,gk->lgpkh', c_re, eye).reshape(depth, g * p, g * h)
    cci = jnp.einsum('lghp,gk->lgpkh', -c_im, eye).reshape(depth, g * p, g * h)
    return dict(s5_m=m.astype(BF16), s5_wis=wis.astype(BF16), s5_wso=wso.astype(BF16), s5_av=av, s5_d=d_t,
                s5_bb=jnp.concatenate([bbr, bbi], -1).astype(BF16),
                s5_cc=jnp.concatenate([ccr, cci], 1).astype(BF16),
                s5_ar=ab_re.reshape(depth, 1, g * p), s5_ai=ab_im.reshape(depth, 1, g * p),
                s5_dflat=d.reshape(depth, 1, g * h))


def _rope_tables(pos):
    half = MLA_ROPE_DIM // 2
    inv = 1.0 / (ROPE_THETA ** (jnp.arange(0, MLA_ROPE_DIM, 2, dtype=F32) / MLA_ROPE_DIM))
    ang = pos.astype(F32)[:, None] * inv[None, :]
    cos, sin = jnp.cos(ang), jnp.sin(ang)
    n = pos.shape[0]
    one, zero = jnp.ones((n, ROPE_LANE0), F32), jnp.zeros((n, ROPE_LANE0), F32)
    z16, z32 = jnp.zeros((n, half), F32), jnp.zeros((n, LANES - ROPE_LANE0 - MLA_ROPE_DIM), F32)
    rc = jnp.concatenate([one, cos, cos, z32], 1)
    rs1 = jnp.concatenate([zero, -sin, z16, z32], 1)
    rs2 = jnp.concatenate([zero, z16, sin, z32], 1)
    return rc, rs1, rs2


def _group_patterns(tt):
    r = np.arange(tt)
    same = (r[:, None] // HG_GROUP) == (r[None, :] // HG_GROUP)
    lcum = same & (r[None, :] <= r[:, None])
    return jnp.asarray(lcum, BF16), jnp.asarray(same, BF16)


def _prepare(p):
    depth = p["w_in"].shape[0]
    w = {}
    w["ffn_w_gu"] = p["ffn_w_gu"].astype(BF16)
    w["ffn_w_down"] = p["ffn_w_down"].astype(BF16)
    w_in = p["w_in"]
    n_main = Z_CKV[1]
    zpad = lambda n: jnp.zeros((depth, D_MODEL, n), w_in.dtype)
    w["w_in"] = jnp.concatenate(
        [w_in[:, :, :n_main], zpad(ROPE_LANE0), w_in[:, :, n_main:], zpad(LANES - ROPE_LANE0 - MLA_ROPE_DIM)],
        -1).astype(BF16)
    w["w_uq"] = _pad_heads(p["mla_w_uq"], MLA_NOPE_DIM + MLA_ROPE_DIM).astype(BF16)
    w["w_uk"] = _pad_heads(p["mla_w_uk"], MLA_NOPE_DIM).astype(BF16)
    w["w_uv"] = _pad_heads(p["mla_w_uv"], MLA_V_DIM).astype(BF16)
    w["w_uv_t"] = jnp.swapaxes(p["mla_w_uv"], 1, 2).astype(BF16)
    w["q_norm"] = p["mla_q_norm"].reshape(depth, 1, MLA_Q_LORA)
    w["kv_norm"] = p["mla_kv_norm"].reshape(depth, 1, MLA_KV_LORA)
    w["w_glu"] = p["ssm_w_glu"].astype(BF16)
    w["n_ssm"] = p["norm_ssm"].reshape(depth, 1, SSM_WIDTH)
    w["n_hg"] = p["norm_hgrn"].reshape(depth, 1, HG_WIDTH)
    w["n_mla"] = jnp.broadcast_to(p["norm_mla"][:, :, None], (depth, MLA_HEADS * MLA_V_DIM, LANES))
    w_out = p["w_out"]
    split = SSM_WIDTH + HG_WIDTH
    w["w_out_a"] = w_out[:, :split].astype(BF16)
    w["w_out_b"] = w_out[:, split:].astype(BF16)
    w.update(_s5_tables(p["ssm_lambda_re"].astype(F32), p["ssm_lambda_im"].astype(F32),
                        p["ssm_b_re"].astype(F32), p["ssm_b_im"].astype(F32),
                        p["ssm_c_re"].astype(F32), p["ssm_c_im"].astype(F32),
                        p["ssm_d"].astype(F32), p["ssm_log_step"].astype(F32)))
    lane = np.arange(SSM_WIDTH)
    step_of, chan_of = lane // SSM_GROUP, lane % SSM_GROUP
    t_idx = np.arange(S5_CHUNK)[:, None, None]
    fwd = (step_of[None, None, :] == t_idx) & (chan_of[None, None, :] == chan_of[None, :, None])
    w["s5_sel_fwd"] = jnp.asarray(fwd, BF16)
    w["s5_sel_bwd"] = jnp.asarray(np.swapaxes(fwd, 1, 2), BF16)
    sm = jax.nn.softmax(p["hgrn_lb_logits"].astype(F32), axis=0)
    lb = jnp.clip(jnp.cumsum(sm, axis=0) - sm[0:1], 0.0, 1.0 - 1e-6)
    rows = jnp.stack([jnp.maximum(lb, LB_FLOOR), 1.0 - lb], 1)
    w["hg_lbp"] = jnp.pad(rows, ((0, 0), (0, 6), (0, 0)))
    head = np.arange(HG_WIDTH) // HG_DK
    same_head = head[:, None] == head[None, :]
    w["hg_e"] = jnp.asarray(same_head, BF16)
    w["hg_bd"] = jnp.asarray(same_head, F32)
    w["hg_e64"] = jnp.asarray(same_head / HG_DK, BF16)
    return w


def _hg_state_in(s):
    b = s.shape[0]
    eye = jnp.eye(HG_HEADS, dtype=s.dtype)
    return jnp.einsum('bhkv,hg->bhvgk', s, eye).reshape(b, HG_WIDTH, HG_WIDTH)


def _hg_state_out(st):
    b = st.shape[0]
    st = st.reshape(b, HG_HEADS, HG_DK, HG_HEADS, HG_DK)
    diag = jnp.stack([st[:, h, :, h, :] for h in range(HG_HEADS)], 1)
    return jnp.transpose(diag, (0, 1, 3, 2))


def _run_prompt(x, grp, w, p, rope):
    batch, seq = grp.batch, grp.seq
    depth = p["w_in"].shape[0]
    nc = seq // S5_CHUNK
    tt = min(256, seq)
    tq = min(1024, seq)
    ln_g = p["ln_g"].reshape(depth, 3, 1, D_MODEL)
    ln_b = p["ln_b"].reshape(depth, 3, 1, D_MODEL)
    x0 = jnp.zeros((SSM_GROUPS, batch, 2 * LANES), F32)
    s0 = jnp.zeros((batch, HG_WIDTH, HG_WIDTH), F32)
    ckv_l, kpe_l, sre_l, sim_l, hg_l = [], [], [], [], []
    for l in range(depth):
        x = _ffn_call(grp, x, l, 0, 0, w["ffn_w_gu"], w["ffn_w_down"], ln_g, ln_b)
        u, zh, qh, kh, vt, ckv, kpe = _mix_in_call(grp, x, l, w, rope)
        u_g = _s5_relayout_call(_s5_pack_kernel, u, w["s5_sel_fwd"], grp, True)
        y_g, xf = _s5_call(u_g, x0, w, l, batch, nc)
        y = _s5_relayout_call(_s5_unpack_kernel, y_g.reshape(u_g.shape), w["s5_sel_bwd"], grp, False)
        o_h, st = _hgrn_call(zh, s0, w, l, batch, seq, tt, tt)
        o_m = _flash_call(qh, kh, vt, batch, seq, tq)
        x = _mix_out_call(grp, x, l, y, zh, o_h, o_m, w, ln_g, ln_b)
        x = _ffn_call(grp, x, l, 1, 2, w["ffn_w_gu"], w["ffn_w_down"], ln_g, ln_b)
        ckv_l.append(ckv.reshape(batch, seq, MLA_KV_LORA))
        kpe_l.append(kpe.reshape(batch, seq, MLA_ROPE_DIM))
        xf = jnp.transpose(xf, (1, 0, 2))
        sre_l.append(xf[..., :SSM_STATE])
        sim_l.append(xf[..., SSM_STATE:])
        hg_l.append(_hg_state_out(st))
    return (x.reshape(batch, seq, D_MODEL), jnp.stack(ckv_l), jnp.stack(kpe_l), jnp.stack(sre_l),
            jnp.stack(sim_l), jnp.stack(hg_l))


def _run_sample(x, grp, w, p, rope, cache_ckv, cache_kpe, page_table, ssm_re, ssm_im, hg_state):
    batch = grp.batch
    depth = p["w_in"].shape[0]
    ln_g = p["ln_g"].reshape(depth, 3, 1, D_MODEL)
    ln_b = p["ln_b"].reshape(depth, 3, 1, D_MODEL)
    n = SSM_GROUPS * SSM_STATE
    lane = np.arange(MLA_PAD)
    sel = np.zeros((MLA_PAD, MLA_ROPE_DIM), np.float32)
    for i in range(MLA_ROPE_DIM):
        sel[(lane % HEAD_PAD) == ROPE_LANE0 + i, i] = 1.0
    sel = jnp.asarray(sel, BF16)
    hm = jnp.asarray((lane[None, :] // HEAD_PAD) == np.arange(MLA_HEADS)[:, None], F32)
    cache_kpe = jnp.swapaxes(cache_kpe, 2, 3)
    ckv_l, kpe_l, sre_l, sim_l, hg_l = [], [], [], [], []
    for l in range(depth):
        x = _ffn_call(grp, x, l, 0, 0, w["ffn_w_gu"], w["ffn_w_down"], ln_g, ln_b)
        u, zh, qh, kh, _, ckv, kpe = _mix_in_call(grp, x, l, w, rope)
        y, nr, ni = _s5_step_call(u, ssm_re[l].reshape(batch, n), ssm_im[l].reshape(batch, n), w, l)
        zh_pad = jnp.pad(zh[:, None, :], ((0, 0), (0, HG_GROUP - 1), (0, 0))).reshape(batch * HG_GROUP, -1)
        o_hp, st = _hgrn_call(zh_pad, _hg_state_in(hg_state[l]), w, l, batch, HG_GROUP, HG_GROUP, 1)
        o_h = o_hp.reshape(batch, HG_GROUP, HG_WIDTH)[:, 0]
        o_m = _decode_call(page_table, qh, ckv, kpe, cache_ckv, cache_kpe, w, l, sel, hm)
        o_m = o_m.reshape(batch, MLA_HEADS, HEAD_PAD)[:, :, :MLA_V_DIM].reshape(batch, -1).T[None]
        x = _mix_out_call(grp, x, l, y, zh, o_h, o_m, w, ln_g, ln_b)
        x = _ffn_call(grp, x, l, 1, 2, w["ffn_w_gu"], w["ffn_w_down"], ln_g, ln_b)
        ckv_l.append(ckv.reshape(batch, 1, MLA_KV_LORA))
        kpe_l.append(kpe.reshape(batch, 1, MLA_ROPE_DIM))
        sre_l.append(nr.reshape(batch, SSM_GROUPS, SSM_STATE))
        sim_l.append(ni.reshape(batch, SSM_GROUPS, SSM_STATE))
        hg_l.append(_hg_state_out(st))
    return (x.reshape(batch, 1, D_MODEL), jnp.stack(ckv_l), jnp.stack(kpe_l), jnp.stack(sre_l),
            jnp.stack(sim_l), jnp.stack(hg_l))


def kernel(x_prompt, x_sample, cache_kv_latent, cache_k_rope, state_ssm_re, state_ssm_im, state_hgrn, page_table, c_prompt, c_sample, w_ada, b_ada, ln_g, ln_b, ffn_w_gu, ffn_w_down, w_in, w_out, ssm_lambda_re, ssm_lambda_im, ssm_b_re, ssm_b_im, ssm_c_re, ssm_c_im, ssm_d, ssm_log_step, ssm_w_glu, norm_ssm, hgrn_lb_logits, norm_hgrn, mla_q_norm, mla_w_uq, mla_kv_norm, mla_w_uk, mla_w_uv, norm_mla):
    p = dict(ln_g=ln_g, ln_b=ln_b, ffn_w_gu=ffn_w_gu, ffn_w_down=ffn_w_down, w_in=w_in, w_out=w_out,
             ssm_lambda_re=ssm_lambda_re, ssm_lambda_im=ssm_lambda_im, ssm_b_re=ssm_b_re, ssm_b_im=ssm_b_im,
             ssm_c_re=ssm_c_re, ssm_c_im=ssm_c_im, ssm_d=ssm_d, ssm_log_step=ssm_log_step, ssm_w_glu=ssm_w_glu,
             norm_ssm=norm_ssm, hgrn_lb_logits=hgrn_lb_logits, norm_hgrn=norm_hgrn, mla_q_norm=mla_q_norm,
             mla_w_uq=mla_w_uq, mla_kv_norm=mla_kv_norm, mla_w_uk=mla_w_uk, mla_w_uv=mla_w_uv, norm_mla=norm_mla)
    depth = w_in.shape[0]
    assert depth == DEPTH_ and x_sample.shape[1] == 1
    bp, seq, _ = x_prompt.shape
    bs = x_sample.shape[0]
    past_len = page_table.shape[1] * PAGE_SIZE
    w = _prepare(p)
    w["hg_lcum"], w["hg_lall"] = {}, {}
    for tt in {min(256, seq), HG_GROUP}:
        w["hg_lcum"][tt], w["hg_lall"][tt] = _group_patterns(tt)

    mod = _ada_call(jnp.concatenate([c_prompt, c_sample], 0), w_ada, b_ada)
    mod = mod.reshape(depth, bp + bs, 9, D_MODEL)
    grp_p = _Group(mod[:, :bp], bp, seq, min(512, seq))
    grp_s = _Group(mod[:, bp:], bs, 1, bs)

    rope_p = _rope_tables(jnp.arange(seq))
    rope_s = _rope_tables(past_len + jnp.arange(1))
    y_p, ckv_p, kpe_p, sre_p, sim_p, hg_p = _run_prompt(x_prompt.reshape(bp * seq, D_MODEL), grp_p, w, p, rope_p)
    y_s, ckv_s, kpe_s, sre_s, sim_s, hg_s = _run_sample(
        x_sample.reshape(bs, D_MODEL), grp_s, w, p, rope_s, cache_kv_latent, cache_k_rope, page_table,
        state_ssm_re, state_ssm_im, state_hgrn)
    return (y_p, y_s, ckv_p, ckv_s, kpe_p, kpe_s, sre_p, sre_s, sim_p, sim_s, hg_p, hg_s)
```

```python
import functools
import math

import numpy as np
import jax
import jax.numpy as jnp
from jax import lax
from jax.experimental import pallas as pl
from jax.experimental.pallas import tpu as pltpu

F32 = jnp.float32
BF16 = jnp.bfloat16

D_MODEL = 1024
D_FF = 2816
SSM_WIDTH = 256
SSM_GROUP = 16
SSM_GROUPS = 16
SSM_STATE = 64
HG_WIDTH = 256
HG_HEADS = 4
HG_DK = 64
LB_FLOOR = 1e-30
MLA_HEADS = 8
MLA_V_DIM = 64
MLA_NOPE_DIM = 64
MLA_ROPE_DIM = 32
MLA_Q_LORA = 384
MLA_KV_LORA = 256
MLA_SCALE = (MLA_NOPE_DIM + MLA_ROPE_DIM) ** -0.5
ROPE_THETA = 10000.0
NEG_MASK = -1e30
PAGE_SIZE = 128

LANES = 128
HEAD_PAD = LANES
MLA_PAD = MLA_HEADS * HEAD_PAD
N_IN_PAD = 2048
ROPE_LANE0 = MLA_NOPE_DIM
S5_CHUNK = 16
HG_GROUP = 16
VMEM_LIMIT = 56 * 1024 * 1024
DEPTH_ = 4
ALPHA = (2 * DEPTH_) ** 0.25


def _cparams(sem):
    return pltpu.CompilerParams(dimension_semantics=sem, vmem_limit_bytes=VMEM_LIMIT)


def _dot(a, b):
    return jnp.dot(a, b, preferred_element_type=F32)


def _dot_nt(a, b):
    return lax.dot_general(a, b, (((1,), (1,)), ((), ())), preferred_element_type=F32)


def _dot_tn(a, b):
    return lax.dot_general(a, b, (((0,), (0,)), ((), ())), preferred_element_type=F32)


def _split_dot(w, x, terms):
    acc = None
    r = x
    for _ in range(terms):
        p = r.astype(BF16)
        d = _dot(w, p)
        acc = d if acc is None else acc + d
        r = r - p.astype(F32)
    return acc


def _split_dot_r(x, w, terms):
    acc = None
    r = x
    for _ in range(terms):
        p = r.astype(BF16)
        d = _dot(p, w)
        acc = d if acc is None else acc + d
        r = r - p.astype(F32)
    return acc


def _layer_norm(y, g, b):
    mu = jnp.mean(y, -1, keepdims=True)
    d = y - mu
    var = jnp.mean(d * d, -1, keepdims=True)
    return d * lax.rsqrt(var + 1e-5) * g + b


def _sigmoid(x):
    return 1.0 / (1.0 + jnp.exp(-x))


def _silu(x):
    return x * _sigmoid(x)


def _gelu_tanh(x):
    c = math.sqrt(2.0 / math.pi)
    return 0.5 * x * (1.0 + jnp.tanh(c * (x + 0.044715 * (x * x * x))))


def _ada_kernel(c_ref, w_ref, b_ref, o_ref):
    c = c_ref[...]
    s = _silu(c).astype(BF16)
    o_ref[...] = _dot(s, w_ref[...].astype(BF16)) + b_ref[...]


def _ada_call(c_all, w_ada, b_ada):
    rows = c_all.shape[0]
    depth, d, n = w_ada.shape
    tn = 1152
    return pl.pallas_call(
        _ada_kernel,
        grid=(depth, n // tn),
        in_specs=[pl.BlockSpec((rows, d), lambda l, j: (0, 0)),
                  pl.BlockSpec((None, d, tn), lambda l, j: (l, 0, j)),
                  pl.BlockSpec((None, 1, tn), lambda l, j: (l, 0, j))],
        out_specs=pl.BlockSpec((None, rows, tn), lambda l, j: (l, 0, j)),
        out_shape=jax.ShapeDtypeStruct((depth, rows, n), F32),
        compiler_params=_cparams(("parallel", "parallel")),
        name="ada",
    )(c_all, w_ada, b_ada.reshape(depth, 1, n))


class _Group:
    def __init__(self, mod, batch, seq, tm):
        self.batch, self.seq, self.tm = batch, seq, tm
        self.rows = batch * seq
        self.per_row = seq == 1
        depth = mod.shape[0]
        if self.per_row:
            self.mod = jnp.transpose(mod, (0, 2, 1, 3))
        else:
            self.mod = mod.reshape(depth, batch, 9, 1, D_MODEL)
            self.tiles_per_batch = seq // tm

    def mod_spec(self, l, j):
        if self.per_row:
            return pl.BlockSpec((None, None, self.tm, D_MODEL), lambda i: (l, j, i, 0))
        tpb = self.tiles_per_batch
        return pl.BlockSpec((None, None, None, 1, D_MODEL), lambda i: (l, i // tpb, j, 0, 0))

    def pos_spec(self, width):
        if self.per_row:
            return pl.BlockSpec((1, width), lambda i: (0, 0))
        tpb = self.tiles_per_batch
        return pl.BlockSpec((self.tm, width), lambda i: (i % tpb, 0))

    def row_spec(self, width, col=0):
        return pl.BlockSpec((self.tm, width), lambda i: (i, col))

    def tcol_shape(self, height):
        return (1, height, self.batch) if self.per_row else (self.batch, height, self.seq)

    def tcol_spec(self, height):
        if self.per_row:
            return pl.BlockSpec((None, height, self.tm), lambda i: (0, 0, 0))
        tpb = self.tiles_per_batch
        return pl.BlockSpec((None, height, self.tm), lambda i: (i // tpb, 0, i % tpb))

    @property
    def grid(self):
        return (self.rows // self.tm,)


def _const_spec(shape, index):
    return pl.BlockSpec(shape, lambda i: index)


FF_CHUNK = 256


def _ffn_kernel(x_ref, sh_ref, sc_ref, gt_ref, wgu_ref, wd_ref, lg_ref, lb_ref, o_ref, a_ref):
    x = x_ref[...]
    h = (x * (1.0 + sc_ref[...]) + sh_ref[...]).astype(BF16)
    for c in range(D_FF // FF_CHUNK):
        lo = c * FF_CHUNK
        g = _dot(h, wgu_ref[:, lo:lo + FF_CHUNK])
        u = _dot(h, wgu_ref[:, D_FF + lo:D_FF + lo + FF_CHUNK])
        a_ref[:, lo:lo + FF_CHUNK] = (_silu(g) * u).astype(BF16)
    f = _dot(a_ref[...], wd_ref[...])
    y = ALPHA * x + 0.5 * (1.0 + gt_ref[...]) * f
    o_ref[...] = _layer_norm(y, lg_ref[...], lb_ref[...])


def _ffn_call(grp, x, l, which, mod_idx, wgu, wd, ln_g, ln_b):
    tm = grp.tm
    return pl.pallas_call(
        _ffn_kernel,
        grid=grp.grid,
        in_specs=[grp.row_spec(D_MODEL),
                  grp.mod_spec(l, 3 * mod_idx), grp.mod_spec(l, 3 * mod_idx + 1), grp.mod_spec(l, 3 * mod_idx + 2),
                  _const_spec((None, None, D_MODEL, 2 * D_FF), (l, which, 0, 0)),
                  _const_spec((None, None, D_FF, D_MODEL), (l, which, 0, 0)),
                  _const_spec((None, None, 1, D_MODEL), (l, mod_idx, 0, 0)),
                  _const_spec((None, None, 1, D_MODEL), (l, mod_idx, 0, 0))],
        out_specs=grp.row_spec(D_MODEL),
        out_shape=jax.ShapeDtypeStruct((grp.rows, D_MODEL), F32),
        scratch_shapes=[pltpu.VMEM((tm, D_FF), BF16)],
        compiler_params=_cparams(("parallel",)),
        name="ffn",
    )(x, grp.mod, grp.mod, grp.mod, wgu, wd, ln_g, ln_b)


Z_U = (0, 256)
Z_H = (256, 1280)
Z_CQ = (1280, 1664)
Z_CKV = (1664, 1920)
Z_KPE = (1920, 2048)


def _rope_group(t, c, s1, s2):
    return t * c + pltpu.roll(t, LANES - MLA_ROPE_DIM // 2, 1) * s1 + pltpu.roll(t, MLA_ROPE_DIM // 2, 1) * s2


def _mix_in_kernel(x_ref, sh_ref, sc_ref, win_ref, qn_ref, wuq_ref, kvn_ref, wuk_ref, wuv_ref,
                   rc_ref, rs1_ref, rs2_ref,
                   u_ref, zh_ref, qh_ref, kh_ref, vt_ref, ckv_ref, kpe_ref):
    x = x_ref[...]
    h = (x * (1.0 + sc_ref[...]) + sh_ref[...]).astype(BF16)
    u_ref[...] = _dot(h, win_ref[:, Z_U[0]:Z_U[1]])
    zh_ref[...] = _dot(h, win_ref[:, Z_H[0]:Z_H[1]])
    rc, rs1, rs2 = rc_ref[...], rs1_ref[...], rs2_ref[...]

    zcq = _dot(h, win_ref[:, Z_CQ[0]:Z_CQ[1]])
    cq = zcq * lax.rsqrt(jnp.mean(zcq * zcq, -1, keepdims=True) + 1e-6) * qn_ref[...]
    q = _dot(cq.astype(BF16), wuq_ref[...]) * MLA_SCALE
    for hd in range(MLA_HEADS):
        lo = hd * HEAD_PAD
        qh_ref[:, lo:lo + HEAD_PAD] = _rope_group(q[:, lo:lo + HEAD_PAD], rc, rs1, rs2).astype(BF16)

    zckv = _dot(h, win_ref[:, Z_CKV[0]:Z_CKV[1]])
    ckv = zckv * lax.rsqrt(jnp.mean(zckv * zckv, -1, keepdims=True) + 1e-6) * kvn_ref[...]
    ckv_ref[...] = ckv
    ckv_b = ckv.astype(BF16)
    kpe = _rope_group(_dot(h, win_ref[:, Z_KPE[0]:Z_KPE[1]]), rc, rs1, rs2)
    kpe_ref[...] = pltpu.roll(kpe, LANES - ROPE_LANE0, 1)[:, :MLA_ROPE_DIM]
    kn = _dot(ckv_b, wuk_ref[...])
    for hd in range(MLA_HEADS):
        lo = hd * HEAD_PAD
        kh_ref[:, lo:lo + HEAD_PAD] = (kn[:, lo:lo + HEAD_PAD] + kpe).astype(BF16)
    vt_ref[...] = _dot_nt(wuv_ref[...], ckv_b).astype(BF16)


def _mix_in_call(grp, x, l, w, rope):
    rows = grp.rows
    v_rows = MLA_HEADS * MLA_V_DIM
    outs = [((rows, SSM_WIDTH), F32), ((rows, 4 * HG_WIDTH), F32), ((rows, MLA_PAD), BF16),
            ((rows, MLA_PAD), BF16), (grp.tcol_shape(v_rows), BF16), ((rows, MLA_KV_LORA), F32),
            ((rows, MLA_ROPE_DIM), F32)]
    out_specs = [grp.row_spec(s[1]) for s, _ in outs]
    out_specs[4] = grp.tcol_spec(v_rows)
    return pl.pallas_call(
        _mix_in_kernel,
        grid=grp.grid,
        in_specs=[grp.row_spec(D_MODEL), grp.mod_spec(l, 3), grp.mod_spec(l, 4),
                  _const_spec((None, D_MODEL, N_IN_PAD), (l, 0, 0)),
                  _const_spec((None, 1, MLA_Q_LORA), (l, 0, 0)),
                  _const_spec((None, MLA_Q_LORA, MLA_PAD), (l, 0, 0)),
                  _const_spec((None, 1, MLA_KV_LORA), (l, 0, 0)),
                  _const_spec((None, MLA_KV_LORA, MLA_PAD), (l, 0, 0)),
                  _const_spec((None, v_rows, MLA_KV_LORA), (l, 0, 0)),
                  grp.pos_spec(LANES), grp.pos_spec(LANES), grp.pos_spec(LANES)],
        out_specs=out_specs,
        out_shape=[jax.ShapeDtypeStruct(s, d) for s, d in outs],
        compiler_params=_cparams(("parallel",)),
        name="mix_in",
    )(x, grp.mod, grp.mod, w["w_in"], w["q_norm"], w["w_uq"], w["kv_norm"], w["w_uk"], w["w_uv_t"], *rope)


def _s5_kernel(u_ref, m_ref, wis_ref, wso_ref, av_ref, d_ref, x0_ref, y_ref, xf_ref, sv_scr, sw_scr, x_scr,
               *, nb, nc):
    u = u_ref[...]
    ub = u.astype(BF16)
    sv_scr[...] = _dot(ub, wis_ref[:, 0:LANES])
    sw_scr[...] = _dot(ub, wis_ref[:, LANES:2 * LANES])
    a1, a2, a3 = av_ref[0:1, :], av_ref[1:2, :], av_ref[2:3, :]

    def body(c, carry):
        v, w = carry
        rows_c = pl.ds(c, nb, stride=nc)
        x_scr[rows_c, :] = v
        return a1 * v + a2 * w + sv_scr[rows_c, :], a1 * w + a3 * v + sw_scr[rows_c, :]

    v, _ = lax.fori_loop(0, nc, body, (x0_ref[:, 0:LANES], x0_ref[:, LANES:2 * LANES]), unroll=8)
    xf_ref[...] = v
    y_ref[...] = _dot(ub, m_ref[...]) + _dot(x_scr[...].astype(BF16), wso_ref[...]) + u * d_ref[...]


def _s5_pack_kernel(u_ref, sel_ref, o_ref, lo_scr, hi_scr, *, nch):
    g = SSM_GROUPS
    lo_scr[...] = u_ref[:, 0:LANES]
    hi_scr[...] = u_ref[:, LANES:2 * LANES]
    lane_grp = lax.broadcasted_iota(jnp.int32, (g * nch, SSM_WIDTH), 1) // SSM_GROUP
    row_grp = lax.broadcasted_iota(jnp.int32, (g * nch, SSM_WIDTH), 0) // nch
    keep = lane_grp == row_grp
    acc = jnp.zeros((g * nch, SSM_WIDTH), F32)
    for t in range(S5_CHUNK):
        step_rows = pl.ds(t, nch, stride=S5_CHUNK)
        rows = jnp.concatenate([lo_scr[step_rows, :], hi_scr[step_rows, :]], 1)
        lhs = jnp.where(keep, jnp.tile(rows, (g, 1)), 0.0).astype(BF16)
        acc = acc + _dot(lhs, sel_ref[t])
    o_ref[...] = acc.reshape(g, nch, SSM_WIDTH).astype(o_ref.dtype)


def _s5_unpack_kernel(y_ref, sel_ref, o_ref, lo_scr, hi_scr, *, nch):
    g = SSM_GROUPS
    lane_grp = lax.broadcasted_iota(jnp.int32, (g * nch, SSM_WIDTH), 1) // SSM_GROUP
    row_grp = lax.broadcasted_iota(jnp.int32, (g * nch, SSM_WIDTH), 0) // nch
    keep = lane_grp == row_grp
    yb = y_ref[...].reshape(g * nch, SSM_WIDTH).astype(BF16)
    for t in range(S5_CHUNK):
        z = jnp.where(keep, _dot(yb, sel_ref[t]), 0.0)
        tok = jnp.sum(z.reshape(g, nch, SSM_WIDTH), 0)
        step_rows = pl.ds(t, nch, stride=S5_CHUNK)
        lo_scr[step_rows, :] = tok[:, 0:LANES]
        hi_scr[step_rows, :] = tok[:, LANES:2 * LANES]
    o_ref[...] = jnp.concatenate([lo_scr[...], hi_scr[...]], 1)


def _s5_relayout_call(kern, x, sel, grp, to_groups):
    nch = grp.tm // S5_CHUNK
    tpb = grp.tiles_per_batch
    tok_spec = grp.row_spec(SSM_WIDTH)
    grp_spec = pl.BlockSpec((SSM_GROUPS, None, nch, SSM_WIDTH), lambda i: (0, i // tpb, i % tpb, 0))
    grp_shape = (SSM_GROUPS, grp.batch, grp.seq // S5_CHUNK, SSM_WIDTH)
    return pl.pallas_call(
        functools.partial(kern, nch=nch),
        grid=grp.grid,
        in_specs=[tok_spec if to_groups else grp_spec,
                  _const_spec((S5_CHUNK, SSM_WIDTH, SSM_WIDTH), (0, 0, 0))],
        out_specs=grp_spec if to_groups else tok_spec,
        out_shape=jax.ShapeDtypeStruct(grp_shape, BF16) if to_groups
        else jax.ShapeDtypeStruct((grp.rows, SSM_WIDTH), F32),
        scratch_shapes=[pltpu.VMEM((grp.tm, LANES), F32)] * 2,
        compiler_params=_cparams(("parallel",)),
        name="s5_pack" if to_groups else "s5_unpack",
    )(x, sel)


def _s5_call(u_g, x0, w, l, nb, nc):
    g = SSM_GROUPS
    rows = nb * nc
    wide = S5_CHUNK * SSM_GROUP
    u_g = u_g.reshape(g, rows, wide)
    return pl.pallas_call(
        functools.partial(_s5_kernel, nb=nb, nc=nc),
        grid=(g,),
        in_specs=[pl.BlockSpec((None, rows, wide), lambda i: (i, 0, 0)),
                  pl.BlockSpec((None, None, wide, wide), lambda i: (l, i, 0, 0)),
                  pl.BlockSpec((None, None, wide, wide), lambda i: (l, i, 0, 0)),
                  pl.BlockSpec((None, None, LANES, wide), lambda i: (l, i, 0, 0)),
                  pl.BlockSpec((None, None, 8, LANES), lambda i: (l, i, 0, 0)),
                  pl.BlockSpec((None, None, 1, wide), lambda i: (l, i, 0, 0)),
                  pl.BlockSpec((None, nb, wide), lambda i: (i, 0, 0))],
        out_specs=[pl.BlockSpec((None, rows, wide), lambda i: (i, 0, 0)),
                   pl.BlockSpec((None, nb, LANES), lambda i: (i, 0, 0))],
        out_shape=[jax.ShapeDtypeStruct((g, rows, wide), F32), jax.ShapeDtypeStruct((g, nb, LANES), F32)],
        scratch_shapes=[pltpu.VMEM((rows, LANES), F32)] * 3,
        compiler_params=_cparams(("parallel",)),
        name="s5_chunks",
    )(u_g, w["s5_m"], w["s5_wis"], w["s5_wso"], w["s5_av"], w["s5_d"], x0)


def _s5_step_kernel(u_ref, xr_ref, xi_ref, bb_ref, cc_ref, ar_ref, ai_ref, d_ref, y_ref, or_ref, oi_ref):
    u = u_ref[...]
    n = SSM_GROUPS * SSM_STATE
    bu = _dot(u.astype(BF16), bb_ref[...])
    xr, xi, ar, ai = xr_ref[...], xi_ref[...], ar_ref[...], ai_ref[...]
    nr = ar * xr - ai * xi + bu[:, :n]
    ni = ar * xi + ai * xr + bu[:, n:]
    or_ref[...] = nr
    oi_ref[...] = ni
    y_ref[...] = (_dot(nr.astype(BF16), cc_ref[0:n, :]) + _dot(ni.astype(BF16), cc_ref[n:2 * n, :])
                  + u * d_ref[...])


def _s5_step_call(u, xr, xi, w, l):
    b = u.shape[0]
    n = SSM_GROUPS * SSM_STATE
    full = lambda shape: pl.BlockSpec(shape, lambda i: (0,) * len(shape))
    lsel = lambda shape: pl.BlockSpec((None,) + shape, lambda i: (l,) + (0,) * len(shape))
    return pl.pallas_call(
        _s5_step_kernel,
        grid=(1,),
        in_specs=[full((b, SSM_WIDTH)), full((b, n)), full((b, n)),
                  lsel((SSM_WIDTH, 2 * n)), lsel((2 * n, SSM_WIDTH)), lsel((1, n)), lsel((1, n)),
                  lsel((1, SSM_WIDTH))],
        out_specs=[full((b, SSM_WIDTH)), full((b, n)), full((b, n))],
        out_shape=[jax.ShapeDtypeStruct((b, SSM_WIDTH), F32), jax.ShapeDtypeStruct((b, n), F32),
                   jax.ShapeDtypeStruct((b, n), F32)],
        compiler_params=_cparams(("arbitrary",)),
        name="s5_step",
    )(u, xr, xi, w["s5_bb"], w["s5_cc"], w["s5_ar"], w["s5_ai"], w["s5_dflat"])


def _hgrn_kernel(zh_ref, s0_ref, lbp_ref, e_ref, bd_ref, lcum_ref, lall_ref, o_ref, sf_ref,
                 perm_scr, acc_scr, st_ref, oi_ref, *, tt, n_valid):
    j = pl.program_id(1)

    @pl.when(j == 0)
    def _():
        st_ref[...] = s0_ref[...]

    w = HG_WIDTH
    ng = tt // HG_GROUP
    zf = zh_ref[:, 0:w]
    q = zh_ref[:, w:2 * w]
    v = zh_ref[:, 2 * w:3 * w]
    lbm, oml = lbp_ref[0:1, :], lbp_ref[1:2, :]
    sig = 1.0 / (1.0 + jnp.exp(-zf))
    f = lbm + oml * sig
    k = oml * (1.0 - sig)
    if n_valid < tt:
        live = lax.broadcasted_iota(jnp.int32, (tt, w), 0) < n_valid
        f = jnp.where(live, f, 1.0)
        k = jnp.where(live, k, 0.0)
    lf = jnp.log(f)

    def regroup(i, x):
        if ng == 1:
            return x
        perm_scr[2 * i] = x[:, 0:LANES]
        perm_scr[2 * i + 1] = x[:, LANES:2 * LANES]
        blocks = []
        for pos in range(HG_GROUP):
            rows = pl.ds(pos, ng, stride=HG_GROUP)
            blocks.append(jnp.concatenate([perm_scr[2 * i, rows, :], perm_scr[2 * i + 1, rows, :]], 1))
        return jnp.concatenate(blocks, 0)

    qg, kg, vg, fg = regroup(0, q), regroup(1, k), regroup(2, v), regroup(3, f)
    e = e_ref[...]
    acc_scr[...] = _dot((qg * kg).astype(BF16), e) * vg
    decay = None
    for d in range(1, min(HG_GROUP, n_valid)):
        n_rows = (HG_GROUP - d) * ng
        f_blk = fg[ng:ng + n_rows]
        decay = f_blk if decay is None else decay[ng:] * f_blk
        p = qg[d * ng:] * kg[:n_rows] * decay
        acc_scr[d * ng:, :] += _dot(p.astype(BF16), e) * vg[:n_rows]
    if ng == 1:
        acc = acc_scr[...]
    else:
        for pos in range(HG_GROUP):
            rows = pl.ds(pos, ng, stride=HG_GROUP)
            perm_scr[0, rows, :] = acc_scr[pos * ng:(pos + 1) * ng, 0:LANES]
            perm_scr[1, rows, :] = acc_scr[pos * ng:(pos + 1) * ng, LANES:2 * LANES]
        acc = jnp.concatenate([perm_scr[0], perm_scr[1]], 1)

    b = _split_dot(lcum_ref[...], lf, 3)
    bl = _split_dot(lall_ref[...], lf, 3)
    qe = (q * jnp.exp(b)).astype(BF16)
    kx = (k * jnp.exp(bl - b)).astype(BF16)
    ebl = jnp.exp(bl)
    vb = v.astype(BF16)
    bd = bd_ref[...]
    for g in range(tt // HG_GROUP):
        lo = g * HG_GROUP
        st = st_ref[...]
        oi_ref[lo:lo + HG_GROUP, :] = _dot_nt(qe[lo:lo + HG_GROUP], st.astype(BF16))
        upd = _dot_tn(vb[lo:lo + HG_GROUP], kx[lo:lo + HG_GROUP])
        st_ref[...] = st * ebl[lo:lo + 1, :] + upd * bd
    o_ref[...] = acc + oi_ref[...]

    @pl.when(j == pl.num_programs(1) - 1)
    def _():
        sf_ref[...] = st_ref[...]


def _hgrn_call(zh, s0t, w, l, batch, seq_rows, tt, n_valid):
    wd = HG_WIDTH
    nj = seq_rows // tt
    return pl.pallas_call(
        functools.partial(_hgrn_kernel, tt=tt, n_valid=n_valid),
        grid=(batch, nj),
        in_specs=[pl.BlockSpec((tt, 4 * wd), lambda b, j: (b * nj + j, 0)),
                  pl.BlockSpec((None, wd, wd), lambda b, j: (b, 0, 0)),
                  pl.BlockSpec((None, 8, wd), lambda b, j: (l, 0, 0)),
                  pl.BlockSpec((wd, wd), lambda b, j: (0, 0)),
                  pl.BlockSpec((wd, wd), lambda b, j: (0, 0)),
                  pl.BlockSpec((tt, tt), lambda b, j: (0, 0)),
                  pl.BlockSpec((tt, tt), lambda b, j: (0, 0))],
        out_specs=[pl.BlockSpec((tt, wd), lambda b, j: (b * nj + j, 0)),
                   pl.BlockSpec((None, wd, wd), lambda b, j: (b, 0, 0))],
        out_shape=[jax.ShapeDtypeStruct((batch * seq_rows, wd), F32),
                   jax.ShapeDtypeStruct((batch, wd, wd), F32)],
        scratch_shapes=[pltpu.VMEM((8, tt, LANES), F32), pltpu.VMEM((tt, wd), F32),
                        pltpu.VMEM((wd, wd), F32), pltpu.VMEM((tt, wd), F32)],
        compiler_params=_cparams(("parallel", "arbitrary")),
        name="hgrn",
    )(zh, s0t, w["hg_lbp"], w["hg_e"], w["hg_bd"], w["hg_lcum"][tt], w["hg_lall"][tt])


def _flash_kernel(q_ref, k_ref, vt_ref, o_ref, m_ref, l_ref, acc_ref):
    qi = pl.program_id(1)
    ki = pl.program_id(2)

    @pl.when(ki == 0)
    def _():
        m_ref[...] = jnp.full(m_ref.shape, NEG_MASK, F32)
        l_ref[...] = jnp.zeros(l_ref.shape, F32)
        acc_ref[...] = jnp.zeros(acc_ref.shape, F32)

    tq = q_ref.shape[0]

    def step(keys, qrys, diagonal):
        for hd in range(MLA_HEADS):
            sl = slice(hd * HEAD_PAD, (hd + 1) * HEAD_PAD)
            vs = slice(hd * MLA_V_DIM, (hd + 1) * MLA_V_DIM)
            st = _dot_nt(k_ref[keys, sl], q_ref[qrys, sl])
            if diagonal:
                key = lax.broadcasted_iota(jnp.int32, st.shape, 0)
                qry = lax.broadcasted_iota(jnp.int32, st.shape, 1)
                st = jnp.where(key <= qry, st, NEG_MASK)
            m_prev = m_ref[hd:hd + 1, qrys]
            m_new = jnp.maximum(m_prev, jnp.max(st, 0, keepdims=True))
            alpha = jnp.exp(m_prev - m_new)
            p = jnp.exp(st - m_new)
            l_ref[hd:hd + 1, qrys] = alpha * l_ref[hd:hd + 1, qrys] + jnp.sum(p, 0, keepdims=True)
            acc_ref[vs, qrys] = alpha * acc_ref[vs, qrys] + _dot(vt_ref[vs, keys], p.astype(BF16))
            m_ref[hd:hd + 1, qrys] = m_new

    @pl.when(ki < qi)
    def _():
        step(slice(0, tq), slice(0, tq), False)

    @pl.when(ki == qi)
    def _():
        step(slice(0, tq), slice(0, tq), True)
        for hd in range(MLA_HEADS):
            vs = slice(hd * MLA_V_DIM, (hd + 1) * MLA_V_DIM)
            o_ref[vs, :] = acc_ref[vs, :] / l_ref[hd:hd + 1, :]


def _flash_call(qh, kh, vt, batch, seq, tq):
    nq = seq // tq
    v_rows = MLA_HEADS * MLA_V_DIM
    q_spec = pl.BlockSpec((None, tq, MLA_PAD), lambda b, qi, ki: (b, qi, 0))
    k_spec = pl.BlockSpec((None, tq, MLA_PAD), lambda b, qi, ki: (b, jnp.minimum(ki, qi), 0))
    vt_spec = pl.BlockSpec((None, v_rows, tq), lambda b, qi, ki: (b, 0, jnp.minimum(ki, qi)))
    shp = (batch, seq, MLA_PAD)
    return pl.pallas_call(
        _flash_kernel,
        grid=(batch, nq, nq),
        in_specs=[q_spec, k_spec, vt_spec],
        out_specs=pl.BlockSpec((None, v_rows, tq), lambda b, qi, ki: (b, 0, qi)),
        out_shape=jax.ShapeDtypeStruct((batch, v_rows, seq), F32),
        scratch_shapes=[pltpu.VMEM((MLA_HEADS, tq), F32), pltpu.VMEM((MLA_HEADS, tq), F32),
                        pltpu.VMEM((v_rows, tq), F32)],
        compiler_params=_cparams(("parallel", "parallel", "arbitrary")),
        name="flash",
    )(qh.reshape(shp), kh.reshape(shp), vt)


PAGES_PER_STEP = 64


def _decode_kernel(pt_ref, qh_ref, ckvn_ref, kpen_ref, wuk_ref, wuv_ref, sel_ref, hm_ref, ckv_hbm, kpe_hbm,
                   o_ref, ckv_buf, kpe_buf, sem_c, sem_k, *, layer, npg, ng):
    b = pl.program_id(0)
    hm = hm_ref[...]

    def copies(seq_i, grp_i, slot):
        out = []
        for i in range(npg):
            page = pt_ref[seq_i, grp_i * npg + i]
            out.append(pltpu.make_async_copy(ckv_hbm.at[layer, page], ckv_buf.at[slot, i], sem_c.at[slot]))
            out.append(pltpu.make_async_copy(kpe_hbm.at[layer, page], kpe_buf.at[slot, i], sem_k.at[slot]))
        return out

    @pl.when(b == 0)
    def _():
        for c in copies(0, 0, 0):
            c.start()

    qbd = jnp.where(hm > 0, jnp.broadcast_to(qh_ref[...], hm.shape), jnp.zeros_like(hm)).astype(BF16)
    qa = _dot_nt(qbd, wuk_ref[...]).astype(BF16)
    qp = _dot(qbd, sel_ref[...]).astype(BF16)
    m = jnp.full((MLA_HEADS, 1), NEG_MASK, F32)
    l = jnp.zeros((MLA_HEADS, 1), F32)
    acc = jnp.zeros((MLA_HEADS, MLA_KV_LORA), F32)
    for j in range(ng):
        slot = lax.rem(b * ng + j, 2)
        if j + 1 < ng:
            for c in copies(b, j + 1, 1 - slot):
                c.start()
        else:
            @pl.when(b + 1 < pl.num_programs(0))
            def _():
                for c in copies(b + 1, 0, 1 - slot):
                    c.start()
        for c in copies(b, j, slot):
            c.wait()
        pages = [ckv_buf[slot, i].astype(BF16) for i in range(npg)]
        s = jnp.concatenate(
            [_dot_nt(qa, pg) + _dot(qp, kpe_buf[slot, i].astype(BF16)) for i, pg in enumerate(pages)], axis=1)
        m_new = jnp.maximum(m, jnp.max(s, -1, keepdims=True))
        alpha = jnp.exp(m - m_new)
        p = jnp.exp(s - m_new)
        l = alpha * l + jnp.sum(p, -1, keepdims=True)
        pb = p.astype(BF16)
        parts = [None, None]
        for i, pg in enumerate(pages):
            d = _dot(pb[:, i * PAGE_SIZE:(i + 1) * PAGE_SIZE], pg)
            parts[i % 2] = d if parts[i % 2] is None else parts[i % 2] + d
        acc = alpha * acc + (parts[0] if parts[1] is None else parts[0] + parts[1])
        m = m_new

    ckvn = ckvn_ref[...]
    s_new = (jnp.sum(qa.astype(F32) * ckvn.astype(BF16).astype(F32), -1, keepdims=True)
             + jnp.sum(qp.astype(F32) * kpen_ref[...].astype(BF16).astype(F32), -1, keepdims=True))
    m_fin = jnp.maximum(m, s_new)
    a = jnp.exp(m - m_fin)
    p_new = jnp.exp(s_new - m_fin)
    l_fin = a * l + p_new
    lat = a * acc + p_new.astype(BF16).astype(F32) * ckvn.astype(BF16).astype(F32)
    lat = lat / l_fin
    o = _dot(lat.astype(BF16), wuv_ref[...])
    o_ref[...] = jnp.sum(jnp.where(hm > 0, o, 0.0), 0, keepdims=True)


def _decode_call(page_table, qh, ckv_new, kpe_new, cache_ckv, cache_kpe, w, l, sel, hm):
    batch, n_pages = page_table.shape
    npg = min(PAGES_PER_STEP, n_pages)
    ng = n_pages // npg
    row3 = lambda width: pl.BlockSpec((None, 1, width), lambda b, pt: (b, 0, 0))
    wsel = lambda shape: pl.BlockSpec((None,) + shape, lambda b, pt: (l,) + (0,) * len(shape))
    full = lambda shape: pl.BlockSpec(shape, lambda b, pt: (0,) * len(shape))
    hbm = pl.BlockSpec(memory_space=pl.ANY)
    grid_spec = pltpu.PrefetchScalarGridSpec(
        num_scalar_prefetch=1,
        grid=(batch,),
        in_specs=[row3(MLA_PAD), row3(MLA_KV_LORA), row3(MLA_ROPE_DIM),
                  wsel((MLA_KV_LORA, MLA_PAD)), wsel((MLA_KV_LORA, MLA_PAD)),
                  full((MLA_PAD, MLA_ROPE_DIM)), full((MLA_HEADS, MLA_PAD)), hbm, hbm],
        out_specs=row3(MLA_PAD),
        scratch_shapes=[pltpu.VMEM((2, npg, PAGE_SIZE, MLA_KV_LORA), cache_ckv.dtype),
                        pltpu.VMEM((2, npg, MLA_ROPE_DIM, PAGE_SIZE), cache_kpe.dtype),
                        pltpu.SemaphoreType.DMA((2,)), pltpu.SemaphoreType.DMA((2,))],
    )
    out = pl.pallas_call(
        functools.partial(_decode_kernel, layer=l, npg=npg, ng=ng),
        grid_spec=grid_spec,
        out_shape=jax.ShapeDtypeStruct((batch, 1, MLA_PAD), F32),
        compiler_params=_cparams(("arbitrary",)),
        name="decode",
    )(page_table, qh.reshape(batch, 1, MLA_PAD), ckv_new.reshape(batch, 1, MLA_KV_LORA),
      kpe_new.reshape(batch, 1, MLA_ROPE_DIM), w["w_uk"], w["w_uv"], sel, hm, cache_ckv, cache_kpe)
    return out.reshape(batch, MLA_PAD)


def _mix_out_kernel(x_ref, gt_ref, y_ref, zg_ref, oh_ref, om_ref, wglu_ref, nssm_ref, nhg_ref, nmla_ref,
                    e64_ref, woa_ref, wob_ref, lg_ref, lb_ref, o_ref, cat_ref):
    w = SSM_WIDTH
    gab = _dot(_gelu_tanh(y_ref[...]).astype(BF16), wglu_ref[...])
    t = gab[:, :w] * _sigmoid(gab[:, w:])
    cat_ref[:, 0:w] = (t * lax.rsqrt(jnp.mean(t * t, -1, keepdims=True) + 1e-6) * nssm_ref[...]).astype(BF16)

    oh = oh_ref[...]
    msq = _split_dot_r(oh * oh, e64_ref[...], 2)
    cat_ref[:, w:2 * w] = (oh * lax.rsqrt(msq + 1e-6) * nhg_ref[...] * _silu(zg_ref[...])).astype(BF16)

    om = om_ref[...]
    tm = om.shape[1]
    nm = nmla_ref[...]
    nm = nm[:, :tm] if tm <= LANES else jnp.tile(nm, (1, tm // LANES))
    ms = jnp.mean(om * om, 0, keepdims=True)
    o_mla = (om * lax.rsqrt(ms + 1e-6) * nm).astype(BF16)

    mix = _dot(cat_ref[...], woa_ref[...]) + _dot_tn(o_mla, wob_ref[...])
    y = ALPHA * x_ref[...] + (1.0 + gt_ref[...]) * mix
    o_ref[...] = _layer_norm(y, lg_ref[...], lb_ref[...])


def _mix_out_call(grp, x, l, y_ssm, zh, o_h, o_m, w, ln_g, ln_b):
    tm = grp.tm
    v_rows = MLA_HEADS * MLA_V_DIM
    return pl.pallas_call(
        _mix_out_kernel,
        grid=grp.grid,
        in_specs=[grp.row_spec(D_MODEL), grp.mod_spec(l, 5),
                  grp.row_spec(SSM_WIDTH), grp.row_spec(HG_WIDTH, col=3), grp.row_spec(HG_WIDTH),
                  grp.tcol_spec(v_rows),
                  _const_spec((None, SSM_WIDTH, 2 * SSM_WIDTH), (l, 0, 0)),
                  _const_spec((None, 1, SSM_WIDTH), (l, 0, 0)),
                  _const_spec((None, 1, HG_WIDTH), (l, 0, 0)),
                  _const_spec((None, v_rows, LANES), (l, 0, 0)),
                  _const_spec((HG_WIDTH, HG_WIDTH), (0, 0)),
                  _const_spec((None, SSM_WIDTH + HG_WIDTH, D_MODEL), (l, 0, 0)),
                  _const_spec((None, v_rows, D_MODEL), (l, 0, 0)),
                  _const_spec((None, None, 1, D_MODEL), (l, 1, 0, 0)),
                  _const_spec((None, None, 1, D_MODEL), (l, 1, 0, 0))],
        out_specs=grp.row_spec(D_MODEL),
        out_shape=jax.ShapeDtypeStruct((grp.rows, D_MODEL), F32),
        scratch_shapes=[pltpu.VMEM((tm, SSM_WIDTH + HG_WIDTH), BF16)],
        compiler_params=_cparams(("parallel",)),
        name="mix_out",
    )(x, grp.mod, y_ssm, zh, o_h, o_m, w["w_glu"], w["n_ssm"], w["n_hg"], w["n_mla"], w["hg_e64"],
      w["w_out_a"], w["w_out_b"], ln_g, ln_b)


def _pad_heads(wt, per_head):
    lead = wt.shape[:-1]
    wt = wt.reshape(lead + (MLA_HEADS, per_head))
    wt = jnp.pad(wt, [(0, 0)] * len(lead) + [(0, 0), (0, HEAD_PAD - per_head)])
    return wt.reshape(lead + (MLA_PAD,))


def _s5_tables(lam_re, lam_im, b_re, b_im, c_re, c_im, d, log_step):
    hp = lax.Precision.HIGHEST
    depth, g, p = lam_re.shape
    h, t = SSM_GROUP, S5_CHUNK
    step = jnp.exp(log_step)[..., None]
    mag = jnp.exp(lam_re * step)
    ab_re, ab_im = mag * jnp.cos(lam_im * step), mag * jnp.sin(lam_im * step)
    den = lam_re * lam_re + lam_im * lam_im
    nr = ab_re - 1.0
    coef_re = (nr * lam_re + ab_im * lam_im) / den
    coef_im = (ab_im * lam_re - nr * lam_im) / den
    bb_re = coef_re[..., None] * b_re - coef_im[..., None] * b_im
    bb_im = coef_re[..., None] * b_im + coef_im[..., None] * b_re
    n = jnp.arange(t + 1, dtype=F32)[:, None, None, None]
    pmag = jnp.exp(n * lam_re * step)
    pw_re, pw_im = pmag * jnp.cos(n * lam_im * step), pmag * jnp.sin(n * lam_im * step)
    pb_re = pw_re[:t, ..., None] * bb_re - pw_im[:t, ..., None] * bb_im
    pb_im = pw_re[:t, ..., None] * bb_im + pw_im[:t, ..., None] * bb_re
    kern = (jnp.einsum('lgop,nlgpi->nlgoi', c_re, pb_re, precision=hp)
            - jnp.einsum('lgop,nlgpi->nlgoi', c_im, pb_im, precision=hp))
    k2 = jnp.transpose(kern, (1, 2, 4, 0, 3)).reshape(depth, g, h, t * h)
    m = jnp.stack([jnp.pad(k2[..., :(t - s) * h], ((0, 0), (0, 0), (0, 0), (s * h, 0))) for s in range(t)], 2)
    m = m.reshape(depth, g, t * h, t * h)
    wr = jnp.transpose(pb_re[::-1], (1, 2, 0, 4, 3)).reshape(depth, g, t * h, p)
    wi = jnp.transpose(pb_im[::-1], (1, 2, 0, 4, 3)).reshape(depth, g, t * h, p)
    wis = jnp.concatenate([wr, wi, wi, wr], -1)
    cp_re = c_re[None] * pw_re[1:, :, :, None, :] - c_im[None] * pw_im[1:, :, :, None, :]
    cp_im = c_re[None] * pw_im[1:, :, :, None, :] + c_im[None] * pw_re[1:, :, :, None, :]
    so_re = jnp.transpose(cp_re, (1, 2, 4, 0, 3)).reshape(depth, g, p, t * h)
    so_im = jnp.transpose(-cp_im, (1, 2, 4, 0, 3)).reshape(depth, g, p, t * h)
    wso = jnp.concatenate([so_re, so_im], 2)
    a_re, a_im = pw_re[t], pw_im[t]
    av = jnp.stack([jnp.concatenate([a_re, a_re], -1), jnp.concatenate([-a_im, a_im], -1),
                    jnp.concatenate([a_im, -a_im], -1)], 2)
    av = jnp.pad(av, ((0, 0), (0, 0), (0, 5), (0, 0)))
    dg = d.reshape(depth, g, 1, h)
    d_t = jnp.tile(dg, (1, 1, 1, t))
    eye = jnp.eye(g, dtype=F32)
    bbr = jnp.einsum('lgph,gk->lghkp', bb_re, eye).reshape(depth, g * h, g * p)
    bbi = jnp.einsum('lgph,gk->lghkp', bb_im, eye).reshape(depth, g * h, g * p)
    ccr = jnp.einsum('lghp,gk->lgpkh', c_re, eye).reshape(depth, g * p, g * h)
    cci = jnp.einsum('lghp,gk->lgpkh', -c_im, eye).reshape(depth, g * p, g * h)
    return dict(s5_m=m.astype(BF16), s5_wis=wis.astype(BF16), s5_wso=wso.astype(BF16), s5_av=av, s5_d=d_t,
                s5_bb=jnp.concatenate([bbr, bbi], -1).astype(BF16),
                s5_cc=jnp.concatenate([ccr, cci], 1).astype(BF16),
                s5_ar=ab_re.reshape(depth, 1, g * p), s5_ai=ab_im.reshape(depth, 1, g * p),
                s5_dflat=d.reshape(depth, 1, g * h))


def _rope_tables(pos):
    half = MLA_ROPE_DIM // 2
    inv = 1.0 / (ROPE_THETA ** (jnp.arange(0, MLA_ROPE_DIM, 2, dtype=F32) / MLA_ROPE_DIM))
    ang = pos.astype(F32)[:, None] * inv[None, :]
    cos, sin = jnp.cos(ang), jnp.sin(ang)
    n = pos.shape[0]
    one, zero = jnp.ones((n, ROPE_LANE0), F32), jnp.zeros((n, ROPE_LANE0), F32)
    z16, z32 = jnp.zeros((n, half), F32), jnp.zeros((n, LANES - ROPE_LANE0 - MLA_ROPE_DIM), F32)
    rc = jnp.concatenate([one, cos, cos, z32], 1)
    rs1 = jnp.concatenate([zero, -sin, z16, z32], 1)
    rs2 = jnp.concatenate([zero, z16, sin, z32], 1)
    return rc, rs1, rs2


def _group_patterns(tt):
    r = np.arange(tt)
    same = (r[:, None] // HG_GROUP) == (r[None, :] // HG_GROUP)
    lcum = same & (r[None, :] <= r[:, None])
    return jnp.asarray(lcum, BF16), jnp.asarray(same, BF16)


def _prepare(p):
    depth = p["w_in"].shape[0]
    w = {}
    w["ffn_w_gu"] = p["ffn_w_gu"].astype(BF16)
    w["ffn_w_down"] = p["ffn_w_down"].astype(BF16)
    w_in = p["w_in"]
    n_main = Z_CKV[1]
    zpad = lambda n: jnp.zeros((depth, D_MODEL, n), w_in.dtype)
    w["w_in"] = jnp.concatenate(
        [w_in[:, :, :n_main], zpad(ROPE_LANE0), w_in[:, :, n_main:], zpad(LANES - ROPE_LANE0 - MLA_ROPE_DIM)],
        -1).astype(BF16)
    w["w_uq"] = _pad_heads(p["mla_w_uq"], MLA_NOPE_DIM + MLA_ROPE_DIM).astype(BF16)
    w["w_uk"] = _pad_heads(p["mla_w_uk"], MLA_NOPE_DIM).astype(BF16)
    w["w_uv"] = _pad_heads(p["mla_w_uv"], MLA_V_DIM).astype(BF16)
    w["w_uv_t"] = jnp.swapaxes(p["mla_w_uv"], 1, 2).astype(BF16)
    w["q_norm"] = p["mla_q_norm"].reshape(depth, 1, MLA_Q_LORA)
    w["kv_norm"] = p["mla_kv_norm"].reshape(depth, 1, MLA_KV_LORA)
    w["w_glu"] = p["ssm_w_glu"].astype(BF16)
    w["n_ssm"] = p["norm_ssm"].reshape(depth, 1, SSM_WIDTH)
    w["n_hg"] = p["norm_hgrn"].reshape(depth, 1, HG_WIDTH)
    w["n_mla"] = jnp.broadcast_to(p["norm_mla"][:, :, None], (depth, MLA_HEADS * MLA_V_DIM, LANES))
    w_out = p["w_out"]
    split = SSM_WIDTH + HG_WIDTH
    w["w_out_a"] = w_out[:, :split].astype(BF16)
    w["w_out_b"] = w_out[:, split:].astype(BF16)
    w.update(_s5_tables(p["ssm_lambda_re"].astype(F32), p["ssm_lambda_im"].astype(F32),
                        p["ssm_b_re"].astype(F32), p["ssm_b_im"].astype(F32),
                        p["ssm_c_re"].astype(F32), p["ssm_c_im"].astype(F32),
                        p["ssm_d"].astype(F32), p["ssm_log_step"].astype(F32)))
    lane = np.arange(SSM_WIDTH)
    step_of, chan_of = lane // SSM_GROUP, lane % SSM_GROUP
    t_idx = np.arange(S5_CHUNK)[:, None, None]
    fwd = (step_of[None, None, :] == t_idx) & (chan_of[None, None, :] == chan_of[None, :, None])
    w["s5_sel_fwd"] = jnp.asarray(fwd, BF16)
    w["s5_sel_bwd"] = jnp.asarray(np.swapaxes(fwd, 1, 2), BF16)
    sm = jax.nn.softmax(p["hgrn_lb_logits"].astype(F32), axis=0)
    lb = jnp.clip(jnp.cumsum(sm, axis=0) - sm[0:1], 0.0, 1.0 - 1e-6)
    rows = jnp.stack([jnp.maximum(lb, LB_FLOOR), 1.0 - lb], 1)
    w["hg_lbp"] = jnp.pad(rows, ((0, 0), (0, 6), (0, 0)))
    head = np.arange(HG_WIDTH) // HG_DK
    same_head = head[:, None] == head[None, :]
    w["hg_e"] = jnp.asarray(same_head, BF16)
    w["hg_bd"] = jnp.asarray(same_head, F32)
    w["hg_e64"] = jnp.asarray(same_head / HG_DK, BF16)
    return w


def _hg_state_in(s):
    b = s.shape[0]
    eye = jnp.eye(HG_HEADS, dtype=s.dtype)
    return jnp.einsum('bhkv,hg->bhvgk', s, eye).reshape(b, HG_WIDTH, HG_WIDTH)


def _hg_state_out(st):
    b = st.shape[0]
    st = st.reshape(b, HG_HEADS, HG_DK, HG_HEADS, HG_DK)
    diag = jnp.stack([st[:, h, :, h, :] for h in range(HG_HEADS)], 1)
    return jnp.transpose(diag, (0, 1, 3, 2))


def _run_prompt(x, grp, w, p, rope):
    batch, seq = grp.batch, grp.seq
    depth = p["w_in"].shape[0]
    nc = seq // S5_CHUNK
    tt = min(256, seq)
    tq = min(1024, seq)
    ln_g = p["ln_g"].reshape(depth, 3, 1, D_MODEL)
    ln_b = p["ln_b"].reshape(depth, 3, 1, D_MODEL)
    x0 = jnp.zeros((SSM_GROUPS, batch, 2 * LANES), F32)
    s0 = jnp.zeros((batch, HG_WIDTH, HG_WIDTH), F32)
    ckv_l, kpe_l, sre_l, sim_l, hg_l = [], [], [], [], []
    for l in range(depth):
        x = _ffn_call(grp, x, l, 0, 0, w["ffn_w_gu"], w["ffn_w_down"], ln_g, ln_b)
        u, zh, qh, kh, vt, ckv, kpe = _mix_in_call(grp, x, l, w, rope)
        u_g = _s5_relayout_call(_s5_pack_kernel, u, w["s5_sel_fwd"], grp, True)
        y_g, xf = _s5_call(u_g, x0, w, l, batch, nc)
        y = _s5_relayout_call(_s5_unpack_kernel, y_g.reshape(u_g.shape), w["s5_sel_bwd"], grp, False)
        o_h, st = _hgrn_call(zh, s0, w, l, batch, seq, tt, tt)
        o_m = _flash_call(qh, kh, vt, batch, seq, tq)
        x = _mix_out_call(grp, x, l, y, zh, o_h, o_m, w, ln_g, ln_b)
        x = _ffn_call(grp, x, l, 1, 2, w["ffn_w_gu"], w["ffn_w_down"], ln_g, ln_b)
        ckv_l.append(ckv.reshape(batch, seq, MLA_KV_LORA))
        kpe_l.append(kpe.reshape(batch, seq, MLA_ROPE_DIM))
        xf = jnp.transpose(xf, (1, 0, 2))
        sre_l.append(xf[..., :SSM_STATE])
        sim_l.append(xf[..., SSM_STATE:])
        hg_l.append(_hg_state_out(st))
    return (x.reshape(batch, seq, D_MODEL), jnp.stack(ckv_l), jnp.stack(kpe_l), jnp.stack(sre_l),
            jnp.stack(sim_l), jnp.stack(hg_l))


def _run_sample(x, grp, w, p, rope, cache_ckv, cache_kpe, page_table, ssm_re, ssm_im, hg_state):
    batch = grp.batch
    depth = p["w_in"].shape[0]
    ln_g = p["ln_g"].reshape(depth, 3, 1, D_MODEL)
    ln_b = p["ln_b"].reshape(depth, 3, 1, D_MODEL)
    n = SSM_GROUPS * SSM_STATE
    lane = np.arange(MLA_PAD)
    sel = np.zeros((MLA_PAD, MLA_ROPE_DIM), np.float32)
    for i in range(MLA_ROPE_DIM):
        sel[(lane % HEAD_PAD) == ROPE_LANE0 + i, i] = 1.0
    sel = jnp.asarray(sel, BF16)
    hm = jnp.asarray((lane[None, :] // HEAD_PAD) == np.arange(MLA_HEADS)[:, None], F32)
    cache_kpe = jnp.swapaxes(cache_kpe, 2, 3)
    ckv_l, kpe_l, sre_l, sim_l, hg_l = [], [], [], [], []
    for l in range(depth):
        x = _ffn_call(grp, x, l, 0, 0, w["ffn_w_gu"], w["ffn_w_down"], ln_g, ln_b)
        u, zh, qh, kh, _, ckv, kpe = _mix_in_call(grp, x, l, w, rope)
        y, nr, ni = _s5_step_call(u, ssm_re[l].reshape(batch, n), ssm_im[l].reshape(batch, n), w, l)
        zh_pad = jnp.pad(zh[:, None, :], ((0, 0), (0, HG_GROUP - 1), (0, 0))).reshape(batch * HG_GROUP, -1)
        o_hp, st = _hgrn_call(zh_pad, _hg_state_in(hg_state[l]), w, l, batch, HG_GROUP, HG_GROUP, 1)
        o_h = o_hp.reshape(batch, HG_GROUP, HG_WIDTH)[:, 0]
        o_m = _decode_call(page_table, qh, ckv, kpe, cache_ckv, cache_kpe, w, l, sel, hm)
        o_m = o_m.reshape(batch, MLA_HEADS, HEAD_PAD)[:, :, :MLA_V_DIM].reshape(batch, -1).T[None]
        x = _mix_out_call(grp, x, l, y, zh, o_h, o_m, w, ln_g, ln_b)
        x = _ffn_call(grp, x, l, 1, 2, w["ffn_w_gu"], w["ffn_w_down"], ln_g, ln_b)
        ckv_l.append(ckv.reshape(batch, 1, MLA_KV_LORA))
        kpe_l.append(kpe.reshape(batch, 1, MLA_ROPE_DIM))
        sre_l.append(nr.reshape(batch, SSM_GROUPS, SSM_STATE))
        sim_l.append(ni.reshape(batch, SSM_GROUPS, SSM_STATE))
        hg_l.append(_hg_state_out(st))
    return (x.reshape(batch, 1, D_MODEL), jnp.stack(ckv_l), jnp.stack(kpe_l), jnp.stack(sre_l),
            jnp.stack(sim_l), jnp.stack(hg_l))


def kernel(x_prompt, x_sample, cache_kv_latent, cache_k_rope, state_ssm_re, state_ssm_im, state_hgrn, page_table, c_prompt, c_sample, w_ada, b_ada, ln_g, ln_b, ffn_w_gu, ffn_w_down, w_in, w_out, ssm_lambda_re, ssm_lambda_im, ssm_b_re, ssm_b_im, ssm_c_re, ssm_c_im, ssm_d, ssm_log_step, ssm_w_glu, norm_ssm, hgrn_lb_logits, norm_hgrn, mla_q_norm, mla_w_uq, mla_kv_norm, mla_w_uk, mla_w_uv, norm_mla):
    p = dict(ln_g=ln_g, ln_b=ln_b, ffn_w_gu=ffn_w_gu, ffn_w_down=ffn_w_down, w_in=w_in, w_out=w_out,
             ssm_lambda_re=ssm_lambda_re, ssm_lambda_im=ssm_lambda_im, ssm_b_re=ssm_b_re, ssm_b_im=ssm_b_im,
             ssm_c_re=ssm_c_re, ssm_c_im=ssm_c_im, ssm_d=ssm_d, ssm_log_step=ssm_log_step, ssm_w_glu=ssm_w_glu,
             norm_ssm=norm_ssm, hgrn_lb_logits=hgrn_lb_logits, norm_hgrn=norm_hgrn, mla_q_norm=mla_q_norm,
             mla_w_uq=mla_w_uq, mla_kv_norm=mla_kv_norm, mla_w_uk=mla_w_uk, mla_w_uv=mla_w_uv, norm_mla=norm_mla)
    depth = w_in.shape[0]
    assert depth == DEPTH_ and x_sample.shape[1] == 1
    bp, seq, _ = x_prompt.shape
    bs = x_sample.shape[0]
    past_len = page_table.shape[1] * PAGE_SIZE
    w = _prepare(p)
    w["hg_lcum"], w["hg_lall"] = {}, {}
    for tt in {min(256, seq), HG_GROUP}:
        w["hg_lcum"][tt], w["hg_lall"][tt] = _group_patterns(tt)

    mod = _ada_call(jnp.concatenate([c_prompt, c_sample], 0), w_ada, b_ada)
    mod = mod.reshape(depth, bp + bs, 9, D_MODEL)
    grp_p = _Group(mod[:, :bp], bp, seq, min(512, seq))
    grp_s = _Group(mod[:, bp:], bs, 1, bs)

    rope_p = _rope_tables(jnp.arange(seq))
    rope_s = _rope_tables(past_len + jnp.arange(1))
    y_p, ckv_p, kpe_p, sre_p, sim_p, hg_p = _run_prompt(x_prompt.reshape(bp * seq, D_MODEL), grp_p, w, p, rope_p)
    y_s, ckv_s, kpe_s, sre_s, sim_s, hg_s = _run_sample(
        x_sample.reshape(bs, D_MODEL), grp_s, w, p, rope_s, cache_kv_latent, cache_k_rope, page_table,
        state_ssm_re, state_ssm_im, state_hgrn)
    return (y_p, y_s, ckv_p, ckv_s, kpe_p, kpe_s, sre_p, sre_s, sim_p, sim_s, hg_p, hg_s)
```

```python
import functools
import math

import numpy as np
import jax
import jax.numpy as jnp
from jax import lax
from jax.experimental import pallas as pl
from jax.experimental.pallas import tpu as pltpu

F32 = jnp.float32
BF16 = jnp.bfloat16

D_MODEL = 1024
D_FF = 2816
SSM_WIDTH = 256
SSM_GROUP = 16
SSM_GROUPS = 16
SSM_STATE = 64
HG_WIDTH = 256
HG_HEADS = 4
HG_DK = 64
LB_FLOOR = 1e-30
MLA_HEADS = 8
MLA_V_DIM = 64
MLA_NOPE_DIM = 64
MLA_ROPE_DIM = 32
MLA_Q_LORA = 384
MLA_KV_LORA = 256
MLA_SCALE = (MLA_NOPE_DIM + MLA_ROPE_DIM) ** -0.5
Q_SCALE = MLA_SCALE * math.log2(math.e)
ROPE_THETA = 10000.0
NEG_MASK = -1e30
PAGE_SIZE = 128

LANES = 128
HEAD_PAD = LANES
MLA_PAD = MLA_HEADS * HEAD_PAD
N_IN_PAD = 2048
ROPE_LANE0 = MLA_NOPE_DIM
S5_CHUNK = 16
HG_GROUP = 16
VMEM_LIMIT = 56 * 1024 * 1024
DEPTH_ = 4
ALPHA = (2 * DEPTH_) ** 0.25


def _cparams(sem):
    return pltpu.CompilerParams(dimension_semantics=sem, vmem_limit_bytes=VMEM_LIMIT)


def _dot(a, b):
    return jnp.dot(a, b, preferred_element_type=F32)


def _dot_nt(a, b):
    return lax.dot_general(a, b, (((1,), (1,)), ((), ())), preferred_element_type=F32)


def _dot_tn(a, b):
    return lax.dot_general(a, b, (((0,), (0,)), ((), ())), preferred_element_type=F32)


def _split_dot(w, x, terms):
    acc = None
    r = x
    for _ in range(terms):
        p = r.astype(BF16)
        d = _dot(w, p)
        acc = d if acc is None else acc + d
        r = r - p.astype(F32)
    return acc


def _split_dot_r(x, w, terms):
    acc = None
    r = x
    for _ in range(terms):
        p = r.astype(BF16)
        d = _dot(p, w)
        acc = d if acc is None else acc + d
        r = r - p.astype(F32)
    return acc


def _layer_norm(y, g, b):
    mu = jnp.mean(y, -1, keepdims=True)
    d = y - mu
    var = jnp.mean(d * d, -1, keepdims=True)
    return d * lax.rsqrt(var + 1e-5) * g + b


def _sigmoid(x):
    return 1.0 / (1.0 + jnp.exp(-x))


def _silu(x):
    return x * _sigmoid(x)


def _gelu_tanh(x):
    c = math.sqrt(2.0 / math.pi)
    return 0.5 * x * (1.0 + jnp.tanh(c * (x + 0.044715 * (x * x * x))))


def _ada_kernel(c_ref, w_ref, b_ref, o_ref):
    c = c_ref[...]
    s = _silu(c).astype(BF16)
    o_ref[...] = _dot(s, w_ref[...].astype(BF16)) + b_ref[...]


def _ada_call(c_all, w_ada, b_ada):
    rows = c_all.shape[0]
    depth, d, n = w_ada.shape
    tn = 1152
    return pl.pallas_call(
        _ada_kernel,
        grid=(depth, n // tn),
        in_specs=[pl.BlockSpec((rows, d), lambda l, j: (0, 0)),
                  pl.BlockSpec((None, d, tn), lambda l, j: (l, 0, j)),
                  pl.BlockSpec((None, 1, tn), lambda l, j: (l, 0, j))],
        out_specs=pl.BlockSpec((None, rows, tn), lambda l, j: (l, 0, j)),
        out_shape=jax.ShapeDtypeStruct((depth, rows, n), F32),
        compiler_params=_cparams(("parallel", "parallel")),
        name="ada",
    )(c_all, w_ada, b_ada.reshape(depth, 1, n))


class _Group:
    def __init__(self, mod, batch, seq, tm):
        self.batch, self.seq, self.tm = batch, seq, tm
        self.rows = batch * seq
        self.per_row = seq == 1
        depth = mod.shape[0]
        if self.per_row:
            self.mod = jnp.transpose(mod, (0, 2, 1, 3))
        else:
            self.mod = mod.reshape(depth, batch, 9, 1, D_MODEL)
            self.tiles_per_batch = seq // tm

    def mod_spec(self, l, j):
        if self.per_row:
            return pl.BlockSpec((None, None, self.tm, D_MODEL), lambda i: (l, j, i, 0))
        tpb = self.tiles_per_batch
        return pl.BlockSpec((None, None, None, 1, D_MODEL), lambda i: (l, i // tpb, j, 0, 0))

    def pos_spec(self, width):
        if self.per_row:
            return pl.BlockSpec((1, width), lambda i: (0, 0))
        tpb = self.tiles_per_batch
        return pl.BlockSpec((self.tm, width), lambda i: (i % tpb, 0))

    def row_spec(self, width, col=0):
        return pl.BlockSpec((self.tm, width), lambda i: (i, col))

    def tcol_shape(self, height):
        return (1, height, self.batch) if self.per_row else (self.batch, height, self.seq)

    def tcol_spec(self, height):
        if self.per_row:
            return pl.BlockSpec((None, height, self.tm), lambda i: (0, 0, 0))
        tpb = self.tiles_per_batch
        return pl.BlockSpec((None, height, self.tm), lambda i: (i // tpb, 0, i % tpb))

    @property
    def grid(self):
        return (self.rows // self.tm,)


def _const_spec(shape, index):
    return pl.BlockSpec(shape, lambda i: index)


FF_CHUNK = 256


def _ffn_kernel(x_ref, sh_ref, sc_ref, gt_ref, wgu_ref, wd_ref, lg_ref, lb_ref, o_ref, a_ref):
    x = x_ref[...]
    h = (x * (1.0 + sc_ref[...]) + sh_ref[...]).astype(BF16)
    for c in range(D_FF // FF_CHUNK):
        lo = c * FF_CHUNK
        g = _dot(h, wgu_ref[:, lo:lo + FF_CHUNK])
        u = _dot(h, wgu_ref[:, D_FF + lo:D_FF + lo + FF_CHUNK])
        a_ref[:, lo:lo + FF_CHUNK] = (_silu(g) * u).astype(BF16)
    f = _dot(a_ref[...], wd_ref[...])
    y = ALPHA * x + 0.5 * (1.0 + gt_ref[...]) * f
    o_ref[...] = _layer_norm(y, lg_ref[...], lb_ref[...])


def _ffn_call(grp, x, l, which, mod_idx, wgu, wd, ln_g, ln_b):
    tm = grp.tm
    return pl.pallas_call(
        _ffn_kernel,
        grid=grp.grid,
        in_specs=[grp.row_spec(D_MODEL),
                  grp.mod_spec(l, 3 * mod_idx), grp.mod_spec(l, 3 * mod_idx + 1), grp.mod_spec(l, 3 * mod_idx + 2),
                  _const_spec((None, None, D_MODEL, 2 * D_FF), (l, which, 0, 0)),
                  _const_spec((None, None, D_FF, D_MODEL), (l, which, 0, 0)),
                  _const_spec((None, None, 1, D_MODEL), (l, mod_idx, 0, 0)),
                  _const_spec((None, None, 1, D_MODEL), (l, mod_idx, 0, 0))],
        out_specs=grp.row_spec(D_MODEL),
        out_shape=jax.ShapeDtypeStruct((grp.rows, D_MODEL), F32),
        scratch_shapes=[pltpu.VMEM((tm, D_FF), BF16)],
        compiler_params=_cparams(("parallel",)),
        name="ffn",
    )(x, grp.mod, grp.mod, grp.mod, wgu, wd, ln_g, ln_b)


Z_U = (0, 256)
Z_H = (256, 1280)
Z_CQ = (1280, 1664)
Z_CKV = (1664, 1920)
Z_KPE = (1920, 2048)


def _rope_group(t, c, s1, s2):
    return t * c + pltpu.roll(t, LANES - MLA_ROPE_DIM // 2, 1) * s1 + pltpu.roll(t, MLA_ROPE_DIM // 2, 1) * s2


def _mix_in_kernel(x_ref, sh_ref, sc_ref, win_ref, qn_ref, wuq_ref, kvn_ref, wuk_ref, wuv_ref,
                   rc_ref, rs1_ref, rs2_ref,
                   u_ref, zh_ref, qh_ref, kh_ref, vt_ref, ckv_ref, kpe_ref):
    x = x_ref[...]
    h = (x * (1.0 + sc_ref[...]) + sh_ref[...]).astype(BF16)
    u_ref[...] = _dot(h, win_ref[:, Z_U[0]:Z_U[1]])
    zh_ref[...] = _dot(h, win_ref[:, Z_H[0]:Z_H[1]])
    rc, rs1, rs2 = rc_ref[...], rs1_ref[...], rs2_ref[...]

    zcq = _dot(h, win_ref[:, Z_CQ[0]:Z_CQ[1]])
    cq = zcq * lax.rsqrt(jnp.mean(zcq * zcq, -1, keepdims=True) + 1e-6) * qn_ref[...]
    q = _dot(cq.astype(BF16), wuq_ref[...]) * Q_SCALE
    for hd in range(MLA_HEADS):
        lo = hd * HEAD_PAD
        qh_ref[:, lo:lo + HEAD_PAD] = _rope_group(q[:, lo:lo + HEAD_PAD], rc, rs1, rs2).astype(BF16)

    zckv = _dot(h, win_ref[:, Z_CKV[0]:Z_CKV[1]])
    ckv = zckv * lax.rsqrt(jnp.mean(zckv * zckv, -1, keepdims=True) + 1e-6) * kvn_ref[...]
    ckv_ref[...] = ckv
    ckv_b = ckv.astype(BF16)
    kpe = _rope_group(_dot(h, win_ref[:, Z_KPE[0]:Z_KPE[1]]), rc, rs1, rs2)
    kpe_ref[...] = pltpu.roll(kpe, LANES - ROPE_LANE0, 1)[:, :MLA_ROPE_DIM]
    kn = _dot(ckv_b, wuk_ref[...])
    for hd in range(MLA_HEADS):
        lo = hd * HEAD_PAD
        kh_ref[:, lo:lo + HEAD_PAD] = (kn[:, lo:lo + HEAD_PAD] + kpe).astype(BF16)
    vt_ref[...] = _dot_nt(wuv_ref[...], ckv_b).astype(BF16)


def _mix_in_call(grp, x, l, w, rope):
    rows = grp.rows
    v_rows = MLA_HEADS * MLA_V_DIM
    outs = [((rows, SSM_WIDTH), F32), ((rows, 4 * HG_WIDTH), F32), ((rows, MLA_PAD), BF16),
            ((rows, MLA_PAD), BF16), (grp.tcol_shape(v_rows), BF16), ((rows, MLA_KV_LORA), F32),
            ((rows, MLA_ROPE_DIM), F32)]
    out_specs = [grp.row_spec(s[1]) for s, _ in outs]
    out_specs[4] = grp.tcol_spec(v_rows)
    return pl.pallas_call(
        _mix_in_kernel,
        grid=grp.grid,
        in_specs=[grp.row_spec(D_MODEL), grp.mod_spec(l, 3), grp.mod_spec(l, 4),
                  _const_spec((None, D_MODEL, N_IN_PAD), (l, 0, 0)),
                  _const_spec((None, 1, MLA_Q_LORA), (l, 0, 0)),
                  _const_spec((None, MLA_Q_LORA, MLA_PAD), (l, 0, 0)),
                  _const_spec((None, 1, MLA_KV_LORA), (l, 0, 0)),
                  _const_spec((None, MLA_KV_LORA, MLA_PAD), (l, 0, 0)),
                  _const_spec((None, v_rows, MLA_KV_LORA), (l, 0, 0)),
                  grp.pos_spec(LANES), grp.pos_spec(LANES), grp.pos_spec(LANES)],
        out_specs=out_specs,
        out_shape=[jax.ShapeDtypeStruct(s, d) for s, d in outs],
        compiler_params=_cparams(("parallel",)),
        name="mix_in",
    )(x, grp.mod, grp.mod, w["w_in"], w["q_norm"], w["w_uq"], w["kv_norm"], w["w_uk"], w["w_uv_t"], *rope)


def _s5_kernel(u_ref, m_ref, wis_ref, wso_ref, av_ref, d_ref, x0_ref, y_ref, xf_ref, sv_scr, sw_scr, x_scr,
               *, nb, nc):
    u = u_ref[...]
    ub = u.astype(BF16)
    sv_scr[...] = _dot(ub, wis_ref[:, 0:LANES])
    sw_scr[...] = _dot(ub, wis_ref[:, LANES:2 * LANES])
    a1, a2, a3 = av_ref[0:1, :], av_ref[1:2, :], av_ref[2:3, :]

    def body(c, carry):
        v, w = carry
        rows_c = pl.ds(c, nb, stride=nc)
        x_scr[rows_c, :] = v
        return a1 * v + a2 * w + sv_scr[rows_c, :], a1 * w + a3 * v + sw_scr[rows_c, :]

    v, _ = lax.fori_loop(0, nc, body, (x0_ref[:, 0:LANES], x0_ref[:, LANES:2 * LANES]), unroll=8)
    xf_ref[...] = v
    y_ref[...] = _dot(ub, m_ref[...]) + _dot(x_scr[...].astype(BF16), wso_ref[...]) + u * d_ref[...]


def _s5_pack_kernel(u_ref, sel_ref, o_ref, lo_scr, hi_scr, *, nch):
    g = SSM_GROUPS
    lo_scr[...] = u_ref[:, 0:LANES]
    hi_scr[...] = u_ref[:, LANES:2 * LANES]
    lane_grp = lax.broadcasted_iota(jnp.int32, (g * nch, SSM_WIDTH), 1) // SSM_GROUP
    row_grp = lax.broadcasted_iota(jnp.int32, (g * nch, SSM_WIDTH), 0) // nch
    keep = lane_grp == row_grp
    acc = jnp.zeros((g * nch, SSM_WIDTH), F32)
    for t in range(S5_CHUNK):
        step_rows = pl.ds(t, nch, stride=S5_CHUNK)
        rows = jnp.concatenate([lo_scr[step_rows, :], hi_scr[step_rows, :]], 1)
        lhs = jnp.where(keep, jnp.tile(rows, (g, 1)), 0.0).astype(BF16)
        acc = acc + _dot(lhs, sel_ref[t])
    o_ref[...] = acc.reshape(g, nch, SSM_WIDTH).astype(o_ref.dtype)


def _s5_unpack_kernel(y_ref, sel_ref, o_ref, lo_scr, hi_scr, *, nch):
    g = SSM_GROUPS
    lane_grp = lax.broadcasted_iota(jnp.int32, (g * nch, SSM_WIDTH), 1) // SSM_GROUP
    row_grp = lax.broadcasted_iota(jnp.int32, (g * nch, SSM_WIDTH), 0) // nch
    keep = lane_grp == row_grp
    yb = y_ref[...].reshape(g * nch, SSM_WIDTH).astype(BF16)
    for t in range(S5_CHUNK):
        z = jnp.where(keep, _dot(yb, sel_ref[t]), 0.0)
        tok = jnp.sum(z.reshape(g, nch, SSM_WIDTH), 0)
        step_rows = pl.ds(t, nch, stride=S5_CHUNK)
        lo_scr[step_rows, :] = tok[:, 0:LANES]
        hi_scr[step_rows, :] = tok[:, LANES:2 * LANES]
    o_ref[...] = jnp.concatenate([lo_scr[...], hi_scr[...]], 1)


def _s5_relayout_call(kern, x, sel, grp, to_groups):
    nch = grp.tm // S5_CHUNK
    tpb = grp.tiles_per_batch
    tok_spec = grp.row_spec(SSM_WIDTH)
    grp_spec = pl.BlockSpec((SSM_GROUPS, None, nch, SSM_WIDTH), lambda i: (0, i // tpb, i % tpb, 0))
    grp_shape = (SSM_GROUPS, grp.batch, grp.seq // S5_CHUNK, SSM_WIDTH)
    return pl.pallas_call(
        functools.partial(kern, nch=nch),
        grid=grp.grid,
        in_specs=[tok_spec if to_groups else grp_spec,
                  _const_spec((S5_CHUNK, SSM_WIDTH, SSM_WIDTH), (0, 0, 0))],
        out_specs=grp_spec if to_groups else tok_spec,
        out_shape=jax.ShapeDtypeStruct(grp_shape, BF16) if to_groups
        else jax.ShapeDtypeStruct((grp.rows, SSM_WIDTH), F32),
        scratch_shapes=[pltpu.VMEM((grp.tm, LANES), F32)] * 2,
        compiler_params=_cparams(("parallel",)),
        name="s5_pack" if to_groups else "s5_unpack",
    )(x, sel)


def _s5_call(u_g, x0, w, l, nb, nc):
    g = SSM_GROUPS
    rows = nb * nc
    wide = S5_CHUNK * SSM_GROUP
    u_g = u_g.reshape(g, rows, wide)
    return pl.pallas_call(
        functools.partial(_s5_kernel, nb=nb, nc=nc),
        grid=(g,),
        in_specs=[pl.BlockSpec((None, rows, wide), lambda i: (i, 0, 0)),
                  pl.BlockSpec((None, None, wide, wide), lambda i: (l, i, 0, 0)),
                  pl.BlockSpec((None, None, wide, wide), lambda i: (l, i, 0, 0)),
                  pl.BlockSpec((None, None, LANES, wide), lambda i: (l, i, 0, 0)),
                  pl.BlockSpec((None, None, 8, LANES), lambda i: (l, i, 0, 0)),
                  pl.BlockSpec((None, None, 1, wide), lambda i: (l, i, 0, 0)),
                  pl.BlockSpec((None, nb, wide), lambda i: (i, 0, 0))],
        out_specs=[pl.BlockSpec((None, rows, wide), lambda i: (i, 0, 0)),
                   pl.BlockSpec((None, nb, LANES), lambda i: (i, 0, 0))],
        out_shape=[jax.ShapeDtypeStruct((g, rows, wide), F32), jax.ShapeDtypeStruct((g, nb, LANES), F32)],
        scratch_shapes=[pltpu.VMEM((rows, LANES), F32)] * 3,
        compiler_params=_cparams(("parallel",)),
        name="s5_chunks",
    )(u_g, w["s5_m"], w["s5_wis"], w["s5_wso"], w["s5_av"], w["s5_d"], x0)


def _s5_step_kernel(u_ref, xr_ref, xi_ref, bb_ref, cc_ref, ar_ref, ai_ref, d_ref, y_ref, or_ref, oi_ref):
    u = u_ref[...]
    n = SSM_GROUPS * SSM_STATE
    bu = _dot(u.astype(BF16), bb_ref[...])
    xr, xi, ar, ai = xr_ref[...], xi_ref[...], ar_ref[...], ai_ref[...]
    nr = ar * xr - ai * xi + bu[:, :n]
    ni = ar * xi + ai * xr + bu[:, n:]
    or_ref[...] = nr
    oi_ref[...] = ni
    y_ref[...] = (_dot(nr.astype(BF16), cc_ref[0:n, :]) + _dot(ni.astype(BF16), cc_ref[n:2 * n, :])
                  + u * d_ref[...])


def _s5_step_call(u, xr, xi, w, l):
    b = u.shape[0]
    n = SSM_GROUPS * SSM_STATE
    full = lambda shape: pl.BlockSpec(shape, lambda i: (0,) * len(shape))
    lsel = lambda shape: pl.BlockSpec((None,) + shape, lambda i: (l,) + (0,) * len(shape))
    return pl.pallas_call(
        _s5_step_kernel,
        grid=(1,),
        in_specs=[full((b, SSM_WIDTH)), full((b, n)), full((b, n)),
                  lsel((SSM_WIDTH, 2 * n)), lsel((2 * n, SSM_WIDTH)), lsel((1, n)), lsel((1, n)),
                  lsel((1, SSM_WIDTH))],
        out_specs=[full((b, SSM_WIDTH)), full((b, n)), full((b, n))],
        out_shape=[jax.ShapeDtypeStruct((b, SSM_WIDTH), F32), jax.ShapeDtypeStruct((b, n), F32),
                   jax.ShapeDtypeStruct((b, n), F32)],
        compiler_params=_cparams(("arbitrary",)),
        name="s5_step",
    )(u, xr, xi, w["s5_bb"], w["s5_cc"], w["s5_ar"], w["s5_ai"], w["s5_dflat"])


def _hgrn_kernel(zh_ref, s0_ref, lbp_ref, e_ref, bd_ref, lcum_ref, lall_ref, o_ref, sf_ref,
                 perm_scr, acc_scr, st_ref, oi_ref, *, tt, n_valid):
    j = pl.program_id(1)

    @pl.when(j == 0)
    def _():
        st_ref[...] = s0_ref[...]

    w = HG_WIDTH
    ng = tt // HG_GROUP
    zf = zh_ref[:, 0:w]
    q = zh_ref[:, w:2 * w]
    v = zh_ref[:, 2 * w:3 * w]
    lbm, oml = lbp_ref[0:1, :], lbp_ref[1:2, :]
    sig = 1.0 / (1.0 + jnp.exp(-zf))
    f = lbm + oml * sig
    k = oml * (1.0 - sig)
    if n_valid < tt:
        live = lax.broadcasted_iota(jnp.int32, (tt, w), 0) < n_valid
        f = jnp.where(live, f, 1.0)
        k = jnp.where(live, k, 0.0)
    lf = jnp.log(f)

    def regroup(i, x):
        if ng == 1:
            return x
        perm_scr[2 * i] = x[:, 0:LANES]
        perm_scr[2 * i + 1] = x[:, LANES:2 * LANES]
        blocks = []
        for pos in range(HG_GROUP):
            rows = pl.ds(pos, ng, stride=HG_GROUP)
            blocks.append(jnp.concatenate([perm_scr[2 * i, rows, :], perm_scr[2 * i + 1, rows, :]], 1))
        return jnp.concatenate(blocks, 0)

    qg, kg, vg, fg = regroup(0, q), regroup(1, k), regroup(2, v), regroup(3, f)
    e = e_ref[...]
    acc_scr[...] = _dot((qg * kg).astype(BF16), e) * vg
    decay = None
    for d in range(1, min(HG_GROUP, n_valid)):
        n_rows = (HG_GROUP - d) * ng
        f_blk = fg[ng:ng + n_rows]
        decay = f_blk if decay is None else decay[ng:] * f_blk
        p = qg[d * ng:] * kg[:n_rows] * decay
        acc_scr[d * ng:, :] += _dot(p.astype(BF16), e) * vg[:n_rows]
    if ng == 1:
        acc = acc_scr[...]
    else:
        for pos in range(HG_GROUP):
            rows = pl.ds(pos, ng, stride=HG_GROUP)
            perm_scr[0, rows, :] = acc_scr[pos * ng:(pos + 1) * ng, 0:LANES]
            perm_scr[1, rows, :] = acc_scr[pos * ng:(pos + 1) * ng, LANES:2 * LANES]
        acc = jnp.concatenate([perm_scr[0], perm_scr[1]], 1)

    b = _split_dot(lcum_ref[...], lf, 3)
    bl = _split_dot(lall_ref[...], lf, 3)
    qe = (q * jnp.exp(b)).astype(BF16)
    kx = (k * jnp.exp(bl - b)).astype(BF16)
    ebl = jnp.exp(bl)
    vb = v.astype(BF16)
    bd = bd_ref[...]
    for g in range(tt // HG_GROUP):
        lo = g * HG_GROUP
        st = st_ref[...]
        oi_ref[lo:lo + HG_GROUP, :] = _dot_nt(qe[lo:lo + HG_GROUP], st.astype(BF16))
        upd = _dot_tn(vb[lo:lo + HG_GROUP], kx[lo:lo + HG_GROUP])
        st_ref[...] = st * ebl[lo:lo + 1, :] + upd * bd
    o_ref[...] = acc + oi_ref[...]

    @pl.when(j == pl.num_programs(1) - 1)
    def _():
        sf_ref[...] = st_ref[...]


def _hgrn_call(zh, s0t, w, l, batch, seq_rows, tt, n_valid):
    wd = HG_WIDTH
    nj = seq_rows // tt
    return pl.pallas_call(
        functools.partial(_hgrn_kernel, tt=tt, n_valid=n_valid),
        grid=(batch, nj),
        in_specs=[pl.BlockSpec((tt, 4 * wd), lambda b, j: (b * nj + j, 0)),
                  pl.BlockSpec((None, wd, wd), lambda b, j: (b, 0, 0)),
                  pl.BlockSpec((None, 8, wd), lambda b, j: (l, 0, 0)),
                  pl.BlockSpec((wd, wd), lambda b, j: (0, 0)),
                  pl.BlockSpec((wd, wd), lambda b, j: (0, 0)),
                  pl.BlockSpec((tt, tt), lambda b, j: (0, 0)),
                  pl.BlockSpec((tt, tt), lambda b, j: (0, 0))],
        out_specs=[pl.BlockSpec((tt, wd), lambda b, j: (b * nj + j, 0)),
                   pl.BlockSpec((None, wd, wd), lambda b, j: (b, 0, 0))],
        out_shape=[jax.ShapeDtypeStruct((batch * seq_rows, wd), F32),
                   jax.ShapeDtypeStruct((batch, wd, wd), F32)],
        scratch_shapes=[pltpu.VMEM((8, tt, LANES), F32), pltpu.VMEM((tt, wd), F32),
                        pltpu.VMEM((wd, wd), F32), pltpu.VMEM((tt, wd), F32)],
        compiler_params=_cparams(("parallel", "arbitrary")),
        name="hgrn",
    )(zh, s0t, w["hg_lbp"], w["hg_e"], w["hg_bd"], w["hg_lcum"][tt], w["hg_lall"][tt])


def _flash_kernel(q_ref, k_ref, vt_ref, o_ref, m_ref, l_ref, acc_ref):
    qi = pl.program_id(1)
    ki = pl.program_id(2)

    @pl.when(ki == 0)
    def _():
        m_ref[...] = jnp.full(m_ref.shape, NEG_MASK, F32)
        l_ref[...] = jnp.zeros(l_ref.shape, F32)
        acc_ref[...] = jnp.zeros(acc_ref.shape, F32)

    tq = q_ref.shape[0]

    def step(keys, qrys, diagonal):
        for hd in range(MLA_HEADS):
            sl = slice(hd * HEAD_PAD, (hd + 1) * HEAD_PAD)
            vs = slice(hd * MLA_V_DIM, (hd + 1) * MLA_V_DIM)
            st = _dot_nt(k_ref[keys, sl], q_ref[qrys, sl])
            if diagonal:
                key = lax.broadcasted_iota(jnp.int32, st.shape, 0)
                qry = lax.broadcasted_iota(jnp.int32, st.shape, 1)
                st = jnp.where(key <= qry, st, NEG_MASK)
            m_prev = m_ref[hd:hd + 1, qrys]
            m_new = jnp.maximum(m_prev, jnp.max(st, 0, keepdims=True))
            alpha = jnp.exp2(m_prev - m_new)
            p = jnp.exp2(st - m_new)
            l_ref[hd:hd + 1, qrys] = alpha * l_ref[hd:hd + 1, qrys] + jnp.sum(p, 0, keepdims=True)
            acc_ref[vs, qrys] = alpha * acc_ref[vs, qrys] + _dot(vt_ref[vs, keys], p.astype(BF16))
            m_ref[hd:hd + 1, qrys] = m_new

    @pl.when(ki < qi)
    def _():
        step(slice(0, tq), slice(0, tq), False)

    @pl.when(ki == qi)
    def _():
        step(slice(0, tq), slice(0, tq), True)
        for hd in range(MLA_HEADS):
            vs = slice(hd * MLA_V_DIM, (hd + 1) * MLA_V_DIM)
            o_ref[vs, :] = acc_ref[vs, :] / l_ref[hd:hd + 1, :]


def _flash_call(qh, kh, vt, batch, seq, tq):
    nq = seq // tq
    v_rows = MLA_HEADS * MLA_V_DIM
    q_spec = pl.BlockSpec((None, tq, MLA_PAD), lambda b, qi, ki: (b, qi, 0))
    k_spec = pl.BlockSpec((None, tq, MLA_PAD), lambda b, qi, ki: (b, jnp.minimum(ki, qi), 0))
    vt_spec = pl.BlockSpec((None, v_rows, tq), lambda b, qi, ki: (b, 0, jnp.minimum(ki, qi)))
    shp = (batch, seq, MLA_PAD)
    return pl.pallas_call(
        _flash_kernel,
        grid=(batch, nq, nq),
        in_specs=[q_spec, k_spec, vt_spec],
        out_specs=pl.BlockSpec((None, v_rows, tq), lambda b, qi, ki: (b, 0, qi)),
        out_shape=jax.ShapeDtypeStruct((batch, v_rows, seq), F32),
        scratch_shapes=[pltpu.VMEM((MLA_HEADS, tq), F32), pltpu.VMEM((MLA_HEADS, tq), F32),
                        pltpu.VMEM((v_rows, tq), F32)],
        compiler_params=_cparams(("parallel", "parallel", "arbitrary")),
        name="flash",
    )(qh.reshape(shp), kh.reshape(shp), vt)


PAGES_PER_STEP = 64


def _decode_kernel(pt_ref, qh_ref, ckvn_ref, kpen_ref, wuk_ref, wuv_ref, sel_ref, hm_ref, ckv_hbm, kpe_hbm,
                   o_ref, ckv_buf, kpe_buf, sem_c, sem_k, *, layer, npg, ng):
    b = pl.program_id(0)
    hm = hm_ref[...]

    def copies(seq_i, grp_i, slot):
        out = []
        for i in range(npg):
            page = pt_ref[seq_i, grp_i * npg + i]
            out.append(pltpu.make_async_copy(ckv_hbm.at[layer, page], ckv_buf.at[slot, i], sem_c.at[slot]))
            out.append(pltpu.make_async_copy(kpe_hbm.at[layer, page], kpe_buf.at[slot, i], sem_k.at[slot]))
        return out

    @pl.when(b == 0)
    def _():
        for c in copies(0, 0, 0):
            c.start()

    qbd = jnp.where(hm > 0, jnp.broadcast_to(qh_ref[...], hm.shape), jnp.zeros_like(hm)).astype(BF16)
    qa = _dot_nt(qbd, wuk_ref[...]).astype(BF16)
    qp = _dot(qbd, sel_ref[...]).astype(BF16)
    m = jnp.full((MLA_HEADS, 1), NEG_MASK, F32)
    l = jnp.zeros((MLA_HEADS, 1), F32)
    acc = jnp.zeros((MLA_HEADS, MLA_KV_LORA), F32)
    for j in range(ng):
        slot = lax.rem(b * ng + j, 2)
        if j + 1 < ng:
            for c in copies(b, j + 1, 1 - slot):
                c.start()
        else:
            @pl.when(b + 1 < pl.num_programs(0))
            def _():
                for c in copies(b + 1, 0, 1 - slot):
                    c.start()
        for c in copies(b, j, slot):
            c.wait()
        pages = [ckv_buf[slot, i].astype(BF16) for i in range(npg)]
        s = jnp.concatenate(
            [_dot_nt(qa, pg) + _dot(qp, kpe_buf[slot, i].astype(BF16)) for i, pg in enumerate(pages)], axis=1)
        m_new = jnp.maximum(m, jnp.max(s, -1, keepdims=True))
        alpha = jnp.exp2(m - m_new)
        p = jnp.exp2(s - m_new)
        l = alpha * l + jnp.sum(p, -1, keepdims=True)
        pb = p.astype(BF16)
        parts = [None, None]
        for i, pg in enumerate(pages):
            d = _dot(pb[:, i * PAGE_SIZE:(i + 1) * PAGE_SIZE], pg)
            parts[i % 2] = d if parts[i % 2] is None else parts[i % 2] + d
        acc = alpha * acc + (parts[0] if parts[1] is None else parts[0] + parts[1])
        m = m_new

    ckvn = ckvn_ref[...]
    s_new = (jnp.sum(qa.astype(F32) * ckvn.astype(BF16).astype(F32), -1, keepdims=True)
             + jnp.sum(qp.astype(F32) * kpen_ref[...].astype(BF16).astype(F32), -1, keepdims=True))
    m_fin = jnp.maximum(m, s_new)
    a = jnp.exp2(m - m_fin)
    p_new = jnp.exp2(s_new - m_fin)
    l_fin = a * l + p_new
    lat = a * acc + p_new.astype(BF16).astype(F32) * ckvn.astype(BF16).astype(F32)
    lat = lat / l_fin
    o = _dot(lat.astype(BF16), wuv_ref[...])
    o_ref[...] = jnp.sum(jnp.where(hm > 0, o, 0.0), 0, keepdims=True)


def _decode_call(page_table, qh, ckv_new, kpe_new, cache_ckv, cache_kpe, w, l, sel, hm):
    batch, n_pages = page_table.shape
    npg = min(PAGES_PER_STEP, n_pages)
    ng = n_pages // npg
    row3 = lambda width: pl.BlockSpec((None, 1, width), lambda b, pt: (b, 0, 0))
    wsel = lambda shape: pl.BlockSpec((None,) + shape, lambda b, pt: (l,) + (0,) * len(shape))
    full = lambda shape: pl.BlockSpec(shape, lambda b, pt: (0,) * len(shape))
    hbm = pl.BlockSpec(memory_space=pl.ANY)
    grid_spec = pltpu.PrefetchScalarGridSpec(
        num_scalar_prefetch=1,
        grid=(batch,),
        in_specs=[row3(MLA_PAD), row3(MLA_KV_LORA), row3(MLA_ROPE_DIM),
                  wsel((MLA_KV_LORA, MLA_PAD)), wsel((MLA_KV_LORA, MLA_PAD)),
                  full((MLA_PAD, MLA_ROPE_DIM)), full((MLA_HEADS, MLA_PAD)), hbm, hbm],
        out_specs=row3(MLA_PAD),
        scratch_shapes=[pltpu.VMEM((2, npg, PAGE_SIZE, MLA_KV_LORA), cache_ckv.dtype),
                        pltpu.VMEM((2, npg, MLA_ROPE_DIM, PAGE_SIZE), cache_kpe.dtype),
                        pltpu.SemaphoreType.DMA((2,)), pltpu.SemaphoreType.DMA((2,))],
    )
    out = pl.pallas_call(
        functools.partial(_decode_kernel, layer=l, npg=npg, ng=ng),
        grid_spec=grid_spec,
        out_shape=jax.ShapeDtypeStruct((batch, 1, MLA_PAD), F32),
        compiler_params=_cparams(("arbitrary",)),
        name="decode",
    )(page_table, qh.reshape(batch, 1, MLA_PAD), ckv_new.reshape(batch, 1, MLA_KV_LORA),
      kpe_new.reshape(batch, 1, MLA_ROPE_DIM), w["w_uk"], w["w_uv"], sel, hm, cache_ckv, cache_kpe)
    return out.reshape(batch, MLA_PAD)


def _mix_out_kernel(x_ref, gt_ref, y_ref, zg_ref, oh_ref, om_ref, wglu_ref, nssm_ref, nhg_ref, nmla_ref,
                    e64_ref, woa_ref, wob_ref, lg_ref, lb_ref, o_ref, cat_ref):
    w = SSM_WIDTH
    gab = _dot(_gelu_tanh(y_ref[...]).astype(BF16), wglu_ref[...])
    t = gab[:, :w] * _sigmoid(gab[:, w:])
    cat_ref[:, 0:w] = (t * lax.rsqrt(jnp.mean(t * t, -1, keepdims=True) + 1e-6) * nssm_ref[...]).astype(BF16)

    oh = oh_ref[...]
    msq = _split_dot_r(oh * oh, e64_ref[...], 2)
    cat_ref[:, w:2 * w] = (oh * lax.rsqrt(msq + 1e-6) * nhg_ref[...] * _silu(zg_ref[...])).astype(BF16)

    om = om_ref[...]
    tm = om.shape[1]
    nm = nmla_ref[...]
    nm = nm[:, :tm] if tm <= LANES else jnp.tile(nm, (1, tm // LANES))
    ms = jnp.mean(om * om, 0, keepdims=True)
    o_mla = (om * lax.rsqrt(ms + 1e-6) * nm).astype(BF16)

    mix = _dot(cat_ref[...], woa_ref[...]) + _dot_tn(o_mla, wob_ref[...])
    y = ALPHA * x_ref[...] + (1.0 + gt_ref[...]) * mix
    o_ref[...] = _layer_norm(y, lg_ref[...], lb_ref[...])


def _mix_out_call(grp, x, l, y_ssm, zh, o_h, o_m, w, ln_g, ln_b):
    tm = grp.tm
    v_rows = MLA_HEADS * MLA_V_DIM
    return pl.pallas_call(
        _mix_out_kernel,
        grid=grp.grid,
        in_specs=[grp.row_spec(D_MODEL), grp.mod_spec(l, 5),
                  grp.row_spec(SSM_WIDTH), grp.row_spec(HG_WIDTH, col=3), grp.row_spec(HG_WIDTH),
                  grp.tcol_spec(v_rows),
                  _const_spec((None, SSM_WIDTH, 2 * SSM_WIDTH), (l, 0, 0)),
                  _const_spec((None, 1, SSM_WIDTH), (l, 0, 0)),
                  _const_spec((None, 1, HG_WIDTH), (l, 0, 0)),
                  _const_spec((None, v_rows, LANES), (l, 0, 0)),
                  _const_spec((HG_WIDTH, HG_WIDTH), (0, 0)),
                  _const_spec((None, SSM_WIDTH + HG_WIDTH, D_MODEL), (l, 0, 0)),
                  _const_spec((None, v_rows, D_MODEL), (l, 0, 0)),
                  _const_spec((None, None, 1, D_MODEL), (l, 1, 0, 0)),
                  _const_spec((None, None, 1, D_MODEL), (l, 1, 0, 0))],
        out_specs=grp.row_spec(D_MODEL),
        out_shape=jax.ShapeDtypeStruct((grp.rows, D_MODEL), F32),
        scratch_shapes=[pltpu.VMEM((tm, SSM_WIDTH + HG_WIDTH), BF16)],
        compiler_params=_cparams(("parallel",)),
        name="mix_out",
    )(x, grp.mod, y_ssm, zh, o_h, o_m, w["w_glu"], w["n_ssm"], w["n_hg"], w["n_mla"], w["hg_e64"],
      w["w_out_a"], w["w_out_b"], ln_g, ln_b)


def _pad_heads(wt, per_head):
    lead = wt.shape[:-1]
    wt = wt.reshape(lead + (MLA_HEADS, per_head))
    wt = jnp.pad(wt, [(0, 0)] * len(lead) + [(0, 0), (0, HEAD_PAD - per_head)])
    return wt.reshape(lead + (MLA_PAD,))


def _s5_tables(lam_re, lam_im, b_re, b_im, c_re, c_im, d, log_step):
    hp = lax.Precision.HIGHEST
    depth, g, p = lam_re.shape
    h, t = SSM_GROUP, S5_CHUNK
    step = jnp.exp(log_step)[..., None]
    mag = jnp.exp(lam_re * step)
    ab_re, ab_im = mag * jnp.cos(lam_im * step), mag * jnp.sin(lam_im * step)
    den = lam_re * lam_re + lam_im * lam_im
    nr = ab_re - 1.0
    coef_re = (nr * lam_re + ab_im * lam_im) / den
    coef_im = (ab_im * lam_re - nr * lam_im) / den
    bb_re = coef_re[..., None] * b_re - coef_im[..., None] * b_im
    bb_im = coef_re[..., None] * b_im + coef_im[..., None] * b_re
    n = jnp.arange(t + 1, dtype=F32)[:, None, None, None]
    pmag = jnp.exp(n * lam_re * step)
    pw_re, pw_im = pmag * jnp.cos(n * lam_im * step), pmag * jnp.sin(n * lam_im * step)
    pb_re = pw_re[:t, ..., None] * bb_re - pw_im[:t, ..., None] * bb_im
    pb_im = pw_re[:t, ..., None] * bb_im + pw_im[:t, ..., None] * bb_re
    kern = (jnp.einsum('lgop,nlgpi->nlgoi', c_re, pb_re, precision=hp)
            - jnp.einsum('lgop,nlgpi->nlgoi', c_im, pb_im, precision=hp))
    k2 = jnp.transpose(kern, (1, 2, 4, 0, 3)).reshape(depth, g, h, t * h)
    m = jnp.stack([jnp.pad(k2[..., :(t - s) * h], ((0, 0), (0, 0), (0, 0), (s * h, 0))) for s in range(t)], 2)
    m = m.reshape(depth, g, t * h, t * h)
    wr = jnp.transpose(pb_re[::-1], (1, 2, 0, 4, 3)).reshape(depth, g, t * h, p)
    wi = jnp.transpose(pb_im[::-1], (1, 2, 0, 4, 3)).reshape(depth, g, t * h, p)
    wis = jnp.concatenate([wr, wi, wi, wr], -1)
    cp_re = c_re[None] * pw_re[1:, :, :, None, :] - c_im[None] * pw_im[1:, :, :, None, :]
    cp_im = c_re[None] * pw_im[1:, :, :, None, :] + c_im[None] * pw_re[1:, :, :, None, :]
    so_re = jnp.transpose(cp_re, (1, 2, 4, 0, 3)).reshape(depth, g, p, t * h)
    so_im = jnp.transpose(-cp_im, (1, 2, 4, 0, 3)).reshape(depth, g, p, t * h)
    wso = jnp.concatenate([so_re, so_im], 2)
    a_re, a_im = pw_re[t], pw_im[t]
    av = jnp.stack([jnp.concatenate([a_re, a_re], -1), jnp.concatenate([-a_im, a_im], -1),
                    jnp.concatenate([a_im, -a_im], -1)], 2)
    av = jnp.pad(av, ((0, 0), (0, 0), (0, 5), (0, 0)))
    dg = d.reshape(depth, g, 1, h)
    d_t = jnp.tile(dg, (1, 1, 1, t))
    eye = jnp.eye(g, dtype=F32)
    bbr = jnp.einsum('lgph,gk->lghkp', bb_re, eye).reshape(depth, g * h, g * p)
    bbi = jnp.einsum('lgph,gk->lghkp', bb_im, eye).reshape(depth, g * h, g * p)
    ccr = jnp.einsum('lghp,gk->lgpkh', c_re, eye).reshape(depth, g * p, g * h)
    cci = jnp.einsum('lghp,gk->lgpkh', -c_im, eye).reshape(depth, g * p, g * h)
    return dict(s5_m=m.astype(BF16), s5_wis=wis.astype(BF16), s5_wso=wso.astype(BF16), s5_av=av, s5_d=d_t,
                s5_bb=jnp.concatenate([bbr, bbi], -1).astype(BF16),
                s5_cc=jnp.concatenate([ccr, cci], 1).astype(BF16),
                s5_ar=ab_re.reshape(depth, 1, g * p), s5_ai=ab_im.reshape(depth, 1, g * p),
                s5_dflat=d.reshape(depth, 1, g * h))


def _rope_tables(pos):
    half = MLA_ROPE_DIM // 2
    inv = 1.0 / (ROPE_THETA ** (jnp.arange(0, MLA_ROPE_DIM, 2, dtype=F32) / MLA_ROPE_DIM))
    ang = pos.astype(F32)[:, None] * inv[None, :]
    cos, sin = jnp.cos(ang), jnp.sin(ang)
    n = pos.shape[0]
    one, zero = jnp.ones((n, ROPE_LANE0), F32), jnp.zeros((n, ROPE_LANE0), F32)
    z16, z32 = jnp.zeros((n, half), F32), jnp.zeros((n, LANES - ROPE_LANE0 - MLA_ROPE_DIM), F32)
    rc = jnp.concatenate([one, cos, cos, z32], 1)
    rs1 = jnp.concatenate([zero, -sin, z16, z32], 1)
    rs2 = jnp.concatenate([zero, z16, sin, z32], 1)
    return rc, rs1, rs2


def _group_patterns(tt):
    r = np.arange(tt)
    same = (r[:, None] // HG_GROUP) == (r[None, :] // HG_GROUP)
    lcum = same & (r[None, :] <= r[:, None])
    return jnp.asarray(lcum, BF16), jnp.asarray(same, BF16)


def _prepare(p):
    depth = p["w_in"].shape[0]
    w = {}
    w["ffn_w_gu"] = p["ffn_w_gu"].astype(BF16)
    w["ffn_w_down"] = p["ffn_w_down"].astype(BF16)
    w_in = p["w_in"]
    n_main = Z_CKV[1]
    zpad = lambda n: jnp.zeros((depth, D_MODEL, n), w_in.dtype)
    w["w_in"] = jnp.concatenate(
        [w_in[:, :, :n_main], zpad(ROPE_LANE0), w_in[:, :, n_main:], zpad(LANES - ROPE_LANE0 - MLA_ROPE_DIM)],
        -1).astype(BF16)
    w["w_uq"] = _pad_heads(p["mla_w_uq"], MLA_NOPE_DIM + MLA_ROPE_DIM).astype(BF16)
    w["w_uk"] = _pad_heads(p["mla_w_uk"], MLA_NOPE_DIM).astype(BF16)
    w["w_uv"] = _pad_heads(p["mla_w_uv"], MLA_V_DIM).astype(BF16)
    w["w_uv_t"] = jnp.swapaxes(p["mla_w_uv"], 1, 2).astype(BF16)
    w["q_norm"] = p["mla_q_norm"].reshape(depth, 1, MLA_Q_LORA)
    w["kv_norm"] = p["mla_kv_norm"].reshape(depth, 1, MLA_KV_LORA)
    w["w_glu"] = p["ssm_w_glu"].astype(BF16)
    w["n_ssm"] = p["norm_ssm"].reshape(depth, 1, SSM_WIDTH)
    w["n_hg"] = p["norm_hgrn"].reshape(depth, 1, HG_WIDTH)
    w["n_mla"] = jnp.broadcast_to(p["norm_mla"][:, :, None], (depth, MLA_HEADS * MLA_V_DIM, LANES))
    w_out = p["w_out"]
    split = SSM_WIDTH + HG_WIDTH
    w["w_out_a"] = w_out[:, :split].astype(BF16)
    w["w_out_b"] = w_out[:, split:].astype(BF16)
    w.update(_s5_tables(p["ssm_lambda_re"].astype(F32), p["ssm_lambda_im"].astype(F32),
                        p["ssm_b_re"].astype(F32), p["ssm_b_im"].astype(F32),
                        p["ssm_c_re"].astype(F32), p["ssm_c_im"].astype(F32),
                        p["ssm_d"].astype(F32), p["ssm_log_step"].astype(F32)))
    lane = np.arange(SSM_WIDTH)
    step_of, chan_of = lane // SSM_GROUP, lane % SSM_GROUP
    t_idx = np.arange(S5_CHUNK)[:, None, None]
    fwd = (step_of[None, None, :] == t_idx) & (chan_of[None, None, :] == chan_of[None, :, None])
    w["s5_sel_fwd"] = jnp.asarray(fwd, BF16)
    w["s5_sel_bwd"] = jnp.asarray(np.swapaxes(fwd, 1, 2), BF16)
    sm = jax.nn.softmax(p["hgrn_lb_logits"].astype(F32), axis=0)
    lb = jnp.clip(jnp.cumsum(sm, axis=0) - sm[0:1], 0.0, 1.0 - 1e-6)
    rows = jnp.stack([jnp.maximum(lb, LB_FLOOR), 1.0 - lb], 1)
    w["hg_lbp"] = jnp.pad(rows, ((0, 0), (0, 6), (0, 0)))
    head = np.arange(HG_WIDTH) // HG_DK
    same_head = head[:, None] == head[None, :]
    w["hg_e"] = jnp.asarray(same_head, BF16)
    w["hg_bd"] = jnp.asarray(same_head, F32)
    w["hg_e64"] = jnp.asarray(same_head / HG_DK, BF16)
    return w


def _hg_state_in(s):
    b = s.shape[0]
    eye = jnp.eye(HG_HEADS, dtype=s.dtype)
    return jnp.einsum('bhkv,hg->bhvgk', s, eye).reshape(b, HG_WIDTH, HG_WIDTH)


def _hg_state_out(st):
    b = st.shape[0]
    st = st.reshape(b, HG_HEADS, HG_DK, HG_HEADS, HG_DK)
    diag = jnp.stack([st[:, h, :, h, :] for h in range(HG_HEADS)], 1)
    return jnp.transpose(diag, (0, 1, 3, 2))


def _run_prompt(x, grp, w, p, rope):
    batch, seq = grp.batch, grp.seq
    depth = p["w_in"].shape[0]
    nc = seq // S5_CHUNK
    tt = min(256, seq)
    tq = min(1024, seq)
    ln_g = p["ln_g"].reshape(depth, 3, 1, D_MODEL)
    ln_b = p["ln_b"].reshape(depth, 3, 1, D_MODEL)
    x0 = jnp.zeros((SSM_GROUPS, batch, 2 * LANES), F32)
    s0 = jnp.zeros((batch, HG_WIDTH, HG_WIDTH), F32)
    ckv_l, kpe_l, sre_l, sim_l, hg_l = [], [], [], [], []
    for l in range(depth):
        x = _ffn_call(grp, x, l, 0, 0, w["ffn_w_gu"], w["ffn_w_down"], ln_g, ln_b)
        u, zh, qh, kh, vt, ckv, kpe = _mix_in_call(grp, x, l, w, rope)
        u_g = _s5_relayout_call(_s5_pack_kernel, u, w["s5_sel_fwd"], grp, True)
        y_g, xf = _s5_call(u_g, x0, w, l, batch, nc)
        y = _s5_relayout_call(_s5_unpack_kernel, y_g.reshape(u_g.shape), w["s5_sel_bwd"], grp, False)
        o_h, st = _hgrn_call(zh, s0, w, l, batch, seq, tt, tt)
        o_m = _flash_call(qh, kh, vt, batch, seq, tq)
        x = _mix_out_call(grp, x, l, y, zh, o_h, o_m, w, ln_g, ln_b)
        x = _ffn_call(grp, x, l, 1, 2, w["ffn_w_gu"], w["ffn_w_down"], ln_g, ln_b)
        ckv_l.append(ckv.reshape(batch, seq, MLA_KV_LORA))
        kpe_l.append(kpe.reshape(batch, seq, MLA_ROPE_DIM))
        xf = jnp.transpose(xf, (1, 0, 2))
        sre_l.append(xf[..., :SSM_STATE])
        sim_l.append(xf[..., SSM_STATE:])
        hg_l.append(_hg_state_out(st))
    return (x.reshape(batch, seq, D_MODEL), jnp.stack(ckv_l), jnp.stack(kpe_l), jnp.stack(sre_l),
            jnp.stack(sim_l), jnp.stack(hg_l))


def _run_sample(x, grp, w, p, rope, cache_ckv, cache_kpe, page_table, ssm_re, ssm_im, hg_state):
    batch = grp.batch
    depth = p["w_in"].shape[0]
    ln_g = p["ln_g"].reshape(depth, 3, 1, D_MODEL)
    ln_b = p["ln_b"].reshape(depth, 3, 1, D_MODEL)
    n = SSM_GROUPS * SSM_STATE
    lane = np.arange(MLA_PAD)
    sel = np.zeros((MLA_PAD, MLA_ROPE_DIM), np.float32)
    for i in range(MLA_ROPE_DIM):
        sel[(lane % HEAD_PAD) == ROPE_LANE0 + i, i] = 1.0
    sel = jnp.asarray(sel, BF16)
    hm = jnp.asarray((lane[None, :] // HEAD_PAD) == np.arange(MLA_HEADS)[:, None], F32)
    cache_kpe = jnp.swapaxes(cache_kpe, 2, 3)
    ckv_l, kpe_l, sre_l, sim_l, hg_l = [], [], [], [], []
    for l in range(depth):
        x = _ffn_call(grp, x, l, 0, 0, w["ffn_w_gu"], w["ffn_w_down"], ln_g, ln_b)
        u, zh, qh, kh, _, ckv, kpe = _mix_in_call(grp, x, l, w, rope)
        y, nr, ni = _s5_step_call(u, ssm_re[l].reshape(batch, n), ssm_im[l].reshape(batch, n), w, l)
        zh_pad = jnp.pad(zh[:, None, :], ((0, 0), (0, HG_GROUP - 1), (0, 0))).reshape(batch * HG_GROUP, -1)
        o_hp, st = _hgrn_call(zh_pad, _hg_state_in(hg_state[l]), w, l, batch, HG_GROUP, HG_GROUP, 1)
        o_h = o_hp.reshape(batch, HG_GROUP, HG_WIDTH)[:, 0]
        o_m = _decode_call(page_table, qh, ckv, kpe, cache_ckv, cache_kpe, w, l, sel, hm)
        o_m = o_m.reshape(batch, MLA_HEADS, HEAD_PAD)[:, :, :MLA_V_DIM].reshape(batch, -1).T[None]
        x = _mix_out_call(grp, x, l, y, zh, o_h, o_m, w, ln_g, ln_b)
        x = _ffn_call(grp, x, l, 1, 2, w["ffn_w_gu"], w["ffn_w_down"], ln_g, ln_b)
        ckv_l.append(ckv.reshape(batch, 1, MLA_KV_LORA))
        kpe_l.append(kpe.reshape(batch, 1, MLA_ROPE_DIM))
        sre_l.append(nr.reshape(batch, SSM_GROUPS, SSM_STATE))
        sim_l.append(ni.reshape(batch, SSM_GROUPS, SSM_STATE))
        hg_l.append(_hg_state_out(st))
    return (x.reshape(batch, 1, D_MODEL), jnp.stack(ckv_l), jnp.stack(kpe_l), jnp.stack(sre_l),
            jnp.stack(sim_l), jnp.stack(hg_l))


def kernel(x_prompt, x_sample, cache_kv_latent, cache_k_rope, state_ssm_re, state_ssm_im, state_hgrn, page_table, c_prompt, c_sample, w_ada, b_ada, ln_g, ln_b, ffn_w_gu, ffn_w_down, w_in, w_out, ssm_lambda_re, ssm_lambda_im, ssm_b_re, ssm_b_im, ssm_c_re, ssm_c_im, ssm_d, ssm_log_step, ssm_w_glu, norm_ssm, hgrn_lb_logits, norm_hgrn, mla_q_norm, mla_w_uq, mla_kv_norm, mla_w_uk, mla_w_uv, norm_mla):
    p = dict(ln_g=ln_g, ln_b=ln_b, ffn_w_gu=ffn_w_gu, ffn_w_down=ffn_w_down, w_in=w_in, w_out=w_out,
             ssm_lambda_re=ssm_lambda_re, ssm_lambda_im=ssm_lambda_im, ssm_b_re=ssm_b_re, ssm_b_im=ssm_b_im,
             ssm_c_re=ssm_c_re, ssm_c_im=ssm_c_im, ssm_d=ssm_d, ssm_log_step=ssm_log_step, ssm_w_glu=ssm_w_glu,
             norm_ssm=norm_ssm, hgrn_lb_logits=hgrn_lb_logits, norm_hgrn=norm_hgrn, mla_q_norm=mla_q_norm,
             mla_w_uq=mla_w_uq, mla_kv_norm=mla_kv_norm, mla_w_uk=mla_w_uk, mla_w_uv=mla_w_uv, norm_mla=norm_mla)
    depth = w_in.shape[0]
    assert depth == DEPTH_ and x_sample.shape[1] == 1
    bp, seq, _ = x_prompt.shape
    bs = x_sample.shape[0]
    past_len = page_table.shape[1] * PAGE_SIZE
    w = _prepare(p)
    w["hg_lcum"], w["hg_lall"] = {}, {}
    for tt in {min(256, seq), HG_GROUP}:
        w["hg_lcum"][tt], w["hg_lall"][tt] = _group_patterns(tt)

    mod = _ada_call(jnp.concatenate([c_prompt, c_sample], 0), w_ada, b_ada)
    mod = mod.reshape(depth, bp + bs, 9, D_MODEL)
    grp_p = _Group(mod[:, :bp], bp, seq, min(512, seq))
    grp_s = _Group(mod[:, bp:], bs, 1, bs)

    rope_p = _rope_tables(jnp.arange(seq))
    rope_s = _rope_tables(past_len + jnp.arange(1))
    y_p, ckv_p, kpe_p, sre_p, sim_p, hg_p = _run_prompt(x_prompt.reshape(bp * seq, D_MODEL), grp_p, w, p, rope_p)
    y_s, ckv_s, kpe_s, sre_s, sim_s, hg_s = _run_sample(
        x_sample.reshape(bs, D_MODEL), grp_s, w, p, rope_s, cache_kv_latent, cache_k_rope, page_table,
        state_ssm_re, state_ssm_im, state_hgrn)
    return (y_p, y_s, ckv_p, ckv_s, kpe_p, kpe_s, sre_p, sre_s, sim_p, sim_s, hg_p, hg_s)
```
